```python
import math, functools
import jax, jax.numpy as jnp
from jax import lax
import numpy as np

D_MODEL = 2048
BATCH = 1
SEQ = 16384
DEPTH = 1
DEC_BATCH = 32
DEC_SEQ = 4
PAST_LEN = 16384
PAGE_SIZE = 128

HEAD_DIM = 128
A_HEADS = 8
A_KV_HEADS = 4
B_HEADS = 8
B_KV_HEADS = 4
IDX_HEADS = 16
IDX_DIM = 64
DSA_TOPK = 256
MOBA_BLOCK = 256
MOBA_TOPK = 3
MEM_TOKENS = 256
MEM_HEADS = 4
MEM_HEAD_DIM = 128
D_FF = 4 * D_MODEL
ROPE_THETA = 500000.0
Q_BLOCK = 128
NORM_EPS = 1e-6

A_Q_COLS = A_HEADS * HEAD_DIM
A_KV_COLS = A_KV_HEADS * HEAD_DIM
I_Q_COLS = IDX_HEADS * IDX_DIM
I_W_COLS = IDX_HEADS
I_K_COLS = IDX_DIM
B_Q_COLS = B_HEADS * HEAD_DIM
B_KV_COLS = B_KV_HEADS * HEAD_DIM
IN_COLS = A_Q_COLS + 2 * A_KV_COLS + I_Q_COLS + I_W_COLS + I_K_COLS + B_Q_COLS + 2 * B_KV_COLS
MIX_WIDTH = A_Q_COLS + B_Q_COLS
MEM_WIDTH = MEM_HEADS * MEM_HEAD_DIM

kernel_name = "hybrid_dsa_moba_step"

F32 = jnp.float32


def rms_norm(x, g):
    x32 = x.astype(F32)
    y = x32 * lax.rsqrt(jnp.mean(x32 * x32, axis=-1, keepdims=True) + NORM_EPS)
    return (y * g.astype(F32)).astype(x.dtype)


def layer_norm(x, g, b):
    x32 = x.astype(F32)
    xc = x32 - jnp.mean(x32, axis=-1, keepdims=True)
    var = jnp.mean(xc * xc, axis=-1, keepdims=True)
    return (xc * lax.rsqrt(var + NORM_EPS) * g.astype(F32) + b.astype(F32)).astype(x.dtype)


def partial_rope(x, pos):
    dh = x.shape[-1]
    rot = dh // 4
    half = rot // 2
    inv = jnp.power(jnp.float32(ROPE_THETA), -jnp.arange(half, dtype=F32) * (2.0 / rot))
    ang = pos.astype(F32)[:, None] * inv[None, :]
    cos = jnp.cos(ang)[:, None, :]
    sin = jnp.sin(ang)[:, None, :]
    x32 = x.astype(F32)
    x1 = x32[..., :half]
    x2 = x32[..., half:rot]
    out = jnp.concatenate([x1 * cos - x2 * sin, x2 * cos + x1 * sin, x32[..., rot:]], axis=-1)
    return out.astype(x.dtype)


def _split_points():
    sizes = (A_Q_COLS, A_KV_COLS, A_KV_COLS, I_Q_COLS, I_W_COLS, I_K_COLS, B_Q_COLS, B_KV_COLS, B_KV_COLS)
    pts = []
    acc = 0
    for s in sizes[:-1]:
        acc += s
        pts.append(acc)
    return pts


def project_mixers(h, w_in, g_kidx, b_kidx, pos):
    b, t, _ = h.shape
    p = jnp.einsum('btd,dc->btc', h, w_in)
    qa, ka, va, qi, wi, ki, qb, kb, vb = jnp.split(p, _split_points(), axis=-1)
    qa = partial_rope(qa.reshape(b, t, A_HEADS, HEAD_DIM), pos)
    ka = partial_rope(ka.reshape(b, t, A_KV_HEADS, HEAD_DIM), pos)
    va = va.reshape(b, t, A_KV_HEADS, HEAD_DIM)
    qi = partial_rope(qi.reshape(b, t, IDX_HEADS, IDX_DIM), pos)
    wi = wi * (IDX_HEADS ** -0.5 * IDX_DIM ** -0.5)
    ki = partial_rope(layer_norm(ki, g_kidx, b_kidx)[:, :, None, :], pos)[:, :, 0, :]
    qb = partial_rope(qb.reshape(b, t, B_HEADS, HEAD_DIM), pos)
    kb = partial_rope(kb.reshape(b, t, B_KV_HEADS, HEAD_DIM), pos)
    vb = vb.reshape(b, t, B_KV_HEADS, HEAD_DIM)
    return qa, ka, va, qi, wi, ki, qb, kb, vb


def dsa_core(q, qi, wi, qpos, kidx, fetch_kv, topk):
    t = q.shape[0]
    n_keys = kidx.shape[0]
    dots = jnp.einsum('thd,sd->ths', qi, kidx).astype(F32)
    score = jnp.einsum('th,ths->ts', wi.astype(F32), jax.nn.relu(dots))
    kpos = jnp.arange(n_keys)
    score = jnp.where(kpos[None, :] <= qpos[:, None], score, -jnp.inf)
    _, idx = lax.top_k(score, topk)
    valid = idx <= qpos[:, None]
    k, v = fetch_kv(idx)
    qg = q.reshape(t, A_KV_HEADS, A_HEADS // A_KV_HEADS, HEAD_DIM)
    s = jnp.einsum('tkgd,tskd->tkgs', qg, k).astype(F32) * (HEAD_DIM ** -0.5)
    s = jnp.where(valid[:, None, None, :], s, -jnp.inf)
    p = jax.nn.softmax(s, axis=-1).astype(v.dtype)
    o = jnp.einsum('tkgs,tskd->tkgd', p, v)
    return o.reshape(t, A_Q_COLS)


def moba_core(q, qpos, means, fetch_blocks, ksel, own_k, own_v, own_pos):
    t = q.shape[0]
    kvh = jnp.arange(B_HEADS) // (B_HEADS // B_KV_HEADS)
    scale = HEAD_DIM ** -0.5
    qblk = qpos // MOBA_BLOCK
    own_v_h = own_v[:, kvh]
    s_own = jnp.einsum('thd,rhd->thr', q, own_k[:, kvh]).astype(F32) * scale
    own_ok = (own_pos[None, :] <= qpos[:, None]) & ((own_pos[None, :] // MOBA_BLOCK) == qblk[:, None])
    s_own = jnp.where(own_ok[:, None, :], s_own, -jnp.inf)
    if ksel == 0:
        p = jax.nn.softmax(s_own, axis=-1).astype(own_v.dtype)
        o = jnp.einsum('thr,rhd->thd', p, own_v_h)
        return o.reshape(t, B_Q_COLS)
    gate = jnp.einsum('thd,nhd->thn', q, means[:, kvh]).astype(F32)
    nb = jnp.arange(means.shape[0])
    gate = jnp.where(nb[None, None, :] < qblk[:, None, None], gate, -jnp.inf)
    _, sel = lax.top_k(gate, ksel)
    valid = sel < qblk[:, None, None]
    kp, vp = fetch_blocks(sel, kvh)
    s_past = jnp.einsum('thd,thnkd->thnk', q, kp).astype(F32) * scale
    s_past = jnp.where(valid[..., None], s_past, -jnp.inf).reshape(t, B_HEADS, ksel * MOBA_BLOCK)
    p = jax.nn.softmax(jnp.concatenate([s_past, s_own], axis=-1), axis=-1)
    pp = p[..., :ksel * MOBA_BLOCK].reshape(t, B_HEADS, ksel, MOBA_BLOCK).astype(vp.dtype)
    po = p[..., ksel * MOBA_BLOCK:].astype(own_v.dtype)
    o = jnp.einsum('thnk,thnkd->thd', pp, vp) + jnp.einsum('thr,rhd->thd', po, own_v_h)
    return o.reshape(t, B_Q_COLS)


def prompt_mixers(qa, ka, va, qi, wi, ki, qb, kb, vb):
    s_len = qa.shape[0]
    nqb = s_len // Q_BLOCK
    topk = min(DSA_TOPK, s_len // 4)
    nbf = s_len // MOBA_BLOCK
    ksel = min(MOBA_TOPK, (s_len - 1) // MOBA_BLOCK)
    nb_pad = -(-s_len // MOBA_BLOCK)
    pad = nb_pad * MOBA_BLOCK - s_len
    kb_pad = jnp.pad(kb, ((0, pad), (0, 0), (0, 0)))
    vb_pad = jnp.pad(vb, ((0, pad), (0, 0), (0, 0)))
    kb_blk = kb_pad.reshape(nb_pad, MOBA_BLOCK, B_KV_HEADS, HEAD_DIM).transpose(0, 2, 1, 3)
    vb_blk = vb_pad.reshape(nb_pad, MOBA_BLOCK, B_KV_HEADS, HEAD_DIM).transpose(0, 2, 1, 3)
    means = jnp.mean(kb_blk[:nbf].astype(F32), axis=2).astype(kb.dtype)

    def fetch_a(idx):
        return ka[idx], va[idx]

    def fetch_b(sel, kvh):
        hh = kvh[None, :, None]
        return kb_blk[sel, hh], vb_blk[sel, hh]

    def step(args):
        qa_b, qi_b, wi_b, qb_b, t0 = args
        qpos = t0 + jnp.arange(Q_BLOCK)
        oa = dsa_core(qa_b, qi_b, wi_b, qpos, ki, fetch_a, topk)
        start = (t0 // MOBA_BLOCK) * MOBA_BLOCK
        own_k = lax.dynamic_slice_in_dim(kb_pad, start, MOBA_BLOCK, axis=0)
        own_v = lax.dynamic_slice_in_dim(vb_pad, start, MOBA_BLOCK, axis=0)
        own_pos = start + jnp.arange(MOBA_BLOCK)
        ob = moba_core(qb_b, qpos, means, fetch_b, ksel, own_k, own_v, own_pos)
        return jnp.concatenate([oa, ob], axis=-1)

    xs = (qa.reshape(nqb, Q_BLOCK, A_HEADS, HEAD_DIM),
          qi.reshape(nqb, Q_BLOCK, IDX_HEADS, IDX_DIM),
          wi.reshape(nqb, Q_BLOCK, IDX_HEADS),
          qb.reshape(nqb, Q_BLOCK, B_HEADS, HEAD_DIM),
          jnp.arange(nqb) * Q_BLOCK)
    out = lax.map(step, xs)
    return out.reshape(s_len, MIX_WIDTH)


def sample_mixers(layer, cache_k_a, cache_v_a, cache_kidx, cache_k_b, cache_v_b,
                  pt, qa, ka_n, va_n, qi, wi, ki_n, qb, kb_n, vb_n):
    t = qa.shape[0]
    n_pages = pt.shape[0]
    past = n_pages * PAGE_SIZE
    qpos = past + jnp.arange(t)
    topk = min(DSA_TOPK, (past + t) // 4)
    ki_all = jnp.concatenate([cache_kidx[layer, pt].reshape(past, IDX_DIM), ki_n], axis=0)

    def fetch_a(idx):
        is_past = (idx < past)[..., None, None]
        ip = jnp.minimum(idx, past - 1)
        phys = pt[ip // PAGE_SIZE]
        off = ip % PAGE_SIZE
        inew = jnp.clip(idx - past, 0, t - 1)
        k = jnp.where(is_past, cache_k_a[layer, phys, off], ka_n[inew])
        v = jnp.where(is_past, cache_v_a[layer, phys, off], va_n[inew])
        return k, v

    oa = dsa_core(qa, qi, wi, qpos, ki_all, fetch_a, topk)
    ppb = MOBA_BLOCK // PAGE_SIZE
    nbp = past // MOBA_BLOCK
    ksel = min(MOBA_TOPK, nbp)
    kb_full = cache_k_b[layer, pt[:nbp * ppb]].reshape(nbp, MOBA_BLOCK, B_KV_HEADS, HEAD_DIM)
    means = jnp.mean(kb_full.astype(F32), axis=1).astype(kb_n.dtype)

    def fetch_b(sel, kvh):
        pages = pt[sel[..., None] * ppb + jnp.arange(ppb)]
        hh = kvh[None, :, None, None]
        k = cache_k_b[layer, pages, :, hh].reshape(t, B_HEADS, sel.shape[-1], MOBA_BLOCK, HEAD_DIM)
        v = cache_v_b[layer, pages, :, hh].reshape(t, B_HEADS, sel.shape[-1], MOBA_BLOCK, HEAD_DIM)
        return k, v

    r = past - nbp * MOBA_BLOCK
    if r > 0:
        own_pages = pt[nbp * ppb: nbp * ppb + r // PAGE_SIZE]
        own_k = jnp.concatenate([cache_k_b[layer, own_pages].reshape(r, B_KV_HEADS, HEAD_DIM), kb_n], axis=0)
        own_v = jnp.concatenate([cache_v_b[layer, own_pages].reshape(r, B_KV_HEADS, HEAD_DIM), vb_n], axis=0)
    else:
        own_k = kb_n
        own_v = vb_n
    own_pos = jnp.arange(past - r, past + t)
    ob = moba_core(qb, qpos, means, fetch_b, ksel, own_k, own_v, own_pos)
    return jnp.concatenate([oa, ob], axis=-1)


def memory_kv(mem, g_mem, w_mkv):
    b, m, _ = mem.shape
    kv = jnp.einsum('bmd,dc->bmc', rms_norm(mem, g_mem), w_mkv)
    k, v = jnp.split(kv, 2, axis=-1)
    return k.reshape(b, m, MEM_HEADS, MEM_HEAD_DIM), v.reshape(b, m, MEM_HEADS, MEM_HEAD_DIM)


def cross_attend(h, mk, mv, w_mq, w_mo):
    b, t, _ = h.shape
    q = jnp.einsum('btd,dc->btc', h, w_mq).reshape(b, t, MEM_HEADS, MEM_HEAD_DIM)
    s = jnp.einsum('bthd,bmhd->bhtm', q, mk).astype(F32) * (MEM_HEAD_DIM ** -0.5)
    p = jax.nn.softmax(s, axis=-1).astype(mv.dtype)
    o = jnp.einsum('bhtm,bmhd->bthd', p, mv).reshape(b, t, MEM_WIDTH)
    return jnp.einsum('btc,cd->btd', o, w_mo)


def sq_relu_mlp(h, w_up, w_down):
    u = jax.nn.relu(jnp.einsum('btd,df->btf', h, w_up))
    return jnp.einsum('btf,fd->btd', u * u, w_down)


def setup_inputs(seed: int = 0) -> dict:
    key = jax.random.key(seed)
    ks = jax.random.split(key, 32)
    n_pages = PAST_LEN // PAGE_SIZE
    n_phys = (5 * DEC_BATCH * n_pages + 3) // 4
    nrm = lambda k, shape, s=1.0: jax.random.normal(k, shape, F32) * s
    gain = lambda k, shape: 1.0 + 0.01 * jax.random.normal(k, shape, F32)
    page_table = jax.random.permutation(ks[9], n_phys)[:DEC_BATCH * n_pages].reshape(DEC_BATCH, n_pages).astype(jnp.int32)
    return {
        'x_prompt': nrm(ks[0], (BATCH, SEQ, D_MODEL)),
        'x_sample': nrm(ks[1], (DEC_BATCH, DEC_SEQ, D_MODEL)),
        'cache_k_a': nrm(ks[2], (DEPTH, n_phys, PAGE_SIZE, A_KV_HEADS, HEAD_DIM)),
        'cache_v_a': nrm(ks[3], (DEPTH, n_phys, PAGE_SIZE, A_KV_HEADS, HEAD_DIM)),
        'cache_kidx': nrm(ks[4], (DEPTH, n_phys, PAGE_SIZE, IDX_DIM)),
        'cache_k_b': nrm(ks[5], (DEPTH, n_phys, PAGE_SIZE, B_KV_HEADS, HEAD_DIM)),
        'cache_v_b': nrm(ks[6], (DEPTH, n_phys, PAGE_SIZE, B_KV_HEADS, HEAD_DIM)),
        'cache_mem_k': nrm(ks[7], (DEPTH, DEC_BATCH, MEM_TOKENS, MEM_HEADS, MEM_HEAD_DIM)),
        'cache_mem_v': nrm(ks[8], (DEPTH, DEC_BATCH, MEM_TOKENS, MEM_HEADS, MEM_HEAD_DIM)),
        'page_table': page_table,
        'mem_prompt': nrm(ks[10], (BATCH, MEM_TOKENS, D_MODEL)),
        'g_mix': gain(ks[11], (DEPTH, D_MODEL)),
        'w_in': nrm(ks[12], (DEPTH, D_MODEL, IN_COLS), D_MODEL ** -0.5),
        'g_kidx': gain(ks[13], (DEPTH, IDX_DIM)),
        'b_kidx': nrm(ks[14], (DEPTH, IDX_DIM), 0.01),
        'w_out': nrm(ks[15], (DEPTH, MIX_WIDTH, D_MODEL), MIX_WIDTH ** -0.5),
        'g_cross': gain(ks[16], (DEPTH, D_MODEL)),
        'w_mq': nrm(ks[17], (DEPTH, D_MODEL, MEM_WIDTH), D_MODEL ** -0.5),
        'g_mem': gain(ks[18], (DEPTH, D_MODEL)),
        'w_mkv': nrm(ks[19], (DEPTH, D_MODEL, 2 * MEM_WIDTH), D_MODEL ** -0.5),
        'w_mo': nrm(ks[20], (DEPTH, MEM_WIDTH, D_MODEL), MEM_WIDTH ** -0.5),
        'g_ffn': gain(ks[21], (DEPTH, D_MODEL)),
        'w_up': nrm(ks[22], (DEPTH, D_MODEL, D_FF), D_MODEL ** -0.5),
        'w_down': nrm(ks[23], (DEPTH, D_FF, D_MODEL), D_FF ** -0.5),
        'g_final': gain(ks[24], (D_MODEL,)),
    }


def reference(x_prompt, x_sample, cache_k_a, cache_v_a, cache_kidx, cache_k_b, cache_v_b,
              cache_mem_k, cache_mem_v, page_table, mem_prompt, g_mix, w_in, g_kidx, b_kidx,
              w_out, g_cross, w_mq, g_mem, w_mkv, w_mo, g_ffn, w_up, w_down, g_final):
    s_len = x_prompt.shape[1]
    t_new = x_sample.shape[1]
    past = page_table.shape[1] * PAGE_SIZE
    pos_p = jnp.arange(s_len)
    pos_s = past + jnp.arange(t_new)
    xp = x_prompt
    xs = x_sample
    pk_a, pv_a, pkidx, pk_b, pv_b, pmk, pmv = [], [], [], [], [], [], []
    sk_a, sv_a, skidx, sk_b, sv_b = [], [], [], [], []
    for l in range(DEPTH):
        qa, ka, va, qi, wi, ki, qb, kb, vb = project_mixers(rms_norm(xp, g_mix[l]), w_in[l], g_kidx[l], b_kidx[l], pos_p)
        mix = jax.vmap(prompt_mixers)(qa, ka, va, qi, wi, ki, qb, kb, vb)
        xp = xp + jnp.einsum('btc,cd->btd', mix, w_out[l])
        mk, mv = memory_kv(mem_prompt, g_mem[l], w_mkv[l])
        xp = xp + cross_attend(rms_norm(xp, g_cross[l]), mk, mv, w_mq[l], w_mo[l])
        xp = xp + sq_relu_mlp(rms_norm(xp, g_ffn[l]), w_up[l], w_down[l])
        pk_a.append(ka); pv_a.append(va); pkidx.append(ki); pk_b.append(kb); pv_b.append(vb)
        pmk.append(mk); pmv.append(mv)
        qa, ka, va, qi, wi, ki, qb, kb, vb = project_mixers(rms_norm(xs, g_mix[l]), w_in[l], g_kidx[l], b_kidx[l], pos_s)
        run = jax.vmap(functools.partial(sample_mixers, l), in_axes=(None,) * 5 + (0,) * 10)
        mix = run(cache_k_a, cache_v_a, cache_kidx, cache_k_b, cache_v_b,
                  page_table, qa, ka, va, qi, wi, ki, qb, kb, vb)
        xs = xs + jnp.einsum('btc,cd->btd', mix, w_out[l])
        xs = xs + cross_attend(rms_norm(xs, g_cross[l]), cache_mem_k[l], cache_mem_v[l], w_mq[l], w_mo[l])
        xs = xs + sq_relu_mlp(rms_norm(xs, g_ffn[l]), w_up[l], w_down[l])
        sk_a.append(ka); sv_a.append(va); skidx.append(ki); sk_b.append(kb); sv_b.append(vb)
    y_prompt = rms_norm(xp, g_final)
    y_sample = rms_norm(xs, g_final)
    return (y_prompt, y_sample,
            jnp.stack(pk_a), jnp.stack(pv_a), jnp.stack(pkidx), jnp.stack(pk_b), jnp.stack(pv_b),
            jnp.stack(pmk), jnp.stack(pmv),
            jnp.stack(sk_a), jnp.stack(sv_a), jnp.stack(skidx), jnp.stack(sk_b), jnp.stack(sv_b))
```

```python
import functools

import jax
import jax.numpy as jnp
from jax import lax
from jax.experimental import pallas as pl
from jax.experimental.pallas import tpu as pltpu

F32 = jnp.float32
BF16 = jnp.bfloat16
I32 = jnp.int32

D_MODEL = 2048
HEAD_DIM = 128
A_HEADS = 8
A_KV_HEADS = 4
B_HEADS = 8
B_KV_HEADS = 4
IDX_HEADS = 16
IDX_DIM = 64
DSA_TOPK = 256
MOBA_BLOCK = 256
MOBA_TOPK = 3
MEM_TOKENS = 256
MEM_HEADS = 4
MEM_HEAD_DIM = 128
PAGE_SIZE = 128
ROPE_THETA = 500000.0
NORM_EPS = 1e-6

A_Q_COLS = A_HEADS * HEAD_DIM
A_KV_COLS = A_KV_HEADS * HEAD_DIM
I_Q_COLS = IDX_HEADS * IDX_DIM
B_Q_COLS = B_HEADS * HEAD_DIM
B_KV_COLS = B_KV_HEADS * HEAD_DIM
MEM_WIDTH = MEM_HEADS * MEM_HEAD_DIM

LANES = 128
SUBLANES = 8
VMEM_LIMIT = 56 * 1024 * 1024
INT_MIN = -(2 ** 31)
NEG = -1e30

PROJ_TN = 512
COL_QA, COL_KA, COL_VA, COL_QI, COL_QB, COL_KB, COL_VB, COL_MISC = 0, 1024, 1536, 2048, 3072, 4096, 4608, 5120
SLAB_COLS = COL_MISC + PROJ_TN
MISC_WI = IDX_DIM


def _nt_dot(a, b):
    return lax.dot_general(a, b, (((1,), (1,)), ((), ())), preferred_element_type=F32)


def _rms(x, g):
    return x * lax.rsqrt(jnp.mean(x * x, axis=-1, keepdims=True) + NORM_EPS) * g


def _rope(x, cos_t, sin_t, half, period):
    width = x.shape[1]
    reps = width // LANES
    c = jnp.concatenate([cos_t] * reps, axis=1) if reps > 1 else cos_t
    s = jnp.concatenate([sin_t] * reps, axis=1) if reps > 1 else sin_t
    lane = lax.broadcasted_iota(I32, x.shape, 1)
    first = (lane & (period - 1)) < half
    up = pltpu.roll(x, width - half, axis=1)
    dn = pltpu.roll(x, half, axis=1)
    return x * c + jnp.where(first, up, dn) * s


def _proj_kernel(x_ref, g_ref, w_ref, c128_ref, s128_ref, c64_ref, s64_ref, gk_ref, bk_ref,
                 slab_ref, ka_ref, va_ref, kb_ref, vb_ref, misc_ref, h_scr):
    j = pl.program_id(1)

    @pl.when(j == 0)
    def _():
        h_scr[...] = _rms(x_ref[...], g_ref[...]).astype(BF16)

    acc = jnp.dot(h_scr[...], w_ref[...], preferred_element_type=F32)

    def rope128(v):
        return _rope(v, c128_ref[...], s128_ref[...], HEAD_DIM // 8, HEAD_DIM)

    is_q128 = (j == COL_QA // PROJ_TN) | (j == COL_QA // PROJ_TN + 1) | \
              (j == COL_QB // PROJ_TN) | (j == COL_QB // PROJ_TN + 1)

    @pl.when(is_q128)
    def _():
        slab_ref[...] = rope128(acc).astype(BF16)

    @pl.when(j == COL_KA // PROJ_TN)
    def _():
        r = rope128(acc)
        ka_ref[...] = r
        slab_ref[...] = r.astype(BF16)

    @pl.when(j == COL_KB // PROJ_TN)
    def _():
        r = rope128(acc)
        kb_ref[...] = r
        slab_ref[...] = r.astype(BF16)

    @pl.when(j == COL_VA // PROJ_TN)
    def _():
        va_ref[...] = acc
        slab_ref[...] = acc.astype(BF16)

    @pl.when(j == COL_VB // PROJ_TN)
    def _():
        vb_ref[...] = acc
        slab_ref[...] = acc.astype(BF16)

    @pl.when((j == COL_QI // PROJ_TN) | (j == COL_QI // PROJ_TN + 1))
    def _():
        slab_ref[...] = _rope(acc, c64_ref[...], s64_ref[...], IDX_DIM // 8, IDX_DIM).astype(BF16)

    @pl.when(j == COL_MISC // PROJ_TN)
    def _():
        y = acc[:, :LANES]
        lane = lax.broadcasted_iota(I32, y.shape, 1)
        is_ki = lane < IDX_DIM
        mu = jnp.sum(jnp.where(is_ki, y, 0.0), axis=-1, keepdims=True) * (1.0 / IDX_DIM)
        xc = y - mu
        var = jnp.sum(jnp.where(is_ki, xc * xc, 0.0), axis=-1, keepdims=True) * (1.0 / IDX_DIM)
        ln = xc * lax.rsqrt(var + NORM_EPS) * gk_ref[...] + bk_ref[...]
        ki = _rope(ln, c64_ref[...], s64_ref[...], IDX_DIM // 8, IDX_DIM)
        wi = y * (IDX_HEADS ** -0.5 * IDX_DIM ** -0.5)
        out = jnp.where(is_ki, ki, wi)
        misc_ref[...] = out
        slab_ref[...] = jnp.concatenate(
            [out.astype(BF16), jnp.zeros((out.shape[0], PROJ_TN - LANES), BF16)], axis=1)


def _proj(x, g, w_perm, tabs, gk, bk, tm):
    rows = x.shape[0]
    c128, s128, c64, s64 = tabs
    n_col = SLAB_COLS // PROJ_TN
    row_spec = lambda w: pl.BlockSpec((tm, w), lambda i, j: (i, 0))
    vec_spec = lambda w: pl.BlockSpec((1, w), lambda i, j: (0, 0))
    return pl.pallas_call(
        _proj_kernel,
        grid=(rows // tm, n_col),
        in_specs=[row_spec(D_MODEL), vec_spec(D_MODEL),
                  pl.BlockSpec((D_MODEL, PROJ_TN), lambda i, j: (0, j)),
                  row_spec(LANES), row_spec(LANES), row_spec(LANES), row_spec(LANES),
                  vec_spec(LANES), vec_spec(LANES)],
        out_specs=[pl.BlockSpec((tm, PROJ_TN), lambda i, j: (i, j)),
                   row_spec(A_KV_COLS), row_spec(A_KV_COLS), row_spec(B_KV_COLS), row_spec(B_KV_COLS),
                   row_spec(LANES)],
        out_shape=[jax.ShapeDtypeStruct((rows, SLAB_COLS), BF16),
                   jax.ShapeDtypeStruct((rows, A_KV_COLS), F32),
                   jax.ShapeDtypeStruct((rows, A_KV_COLS), F32),
                   jax.ShapeDtypeStruct((rows, B_KV_COLS), F32),
                   jax.ShapeDtypeStruct((rows, B_KV_COLS), F32),
                   jax.ShapeDtypeStruct((rows, LANES), F32)],
        scratch_shapes=[pltpu.VMEM((tm, D_MODEL), BF16)],
        compiler_params=pltpu.CompilerParams(
            dimension_semantics=("parallel", "arbitrary"), vmem_limit_bytes=VMEM_LIMIT),
    )(x, g, w_perm, c128, s128, c64, s64, gk, bk)


def _rope_tables(pos):
    posf = pos.astype(F32)[:, None]

    def tables(dh):
        rot = dh // 4
        half = rot // 2
        inv = jnp.power(jnp.float32(ROPE_THETA), -jnp.arange(half, dtype=F32) * (2.0 / rot))
        ang = posf * inv[None, :]
        cos, sin = jnp.cos(ang), jnp.sin(ang)
        ones = jnp.ones((pos.shape[0], dh - rot), F32)
        c = jnp.concatenate([cos, cos, ones], axis=1)
        s = jnp.concatenate([-sin, sin, 0.0 * ones], axis=1)
        reps = LANES // dh
        return jnp.tile(c, (1, reps)), jnp.tile(s, (1, reps))

    c128, s128 = tables(HEAD_DIM)
    c64, s64 = tables(IDX_DIM)
    return c128, s128, c64, s64


def _permute_w_in(w_in):
    sizes = (A_Q_COLS, A_KV_COLS, A_KV_COLS, I_Q_COLS, IDX_HEADS, IDX_DIM, B_Q_COLS, B_KV_COLS, B_KV_COLS)
    offs = [0]
    for s in sizes:
        offs.append(offs[-1] + s)
    qa, ka, va, qi, wi, ki, qb, kb, vb = [w_in[:, offs[n]:offs[n + 1]] for n in range(len(sizes))]
    pad = jnp.zeros((w_in.shape[0], PROJ_TN - IDX_DIM - IDX_HEADS), w_in.dtype)
    return jnp.concatenate([qa, ka, va, qi, qb, kb, vb, ki, wi, pad], axis=1).astype(BF16)


def _sort_key(score):
    score = jnp.where(score == 0.0, 0.0, score)
    bits = pltpu.bitcast(score, I32)
    return bits ^ ((bits >> 31) & 0x7FFFFFFF)


def _lane_fold(x):
    part = x[:, :LANES]
    for u in range(1, x.shape[1] // LANES):
        part = part + x[:, u * LANES:(u + 1) * LANES]
    return part


def _count(keys_ref, nch, cw, pred):
    rows = keys_ref.shape[0]

    def body(c, acc):
        off = pl.multiple_of(c * cw, cw)
        blk = keys_ref[:, pl.ds(off, cw)]
        idx = off + lax.broadcasted_iota(I32, blk.shape, 1)
        return acc + _lane_fold(jnp.where(pred(blk, idx), 1.0, 0.0))

    acc = lax.fori_loop(0, nch, body, jnp.zeros((rows, LANES), F32))
    return jnp.sum(acc, axis=1, keepdims=True)


def _select_threshold(keys_ref, nch, cw, topk, idx_bits):
    rows = keys_ref.shape[0]
    kf = float(topk)
    cnt0 = _count(keys_ref, nch, cw, lambda k, i: k >= 0)
    tau = jnp.where(cnt0 >= kf, jnp.zeros((rows, 1), I32), jnp.full((rows, 1), INT_MIN, I32))

    def bit_body(b, tau):
        cand = tau + lax.shift_left(jnp.int32(1), 30 - b)
        cnt = _count(keys_ref, nch, cw, lambda k, i: k >= cand)
        return jnp.where(cnt >= kf, cand, tau)

    tau = lax.fori_loop(0, 31, bit_body, tau)
    n_ge = _count(keys_ref, nch, cw, lambda k, i: k >= tau)
    has_k = tau > INT_MIN
    tied = has_k & (n_ge > kf)
    any_tied = jnp.max(jnp.where(tied, 1.0, 0.0)) > 0.5

    def resolve(_):
        need = kf - _count(keys_ref, nch, cw, lambda k, i: k > tau)
        lim = jnp.zeros((rows, 1), I32)
        for b in range(idx_bits - 1, -1, -1):
            cand = lim + (1 << b)
            cnt = _count(keys_ref, nch, cw, lambda k, i: (k == tau) & (i < cand))
            lim = jnp.where(cnt < need, cand, lim)
        return lim

    lim = lax.cond(any_tied, resolve, lambda _: jnp.zeros((rows, 1), I32), 0)
    jlim = jnp.where(tied, lim, jnp.where(has_k, jnp.int32(2 ** 31 - 1), jnp.int32(-1)))
    return tau, jlim


def _topk_mask(key, idx, tau, jlim):
    return (key > tau) | ((key == tau) & (idx <= jlim))


def _flash_update(h, s, mask, v, m_scr, l_scr, acc_scr):
    s = jnp.where(mask, s, NEG)
    m_old = m_scr[h]
    m_new = jnp.maximum(m_old, jnp.max(s, axis=1, keepdims=True))
    alpha = jnp.exp(m_old - m_new)
    p = jnp.exp(s - m_new)
    l_scr[h] = alpha * l_scr[h] + jnp.sum(p, axis=1, keepdims=True)
    acc_scr[h] = alpha * acc_scr[h] + jnp.dot(p.astype(BF16), v, preferred_element_type=F32)
    m_scr[h] = m_new


def _flash_init(m_scr, l_scr, acc_scr):
    m_scr[...] = jnp.full(m_scr.shape, NEG, F32)
    l_scr[...] = jnp.zeros(l_scr.shape, F32)
    acc_scr[...] = jnp.zeros(acc_scr.shape, F32)


def _dsa_prompt_kernel(qi_ref, misc_ref, kit_ref, qa_ref, k_ref, v_ref, o_ref,
                       keys_scr, tau_scr, jl_scr, m_scr, l_scr, acc_scr, *, tq, tk, topk, idx_bits):
    i = pl.program_id(0)
    c = pl.program_id(1)
    t0 = i * tq
    nch = (t0 + tq + tk - 1) // tk
    scale = HEAD_DIM ** -0.5

    @pl.when(c == 0)
    def _():
        qpos = t0 + lax.broadcasted_iota(I32, (tq, 1), 0)
        wi = misc_ref[...]

        def chunk_body(kc, carry):
            off = pl.multiple_of(kc * LANES, LANES)
            kt = kit_ref[:, pl.ds(off, LANES)]
            score = jnp.zeros((tq, LANES), F32)
            for hh in range(IDX_HEADS):
                d = jnp.dot(qi_ref[:, hh * IDX_DIM:(hh + 1) * IDX_DIM], kt, preferred_element_type=F32)
                score = score + wi[:, MISC_WI + hh:MISC_WI + hh + 1] * jnp.maximum(d, 0.0)
            kpos = off + lax.broadcasted_iota(I32, (tq, LANES), 1)
            keys_scr[:, pl.ds(off, LANES)] = jnp.where(kpos <= qpos, _sort_key(score), INT_MIN)
            return carry

        lax.fori_loop(0, nch * (tk // LANES), chunk_body, 0)
        tau, jl = _select_threshold(keys_scr, nch, tk, topk, idx_bits)
        tau_scr[...] = tau
        jl_scr[...] = jl
        _flash_init(m_scr, l_scr, acc_scr)

    @pl.when(c < nch)
    def _():
        off = pl.multiple_of(c * tk, tk)
        key = keys_scr[:, pl.ds(off, tk)]
        kidx = off + lax.broadcasted_iota(I32, key.shape, 1)
        mask = _topk_mask(key, kidx, tau_scr[...], jl_scr[...])
        for kv in range(A_KV_HEADS):
            kk = k_ref[:, kv * HEAD_DIM:(kv + 1) * HEAD_DIM]
            vv = v_ref[:, kv * HEAD_DIM:(kv + 1) * HEAD_DIM]
            for g in range(A_HEADS // A_KV_HEADS):
                h = kv * (A_HEADS // A_KV_HEADS) + g
                s = _nt_dot(qa_ref[:, h * HEAD_DIM:(h + 1) * HEAD_DIM], kk) * scale
                _flash_update(h, s, mask, vv, m_scr, l_scr, acc_scr)

    @pl.when(c == pl.num_programs(1) - 1)
    def _():
        for h in range(A_HEADS):
            o_ref[:, h * HEAD_DIM:(h + 1) * HEAD_DIM] = (acc_scr[h] / l_scr[h]).astype(o_ref.dtype)


def _dsa_prompt(slab, misc, kit, tq, tk):
    rows = slab.shape[0]
    topk = min(DSA_TOPK, rows // 4)
    nkc = rows // tk

    def kv_map(col):
        return lambda i, c: (jnp.minimum(c, (i * tq + tq - 1) // tk), col)

    kern = functools.partial(_dsa_prompt_kernel, tq=tq, tk=tk, topk=topk,
                             idx_bits=max(1, (rows - 1).bit_length()))
    return pl.pallas_call(
        kern,
        grid=(rows // tq, nkc),
        in_specs=[pl.BlockSpec((tq, I_Q_COLS), lambda i, c: (i, COL_QI // I_Q_COLS)),
                  pl.BlockSpec((tq, LANES), lambda i, c: (i, 0)),
                  pl.BlockSpec((IDX_DIM, rows), lambda i, c: (0, 0)),
                  pl.BlockSpec((tq, A_Q_COLS), lambda i, c: (i, COL_QA // A_Q_COLS)),
                  pl.BlockSpec((tk, A_KV_COLS), kv_map(COL_KA // A_KV_COLS)),
                  pl.BlockSpec((tk, A_KV_COLS), kv_map(COL_VA // A_KV_COLS))],
        out_specs=pl.BlockSpec((tq, A_Q_COLS), lambda i, c: (i, 0)),
        out_shape=jax.ShapeDtypeStruct((rows, A_Q_COLS), BF16),
        scratch_shapes=[pltpu.VMEM((tq, rows), I32),
                        pltpu.VMEM((tq, 1), I32), pltpu.VMEM((tq, 1), I32),
                        pltpu.VMEM((A_HEADS, tq, 1), F32), pltpu.VMEM((A_HEADS, tq, 1), F32),
                        pltpu.VMEM((A_HEADS, tq, HEAD_DIM), F32)],
        compiler_params=pltpu.CompilerParams(
            dimension_semantics=("parallel", "arbitrary"), vmem_limit_bytes=VMEM_LIMIT),
    )(slab, misc, kit, slab, slab, slab)


def _gate_topk(gate, n_valid, ksel):
    lane = lax.broadcasted_iota(I32, gate.shape, 1)
    lanef = lane.astype(F32)
    gate = jnp.where(lane < n_valid, gate, NEG)
    sel = jnp.zeros(gate.shape, F32)
    for _ in range(ksel):
        mx = jnp.max(gate, axis=1, keepdims=True)
        first = jnp.min(jnp.where(gate == mx, lanef, 1e9), axis=1, keepdims=True)
        hit = (lanef == first) & (mx > 0.5 * NEG)
        sel = jnp.where(hit, 1.0, sel)
        gate = jnp.where(hit, NEG, gate)
    return sel


def _sel_column(sel, blk):
    lane = lax.broadcasted_iota(I32, sel.shape, 1)
    return jnp.sum(jnp.where(lane == blk, sel, 0.0), axis=1, keepdims=True)


def _block_means_kernel(k_ref, o_ref):
    o_ref[0] = jnp.sum(k_ref[...], axis=0, keepdims=True) * (1.0 / MOBA_BLOCK)


def _block_means(kb):
    nb = kb.shape[0] // MOBA_BLOCK
    out = pl.pallas_call(
        _block_means_kernel,
        grid=(nb,),
        in_specs=[pl.BlockSpec((MOBA_BLOCK, B_KV_COLS), lambda n: (n, 0))],
        out_specs=pl.BlockSpec((1, 1, B_KV_COLS), lambda n: (n, 0, 0)),
        out_shape=jax.ShapeDtypeStruct((nb, 1, B_KV_COLS), F32),
    )(kb)
    return out.reshape(nb, B_KV_COLS)


def _moba_prompt_kernel(qb_ref, means_ref, k_ref, v_ref, o_ref,
                        sel_scr, m_scr, l_scr, acc_scr, *, tq, ksel):
    i = pl.program_id(0)
    c = pl.program_id(1)
    t0 = i * tq
    qblk = t0 // MOBA_BLOCK
    scale = HEAD_DIM ** -0.5
    group = B_HEADS // B_KV_HEADS

    @pl.when(c == 0)
    def _():
        for h in range(B_HEADS):
            kv = h // group
            gate = _nt_dot(qb_ref[:, h * HEAD_DIM:(h + 1) * HEAD_DIM],
                           means_ref[:, kv * HEAD_DIM:(kv + 1) * HEAD_DIM])
            sel_scr[h] = _gate_topk(gate, qblk, ksel)
        _flash_init(m_scr, l_scr, acc_scr)

    @pl.when(c <= qblk)
    def _():
        own = (c == qblk).astype(I32)
        qlim = t0 + lax.broadcasted_iota(I32, (tq, 1), 0) + (1 - own) * (2 ** 30)
        kpos = c * MOBA_BLOCK + lax.broadcasted_iota(I32, (tq, MOBA_BLOCK), 1)
        causal = kpos <= qlim
        for kv in range(B_KV_HEADS):
            kk = k_ref[:, kv * HEAD_DIM:(kv + 1) * HEAD_DIM]
            vv = v_ref[:, kv * HEAD_DIM:(kv + 1) * HEAD_DIM]
            for g in range(group):
                h = kv * group + g
                col = _sel_column(sel_scr[h], c) + own.astype(F32)
                s = _nt_dot(qb_ref[:, h * HEAD_DIM:(h + 1) * HEAD_DIM], kk) * scale
                _flash_update(h, s, causal & (col > 0.5), vv, m_scr, l_scr, acc_scr)

    @pl.when(c == pl.num_programs(1) - 1)
    def _():
        for h in range(B_HEADS):
            o_ref[:, h * HEAD_DIM:(h + 1) * HEAD_DIM] = (acc_scr[h] / l_scr[h]).astype(o_ref.dtype)


def _moba_prompt(slab, means_pad, tq):
    rows = slab.shape[0]
    nb = rows // MOBA_BLOCK
    ksel = min(MOBA_TOPK, (rows - 1) // MOBA_BLOCK)
    nbp = means_pad.shape[0]

    def kv_map(col):
        return lambda i, c: (jnp.minimum(c, (i * tq) // MOBA_BLOCK), col)

    kern = functools.partial(_moba_prompt_kernel, tq=tq, ksel=ksel)
    return pl.pallas_call(
        kern,
        grid=(rows // tq, nb),
        in_specs=[pl.BlockSpec((tq, B_Q_COLS), lambda i, c: (i, COL_QB // B_Q_COLS)),
                  pl.BlockSpec((nbp, B_KV_COLS), lambda i, c: (0, 0)),
                  pl.BlockSpec((MOBA_BLOCK, B_KV_COLS), kv_map(COL_KB // B_KV_COLS)),
                  pl.BlockSpec((MOBA_BLOCK, B_KV_COLS), kv_map(COL_VB // B_KV_COLS))],
        out_specs=pl.BlockSpec((tq, B_Q_COLS), lambda i, c: (i, 0)),
        out_shape=jax.ShapeDtypeStruct((rows, B_Q_COLS), BF16),
        scratch_shapes=[pltpu.VMEM((B_HEADS, tq, nbp), F32),
                        pltpu.VMEM((B_HEADS, tq, 1), F32), pltpu.VMEM((B_HEADS, tq, 1), F32),
                        pltpu.VMEM((B_HEADS, tq, HEAD_DIM), F32)],
        compiler_params=pltpu.CompilerParams(
            dimension_semantics=("parallel", "arbitrary"), vmem_limit_bytes=VMEM_LIMIT),
    )(slab, means_pad, slab, slab)


TPAD = SUBLANES


def _dsa_sample_kernel(pt_ref, qi_ref, wi_ref, kidx_ref, kin_ref, qa_ref, k_ref, v_ref, kn_ref, vn_ref,
                       o_ref, keys_scr, tau_scr, jl_scr, m_scr, l_scr, acc_scr,
                       *, n_pages, t_new, topk, idx_bits):
    ph = pl.program_id(1)
    p = pl.program_id(2)
    past = n_pages * PAGE_SIZE
    scale = HEAD_DIM ** -0.5
    off = pl.multiple_of(p * PAGE_SIZE, PAGE_SIZE)
    trow = lax.broadcasted_iota(I32, (TPAD, 1), 0)
    kpos = off + lax.broadcasted_iota(I32, (TPAD, PAGE_SIZE), 1)
    visible = (kpos <= past + trow) & (trow < t_new)

    def index_page(kd):
        d = _nt_dot(qi_ref[0], kd)
        x = wi_ref[0] * jnp.maximum(d, 0.0)
        score = x[:TPAD]
        for hh in range(1, IDX_HEADS):
            score = score + x[hh * TPAD:(hh + 1) * TPAD]
        keys_scr[:, pl.ds(off, PAGE_SIZE)] = jnp.where(visible, _sort_key(score), INT_MIN)

    @pl.when((ph == 0) & (p < n_pages))
    def _():
        index_page(kidx_ref[0].astype(BF16))

    @pl.when((ph == 0) & (p == n_pages))
    def _():
        index_page(kin_ref[0])
        tau, jl = _select_threshold(keys_scr, n_pages + 1, PAGE_SIZE, topk, idx_bits)
        tau_scr[...] = tau
        jl_scr[...] = jl
        _flash_init(m_scr, l_scr, acc_scr)

    def attend(kpage, vpage):
        key = keys_scr[:, pl.ds(off, PAGE_SIZE)]
        mask = _topk_mask(key, kpos, tau_scr[...], jl_scr[...]) & visible
        for h in range(A_HEADS):
            kv = h // (A_HEADS // A_KV_HEADS)
            s = _nt_dot(qa_ref[0, h], kpage[:, kv * HEAD_DIM:(kv + 1) * HEAD_DIM]) * scale
            _flash_update(h, s, mask, vpage[:, kv * HEAD_DIM:(kv + 1) * HEAD_DIM], m_scr, l_scr, acc_scr)

    @pl.when((ph == 1) & (p < n_pages))
    def _():
        attend(k_ref[0].astype(BF16), v_ref[0].astype(BF16))

    @pl.when((ph == 1) & (p == n_pages))
    def _():
        attend(kn_ref[0], vn_ref[0])
        for h in range(A_HEADS):
            o_ref[0, h] = acc_scr[h] / jnp.maximum(l_scr[h], 1e-30)


def _page_map(n_pages, phase):
    def index(b, ph, p, pt):
        logical = jnp.where(ph == phase, jnp.minimum(p, n_pages - 1), 0 if phase == 1 else n_pages - 1)
        return (pt[b * n_pages + logical], 0, 0)
    return index


def _dsa_sample(pt_flat, n_pages, t_new, qi_s, wi_s, kidx_pages, ki_new, qa_s, k_pages, v_pages, ka_new, va_new):
    ns = qi_s.shape[0]
    past = n_pages * PAGE_SIZE
    topk = min(DSA_TOPK, (past + t_new) // 4)
    ntot = (n_pages + 1) * PAGE_SIZE
    seq3 = lambda s1, s2: pl.BlockSpec((1, s1, s2), lambda b, ph, p, pt: (b, 0, 0))
    seq4 = pl.BlockSpec((1, A_HEADS, TPAD, HEAD_DIM), lambda b, ph, p, pt: (b, 0, 0, 0))
    kern = functools.partial(_dsa_sample_kernel, n_pages=n_pages, t_new=t_new, topk=topk,
                             idx_bits=max(1, (ntot - 1).bit_length()))
    grid_spec = pltpu.PrefetchScalarGridSpec(
        num_scalar_prefetch=1,
        grid=(ns, 2, n_pages + 1),
        in_specs=[seq3(IDX_HEADS * TPAD, IDX_DIM), seq3(IDX_HEADS * TPAD, 1),
                  pl.BlockSpec((1, PAGE_SIZE, IDX_DIM), _page_map(n_pages, 0)),
                  seq3(PAGE_SIZE, IDX_DIM),
                  seq4,
                  pl.BlockSpec((1, PAGE_SIZE, A_KV_COLS), _page_map(n_pages, 1)),
                  pl.BlockSpec((1, PAGE_SIZE, A_KV_COLS), _page_map(n_pages, 1)),
                  seq3(PAGE_SIZE, A_KV_COLS), seq3(PAGE_SIZE, A_KV_COLS)],
        out_specs=seq4,
        scratch_shapes=[pltpu.VMEM((TPAD, ntot), I32),
                        pltpu.VMEM((TPAD, 1), I32), pltpu.VMEM((TPAD, 1), I32),
                        pltpu.VMEM((A_HEADS, TPAD, 1), F32), pltpu.VMEM((A_HEADS, TPAD, 1), F32),
                        pltpu.VMEM((A_HEADS, TPAD, HEAD_DIM), F32)])
    return pl.pallas_call(
        kern, grid_spec=grid_spec,
        out_shape=jax.ShapeDtypeStruct((ns, A_HEADS, TPAD, HEAD_DIM), F32),
        compiler_params=pltpu.CompilerParams(
            dimension_semantics=("parallel", "arbitrary", "arbitrary"), vmem_limit_bytes=VMEM_LIMIT),
    )(pt_flat, qi_s, wi_s, kidx_pages, ki_new, qa_s, k_pages, v_pages, ka_new, va_new)


def _moba_sample_kernel(pt_ref, qb_ref, k_ref, v_ref, kn_ref, vn_ref, o_ref,
                        bsum_scr, s_scr, sel_scr, m_scr, l_scr, acc_scr, *, n_pages, t_new, ksel):
    ph = pl.program_id(1)
    p = pl.program_id(2)
    past = n_pages * PAGE_SIZE
    ppb = MOBA_BLOCK // PAGE_SIZE
    nbp = past // MOBA_BLOCK
    scale = HEAD_DIM ** -0.5
    group = B_HEADS // B_KV_HEADS
    off = pl.multiple_of(p * PAGE_SIZE, PAGE_SIZE)

    def score_page(kpage):
        for h in range(B_HEADS):
            kv = h // group
            s_scr[h, :, pl.ds(off, PAGE_SIZE)] = _nt_dot(
                qb_ref[0, h], kpage[:, kv * HEAD_DIM:(kv + 1) * HEAD_DIM]) * scale

    @pl.when((ph == 0) & (p == 0))
    def _():
        bsum_scr[...] = jnp.zeros(bsum_scr.shape, F32)

    @pl.when((ph == 0) & (p < n_pages))
    def _():
        kpage = k_ref[0]
        blk = p // ppb
        bsum_scr[pl.ds(blk, 1), :] = bsum_scr[pl.ds(blk, 1), :] + jnp.sum(kpage, axis=0, keepdims=True)
        score_page(kpage.astype(BF16))

    @pl.when((ph == 0) & (p == n_pages))
    def _():
        score_page(kn_ref[0])
        means = (bsum_scr[...] * (1.0 / MOBA_BLOCK)).astype(BF16)
        for h in range(B_HEADS):
            kv = h // group
            gate = _nt_dot(qb_ref[0, h], means[:, kv * HEAD_DIM:(kv + 1) * HEAD_DIM])
            sel_scr[h] = _gate_topk(gate, nbp, ksel)
        _flash_init(m_scr, l_scr, acc_scr)

    trow = lax.broadcasted_iota(I32, (TPAD, 1), 0)

    @pl.when((ph == 1) & (p < n_pages))
    def _():
        vpage = v_ref[0].astype(BF16)
        blk = p // ppb
        for h in range(B_HEADS):
            kv = h // group
            col = _sel_column(sel_scr[h], blk)
            mask = jnp.broadcast_to((col > 0.5) & (trow < t_new), (TPAD, PAGE_SIZE))
            _flash_update(h, s_scr[h, :, pl.ds(off, PAGE_SIZE)], mask,
                          vpage[:, kv * HEAD_DIM:(kv + 1) * HEAD_DIM], m_scr, l_scr, acc_scr)

    @pl.when((ph == 1) & (p == n_pages))
    def _():
        vpage = vn_ref[0]
        kpos = off + lax.broadcasted_iota(I32, (TPAD, PAGE_SIZE), 1)
        mask = (kpos <= past + trow) & (trow < t_new)
        for h in range(B_HEADS):
            kv = h // group
            _flash_update(h, s_scr[h, :, pl.ds(off, PAGE_SIZE)], mask,
                          vpage[:, kv * HEAD_DIM:(kv + 1) * HEAD_DIM], m_scr, l_scr, acc_scr)
            o_ref[0, h] = acc_scr[h] / jnp.maximum(l_scr[h], 1e-30)


def _moba_sample(pt_flat, n_pages, t_new, qb_s, k_pages, v_pages, kb_new, vb_new):
    ns = qb_s.shape[0]
    past = n_pages * PAGE_SIZE
    nbp = past // MOBA_BLOCK
    ksel = min(MOBA_TOPK, nbp)
    nb_pad = -(-nbp // LANES) * LANES
    ntot = (n_pages + 1) * PAGE_SIZE
    seq3 = lambda s1, s2: pl.BlockSpec((1, s1, s2), lambda b, ph, p, pt: (b, 0, 0))
    seq4 = pl.BlockSpec((1, B_HEADS, TPAD, HEAD_DIM), lambda b, ph, p, pt: (b, 0, 0, 0))
    kern = functools.partial(_moba_sample_kernel, n_pages=n_pages, t_new=t_new, ksel=ksel)
    grid_spec = pltpu.PrefetchScalarGridSpec(
        num_scalar_prefetch=1,
        grid=(ns, 2, n_pages + 1),
        in_specs=[seq4,
                  pl.BlockSpec((1, PAGE_SIZE, B_KV_COLS), _page_map(n_pages, 0)),
                  pl.BlockSpec((1, PAGE_SIZE, B_KV_COLS), _page_map(n_pages, 1)),
                  seq3(PAGE_SIZE, B_KV_COLS), seq3(PAGE_SIZE, B_KV_COLS)],
        out_specs=seq4,
        scratch_shapes=[pltpu.VMEM((nb_pad, B_KV_COLS), F32),
                        pltpu.VMEM((B_HEADS, TPAD, ntot), F32),
                        pltpu.VMEM((B_HEADS, TPAD, nb_pad), F32),
                        pltpu.VMEM((B_HEADS, TPAD, 1), F32), pltpu.VMEM((B_HEADS, TPAD, 1), F32),
                        pltpu.VMEM((B_HEADS, TPAD, HEAD_DIM), F32)])
    return pl.pallas_call(
        kern, grid_spec=grid_spec,
        out_shape=jax.ShapeDtypeStruct((ns, B_HEADS, TPAD, HEAD_DIM), F32),
        compiler_params=pltpu.CompilerParams(
            dimension_semantics=("parallel", "arbitrary", "arbitrary"), vmem_limit_bytes=VMEM_LIMIT),
    )(pt_flat, qb_s, k_pages, v_pages, kb_new, vb_new)


def _memory_kv_kernel(mem_ref, g_ref, w_ref, k_ref, v_ref):
    h = _rms(mem_ref[...], g_ref[...]).astype(BF16)
    kv = jnp.dot(h, w_ref[...], preferred_element_type=F32)
    k_ref[...] = kv[:, :MEM_WIDTH]
    v_ref[...] = kv[:, MEM_WIDTH:]


def _memory_kv(mem, g, w_bf):
    m = mem.shape[0]
    return pl.pallas_call(
        _memory_kv_kernel,
        out_shape=[jax.ShapeDtypeStruct((m, MEM_WIDTH), F32), jax.ShapeDtypeStruct((m, MEM_WIDTH), F32)],
        compiler_params=pltpu.CompilerParams(vmem_limit_bytes=VMEM_LIMIT),
    )(mem, g, w_bf)


def _outproj_cross_kernel(x_ref, oa_ref, ob_ref, woa_ref, wob_ref, g_ref, wq_ref, mk_ref, mv_ref, wo_ref,
                          y_ref, *, rows_per_seq):
    x1 = x_ref[...] + jnp.dot(oa_ref[...], woa_ref[...], preferred_element_type=F32) \
        + jnp.dot(ob_ref[...], wob_ref[...], preferred_element_type=F32)
    hc = _rms(x1, g_ref[...]).astype(BF16)
    q = jnp.dot(hc, wq_ref[...], preferred_element_type=F32).astype(BF16)
    tm = q.shape[0]
    nk = mk_ref.shape[0]
    scale = MEM_HEAD_DIM ** -0.5
    if rows_per_seq is not None:
        rseq = (pl.program_id(0) * tm + lax.broadcasted_iota(I32, (tm, 1), 0)) // rows_per_seq
        kseq = lax.broadcasted_iota(I32, (tm, nk), 1) // MEM_TOKENS
        mask = kseq == rseq
    outs = []
    for h in range(MEM_HEADS):
        sl = slice(h * MEM_HEAD_DIM, (h + 1) * MEM_HEAD_DIM)
        s = _nt_dot(q[:, sl], mk_ref[:, sl]) * scale
        if rows_per_seq is not None:
            s = jnp.where(mask, s, NEG)
        m = jnp.max(s, axis=1, keepdims=True)
        e = jnp.exp(s - m)
        p = e / jnp.sum(e, axis=1, keepdims=True)
        outs.append(jnp.dot(p.astype(BF16), mv_ref[:, sl], preferred_element_type=F32))
    o = jnp.concatenate(outs, axis=1).astype(BF16)
    y_ref[...] = x1 + jnp.dot(o, wo_ref[...], preferred_element_type=F32)


def _outproj_cross(x, oa, ob, w_out_bf, g_cross, w_mq_bf, mk_bf, mv_bf, w_mo_bf, tm, rows_per_seq):
    rows = x.shape[0]
    nk = mk_bf.shape[0]
    row_spec = lambda w: pl.BlockSpec((tm, w), lambda i: (i, 0))
    full = lambda a, b: pl.BlockSpec((a, b), lambda i: (0, 0))
    kern = functools.partial(_outproj_cross_kernel, rows_per_seq=rows_per_seq)
    return pl.pallas_call(
        kern,
        grid=(rows // tm,),
        in_specs=[row_spec(D_MODEL), row_spec(A_Q_COLS), row_spec(B_Q_COLS),
                  pl.BlockSpec((A_Q_COLS, D_MODEL), lambda i: (0, 0)),
                  pl.BlockSpec((B_Q_COLS, D_MODEL), lambda i: (1, 0)),
                  full(1, D_MODEL), full(D_MODEL, MEM_WIDTH), full(nk, MEM_WIDTH), full(nk, MEM_WIDTH),
                  full(MEM_WIDTH, D_MODEL)],
        out_specs=row_spec(D_MODEL),
        out_shape=jax.ShapeDtypeStruct((rows, D_MODEL), F32),
        compiler_params=pltpu.CompilerParams(
            dimension_semantics=("parallel",), vmem_limit_bytes=VMEM_LIMIT),
    )(x, oa, ob, w_out_bf, w_out_bf, g_cross, w_mq_bf, mk_bf, mv_bf, w_mo_bf)


def _mlp_final_kernel(x_ref, g_ref, wu_ref, wd_ref, gf_ref, y_ref, h_scr, acc_scr):
    f = pl.program_id(1)

    @pl.when(f == 0)
    def _():
        h_scr[...] = _rms(x_ref[...], g_ref[...]).astype(BF16)
        acc_scr[...] = jnp.zeros(acc_scr.shape, F32)

    u = jnp.maximum(jnp.dot(h_scr[...], wu_ref[...], preferred_element_type=F32), 0.0)
    acc_scr[...] += jnp.dot((u * u).astype(BF16), wd_ref[...], preferred_element_type=F32)

    @pl.when(f == pl.num_programs(1) - 1)
    def _():
        y_ref[...] = _rms(x_ref[...] + acc_scr[...], gf_ref[...])


def _mlp_final(x, g_ffn, w_up_bf, w_down_bf, g_final, tm, tf):
    rows = x.shape[0]
    d_ff = w_up_bf.shape[1]
    return pl.pallas_call(
        _mlp_final_kernel,
        grid=(rows // tm, d_ff // tf),
        in_specs=[pl.BlockSpec((tm, D_MODEL), lambda i, f: (i, 0)),
                  pl.BlockSpec((1, D_MODEL), lambda i, f: (0, 0)),
                  pl.BlockSpec((D_MODEL, tf), lambda i, f: (0, f)),
                  pl.BlockSpec((tf, D_MODEL), lambda i, f: (f, 0)),
                  pl.BlockSpec((1, D_MODEL), lambda i, f: (0, 0))],
        out_specs=pl.BlockSpec((tm, D_MODEL), lambda i, f: (i, 0)),
        out_shape=jax.ShapeDtypeStruct((rows, D_MODEL), F32),
        scratch_shapes=[pltpu.VMEM((tm, D_MODEL), BF16), pltpu.VMEM((tm, D_MODEL), F32)],
        compiler_params=pltpu.CompilerParams(
            dimension_semantics=("parallel", "arbitrary"), vmem_limit_bytes=VMEM_LIMIT),
    )(x, g_ffn, w_up_bf, w_down_bf, g_final)


def _row_tile(rows, want):
    t = min(rows, want)
    while rows % t:
        t //= 2
    return t


def _pad_rows(a, axis, size):
    pad = [(0, 0)] * a.ndim
    pad[axis] = (0, size - a.shape[axis])
    return jnp.pad(a, pad)


def _heads_first(a, ns, t_new, heads, dim):
    a = a.reshape(ns, t_new, heads, dim).transpose(0, 2, 1, 3)
    return _pad_rows(a, 2, TPAD)


def kernel(x_prompt, x_sample, cache_k_a, cache_v_a, cache_kidx, cache_k_b, cache_v_b, cache_mem_k,
           cache_mem_v, page_table, mem_prompt, g_mix, w_in, g_kidx, b_kidx, w_out, g_cross, w_mq, g_mem,
           w_mkv, w_mo, g_ffn, w_up, w_down, g_final):
    batch, s_len, _ = x_prompt.shape
    ns, t_new, _ = x_sample.shape
    depth = w_in.shape[0]
    n_pages = page_table.shape[1]
    past = n_pages * PAGE_SIZE
    n_phys = cache_k_a.shape[1]
    assert batch == 1 and depth == 1
    assert s_len % 512 == 0 and past % MOBA_BLOCK == 0 and t_new <= TPAD
    l = 0
    row = lambda v: v.reshape(1, -1)

    w_in_p = _permute_w_in(w_in[l])
    gk = row(jnp.concatenate([g_kidx[l], jnp.zeros((LANES - IDX_DIM,), F32)]))
    bk = row(jnp.concatenate([b_kidx[l], jnp.zeros((LANES - IDX_DIM,), F32)]))
    w_out_bf = w_out[l].astype(BF16)
    w_mq_bf = w_mq[l].astype(BF16)
    w_mo_bf = w_mo[l].astype(BF16)
    w_up_bf = w_up[l].astype(BF16)
    w_down_bf = w_down[l].astype(BF16)

    xp = x_prompt.reshape(s_len, D_MODEL)
    tabs_p = _rope_tables(jnp.arange(s_len))
    slab, ka, va, kb, vb, misc = _proj(xp, row(g_mix[l]), w_in_p, tabs_p, gk, bk, _row_tile(s_len, 512))
    kit = slab[:, COL_MISC:COL_MISC + IDX_DIM].T
    oa = _dsa_prompt(slab, misc, kit, tq=128, tk=512)
    nb = s_len // MOBA_BLOCK
    means = _block_means(kb).astype(BF16)
    means_pad = _pad_rows(means, 0, -(-nb // LANES) * LANES)
    ob = _moba_prompt(slab, means_pad, tq=128)
    mk, mv = _memory_kv(mem_prompt.reshape(MEM_TOKENS, D_MODEL), row(g_mem[l]), w_mkv[l].astype(BF16))
    xp2 = _outproj_cross(xp, oa, ob, w_out_bf, row(g_cross[l]), w_mq_bf, mk.astype(BF16), mv.astype(BF16),
                         w_mo_bf, _row_tile(s_len, 256), None)
    y_prompt = _mlp_final(xp2, row(g_ffn[l]), w_up_bf, w_down_bf, row(g_final), _row_tile(s_len, 512), 512)

    rows_s = ns * t_new
    xs = x_sample.reshape(rows_s, D_MODEL)
    pos_s = jnp.tile(past + jnp.arange(t_new), ns)
    tabs_s = _rope_tables(pos_s)
    rows_pad = -(-rows_s // SUBLANES) * SUBLANES
    xs_pad = _pad_rows(xs, 0, rows_pad)
    tabs_s = tuple(_pad_rows(t, 0, rows_pad) for t in tabs_s)
    slab_s, ka_s, va_s, kb_s, vb_s, misc_s = _proj(xs_pad, row(g_mix[l]), w_in_p, tabs_s, gk, bk,
                                                   _row_tile(rows_pad, 128))
    slab_s, ka_s, va_s, kb_s, vb_s, misc_s = [a[:rows_s] for a in (slab_s, ka_s, va_s, kb_s, vb_s, misc_s)]

    pt_flat = page_table.reshape(-1).astype(I32)
    qi_s = _heads_first(slab_s[:, COL_QI:COL_QI + I_Q_COLS], ns, t_new, IDX_HEADS, IDX_DIM)
    qi_s = qi_s.reshape(ns, IDX_HEADS * TPAD, IDX_DIM)
    wi_s = _heads_first(misc_s[:, MISC_WI:MISC_WI + IDX_HEADS], ns, t_new, IDX_HEADS, 1)
    wi_s = wi_s.reshape(ns, IDX_HEADS * TPAD, 1)
    new_rows = lambda a, w: _pad_rows(a.reshape(ns, t_new, w), 1, PAGE_SIZE)
    ki_new = new_rows(slab_s[:, COL_MISC:COL_MISC + IDX_DIM], IDX_DIM)
    qa_s = _heads_first(slab_s[:, COL_QA:COL_QA + A_Q_COLS], ns, t_new, A_HEADS, HEAD_DIM)
    ka_new = new_rows(slab_s[:, COL_KA:COL_KA + A_KV_COLS], A_KV_COLS)
    va_new = new_rows(slab_s[:, COL_VA:COL_VA + A_KV_COLS], A_KV_COLS)
    oa_s = _dsa_sample(pt_flat, n_pages, t_new, qi_s, wi_s,
                       cache_kidx[l].reshape(n_phys, PAGE_SIZE, IDX_DIM), ki_new, qa_s,
                       cache_k_a[l].reshape(n_phys, PAGE_SIZE, A_KV_COLS),
                       cache_v_a[l].reshape(n_phys, PAGE_SIZE, A_KV_COLS), ka_new, va_new)
    qb_s = _heads_first(slab_s[:, COL_QB:COL_QB + B_Q_COLS], ns, t_new, B_HEADS, HEAD_DIM)
    kb_new = new_rows(slab_s[:, COL_KB:COL_KB + B_KV_COLS], B_KV_COLS)
    vb_new = new_rows(slab_s[:, COL_VB:COL_VB + B_KV_COLS], B_KV_COLS)
    ob_s = _moba_sample(pt_flat, n_pages, t_new, qb_s,
                        cache_k_b[l].reshape(n_phys, PAGE_SIZE, B_KV_COLS),
                        cache_v_b[l].reshape(n_phys, PAGE_SIZE, B_KV_COLS), kb_new, vb_new)
    tokens_first = lambda o: o[:, :, :t_new].transpose(0, 2, 1, 3).reshape(rows_s, -1).astype(BF16)
    oa_s2 = _pad_rows(tokens_first(oa_s), 0, rows_pad)
    ob_s2 = _pad_rows(tokens_first(ob_s), 0, rows_pad)
    mk_s = cache_mem_k[l].reshape(ns * MEM_TOKENS, MEM_WIDTH).astype(BF16)
    mv_s = cache_mem_v[l].reshape(ns * MEM_TOKENS, MEM_WIDTH).astype(BF16)
    xs2 = _outproj_cross(xs_pad, oa_s2, ob_s2, w_out_bf, row(g_cross[l]), w_mq_bf, mk_s, mv_s, w_mo_bf,
                         rows_pad, t_new)
    y_sample = _mlp_final(xs2, row(g_ffn[l]), w_up_bf, w_down_bf, row(g_final), rows_pad, 512)[:rows_s]

    kv5 = lambda a, n, t: a.reshape(1, n, t, A_KV_HEADS, HEAD_DIM)
    return (y_prompt.reshape(batch, s_len, D_MODEL), y_sample.reshape(ns, t_new, D_MODEL),
            kv5(ka, batch, s_len), kv5(va, batch, s_len),
            misc[:, :IDX_DIM].reshape(1, batch, s_len, IDX_DIM),
            kv5(kb, batch, s_len), kv5(vb, batch, s_len),
            mk.reshape(1, batch, MEM_TOKENS, MEM_HEADS, MEM_HEAD_DIM),
            mv.reshape(1, batch, MEM_TOKENS, MEM_HEADS, MEM_HEAD_DIM),
            kv5(ka_s, ns, t_new), kv5(va_s, ns, t_new),
            misc_s[:, :IDX_DIM].reshape(1, ns, t_new, IDX_DIM),
            kv5(kb_s, ns, t_new), kv5(vb_s, ns, t_new))
```

```python
import functools

import jax
import jax.numpy as jnp
from jax import lax
from jax.experimental import pallas as pl
from jax.experimental.pallas import tpu as pltpu

F32 = jnp.float32
BF16 = jnp.bfloat16
I32 = jnp.int32

D_MODEL = 2048
HEAD_DIM = 128
A_HEADS = 8
A_KV_HEADS = 4
B_HEADS = 8
B_KV_HEADS = 4
IDX_HEADS = 16
IDX_DIM = 64
DSA_TOPK = 256
MOBA_BLOCK = 256
MOBA_TOPK = 3
MEM_TOKENS = 256
MEM_HEADS = 4
MEM_HEAD_DIM = 128
PAGE_SIZE = 128
ROPE_THETA = 500000.0
NORM_EPS = 1e-6

A_Q_COLS = A_HEADS * HEAD_DIM
A_KV_COLS = A_KV_HEADS * HEAD_DIM
I_Q_COLS = IDX_HEADS * IDX_DIM
B_Q_COLS = B_HEADS * HEAD_DIM
B_KV_COLS = B_KV_HEADS * HEAD_DIM
MEM_WIDTH = MEM_HEADS * MEM_HEAD_DIM

LANES = 128
SUBLANES = 8
VMEM_LIMIT = 56 * 1024 * 1024
INT_MIN = -(2 ** 31)
NEG = -1e30
LOG2E = 1.4426950408889634
SOFTMAX_EXP2_SCALE = HEAD_DIM ** -0.5 * LOG2E

PROJ_TN = 512
COL_QA, COL_KA, COL_VA, COL_QI, COL_QB, COL_KB, COL_VB, COL_MISC = 0, 1024, 1536, 2048, 3072, 4096, 4608, 5120
SLAB_COLS = COL_MISC + PROJ_TN
MISC_WI = IDX_DIM


def _nt_dot(a, b):
    return lax.dot_general(a, b, (((1,), (1,)), ((), ())), preferred_element_type=F32)


def _rms(x, g):
    return x * lax.rsqrt(jnp.mean(x * x, axis=-1, keepdims=True) + NORM_EPS) * g


def _rope(x, cos_t, sin_t, half, period):
    width = x.shape[1]
    reps = width // LANES
    c = jnp.concatenate([cos_t] * reps, axis=1) if reps > 1 else cos_t
    s = jnp.concatenate([sin_t] * reps, axis=1) if reps > 1 else sin_t
    lane = lax.broadcasted_iota(I32, x.shape, 1)
    first = (lane & (period - 1)) < half
    up = pltpu.roll(x, width - half, axis=1)
    dn = pltpu.roll(x, half, axis=1)
    return x * c + jnp.where(first, up, dn) * s


def _proj_kernel(x_ref, g_ref, w_ref, c128_ref, s128_ref, c64_ref, s64_ref, gk_ref, bk_ref,
                 slab_ref, ka_ref, va_ref, kb_ref, vb_ref, misc_ref, miscb_ref, h_scr):
    j = pl.program_id(1)

    @pl.when(j == 0)
    def _():
        h_scr[...] = _rms(x_ref[...], g_ref[...]).astype(BF16)

    acc = jnp.dot(h_scr[...], w_ref[...], preferred_element_type=F32)

    def rope128(v):
        return _rope(v, c128_ref[...], s128_ref[...], HEAD_DIM // 8, HEAD_DIM)

    is_q128 = (j == COL_QA // PROJ_TN) | (j == COL_QA // PROJ_TN + 1) | \
              (j == COL_QB // PROJ_TN) | (j == COL_QB // PROJ_TN + 1)

    @pl.when(is_q128)
    def _():
        slab_ref[...] = (rope128(acc) * SOFTMAX_EXP2_SCALE).astype(BF16)

    @pl.when(j == COL_KA // PROJ_TN)
    def _():
        r = rope128(acc)
        ka_ref[...] = r
        slab_ref[...] = r.astype(BF16)

    @pl.when(j == COL_KB // PROJ_TN)
    def _():
        r = rope128(acc)
        kb_ref[...] = r
        slab_ref[...] = r.astype(BF16)

    @pl.when(j == COL_VA // PROJ_TN)
    def _():
        va_ref[...] = acc
        slab_ref[...] = acc.astype(BF16)

    @pl.when(j == COL_VB // PROJ_TN)
    def _():
        vb_ref[...] = acc
        slab_ref[...] = acc.astype(BF16)

    @pl.when((j == COL_QI // PROJ_TN) | (j == COL_QI // PROJ_TN + 1))
    def _():
        slab_ref[...] = _rope(acc, c64_ref[...], s64_ref[...], IDX_DIM // 8, IDX_DIM).astype(BF16)

    @pl.when(j == COL_MISC // PROJ_TN)
    def _():
        y = acc[:, :LANES]
        lane = lax.broadcasted_iota(I32, y.shape, 1)
        is_ki = lane < IDX_DIM
        mu = jnp.sum(jnp.where(is_ki, y, 0.0), axis=-1, keepdims=True) * (1.0 / IDX_DIM)
        xc = y - mu
        var = jnp.sum(jnp.where(is_ki, xc * xc, 0.0), axis=-1, keepdims=True) * (1.0 / IDX_DIM)
        ln = xc * lax.rsqrt(var + NORM_EPS) * gk_ref[...] + bk_ref[...]
        ki = _rope(ln, c64_ref[...], s64_ref[...], IDX_DIM // 8, IDX_DIM)
        wi = y * (IDX_HEADS ** -0.5 * IDX_DIM ** -0.5)
        out = jnp.where(is_ki, ki, wi)
        misc_ref[...] = out
        miscb_ref[...] = out.astype(BF16)
        slab_ref[...] = jnp.concatenate(
            [out.astype(BF16), jnp.zeros((out.shape[0], PROJ_TN - LANES), BF16)], axis=1)


def _proj(x, g, w_perm, tabs, gk, bk, tm):
    rows = x.shape[0]
    c128, s128, c64, s64 = tabs
    n_col = SLAB_COLS // PROJ_TN
    row_spec = lambda w: pl.BlockSpec((tm, w), lambda i, j: (i, 0))
    vec_spec = lambda w: pl.BlockSpec((1, w), lambda i, j: (0, 0))
    return pl.pallas_call(
        _proj_kernel,
        grid=(rows // tm, n_col),
        in_specs=[row_spec(D_MODEL), vec_spec(D_MODEL),
                  pl.BlockSpec((D_MODEL, PROJ_TN), lambda i, j: (0, j)),
                  row_spec(LANES), row_spec(LANES), row_spec(LANES), row_spec(LANES),
                  vec_spec(LANES), vec_spec(LANES)],
        out_specs=[pl.BlockSpec((tm, PROJ_TN), lambda i, j: (i, j)),
                   row_spec(A_KV_COLS), row_spec(A_KV_COLS), row_spec(B_KV_COLS), row_spec(B_KV_COLS),
                   row_spec(LANES), row_spec(LANES)],
        out_shape=[jax.ShapeDtypeStruct((rows, SLAB_COLS), BF16),
                   jax.ShapeDtypeStruct((rows, A_KV_COLS), F32),
                   jax.ShapeDtypeStruct((rows, A_KV_COLS), F32),
                   jax.ShapeDtypeStruct((rows, B_KV_COLS), F32),
                   jax.ShapeDtypeStruct((rows, B_KV_COLS), F32),
                   jax.ShapeDtypeStruct((rows, LANES), F32),
                   jax.ShapeDtypeStruct((rows, LANES), BF16)],
        scratch_shapes=[pltpu.VMEM((tm, D_MODEL), BF16)],
        name="proj",
        compiler_params=pltpu.CompilerParams(
            dimension_semantics=("parallel", "arbitrary"), vmem_limit_bytes=VMEM_LIMIT),
    )(x, g, w_perm, c128, s128, c64, s64, gk, bk)


def _rope_tables(pos):
    posf = pos.astype(F32)[:, None]

    def tables(dh):
        rot = dh // 4
        half = rot // 2
        inv = jnp.power(jnp.float32(ROPE_THETA), -jnp.arange(half, dtype=F32) * (2.0 / rot))
        ang = posf * inv[None, :]
        cos, sin = jnp.cos(ang), jnp.sin(ang)
        ones = jnp.ones((pos.shape[0], dh - rot), F32)
        c = jnp.concatenate([cos, cos, ones], axis=1)
        s = jnp.concatenate([-sin, sin, 0.0 * ones], axis=1)
        reps = LANES // dh
        return jnp.tile(c, (1, reps)), jnp.tile(s, (1, reps))

    c128, s128 = tables(HEAD_DIM)
    c64, s64 = tables(IDX_DIM)
    return c128, s128, c64, s64


def _permute_w_in(w_in):
    sizes = (A_Q_COLS, A_KV_COLS, A_KV_COLS, I_Q_COLS, IDX_HEADS, IDX_DIM, B_Q_COLS, B_KV_COLS, B_KV_COLS)
    offs = [0]
    for s in sizes:
        offs.append(offs[-1] + s)
    qa, ka, va, qi, wi, ki, qb, kb, vb = [w_in[:, offs[n]:offs[n + 1]] for n in range(len(sizes))]
    pad = jnp.zeros((w_in.shape[0], PROJ_TN - IDX_DIM - IDX_HEADS), w_in.dtype)
    return jnp.concatenate([qa, ka, va, qi, qb, kb, vb, ki, wi, pad], axis=1).astype(BF16)


TOPK_GROUPS = 256


def _sort_key(score):
    score = jnp.where(score == 0.0, 0.0, score)
    bits = pltpu.bitcast(score, I32)
    return bits ^ ((bits >> 31) & 0x7FFFFFFF)


def _fold_lanes(op, x, width):
    parts = [x[:, u * width:(u + 1) * width] for u in range(x.shape[1] // width)]
    while len(parts) > 1:
        nxt = [op(parts[u], parts[u + 1]) for u in range(0, len(parts) - 1, 2)]
        if len(parts) % 2:
            nxt.append(parts[-1])
        parts = nxt
    return parts[0]


def _lane_allreduce(op, x):
    shift = LANES // 2
    while shift >= 1:
        x = op(x, pltpu.roll(x, shift, axis=1))
        shift //= 2
    return x


def _count(keys_ref, nch, cw, pred):
    rows = keys_ref.shape[0]

    def body(c, acc):
        off = pl.multiple_of(c * cw, cw)
        blk = keys_ref[:, pl.ds(off, cw)]
        idx = off + lax.broadcasted_iota(I32, blk.shape, 1)
        return acc + _fold_lanes(jnp.add, jnp.where(pred(blk, idx), 1.0, 0.0), LANES)

    acc = lax.fori_loop(0, nch, body, jnp.zeros((rows, LANES), F32))
    return jnp.sum(acc, axis=1, keepdims=True)


def _row_bounds(keys_ref, nch, cw):
    rows = keys_ref.shape[0]

    def body(c, gmax):
        off = pl.multiple_of(c * cw, cw)
        return jnp.maximum(gmax, _fold_lanes(jnp.maximum, keys_ref[:, pl.ds(off, cw)], TOPK_GROUPS))

    gmax = lax.fori_loop(0, nch, body, jnp.full((rows, TOPK_GROUPS), INT_MIN, I32))
    lo = _lane_allreduce(jnp.minimum, _fold_lanes(jnp.minimum, gmax, LANES))[:, :1]
    top = _lane_allreduce(jnp.maximum, _fold_lanes(jnp.maximum, gmax, LANES))[:, :1]
    return lo, top


def _select_threshold(keys_ref, nch, cw, topk, idx_bits):
    rows = keys_ref.shape[0]
    kf = float(topk)
    lo, top = _row_bounds(keys_ref, nch, cw)
    hi = top + 1

    def n_open(lo, hi):
        return jnp.max(jnp.where(hi - 1 > lo, 1.0, 0.0))

    def cond(st):
        return (st[2] > 0.5) & (st[3] < 40)

    def body(st):
        lo, hi, _, it = st
        mid = (lo & hi) + ((lo ^ hi) >> 1)
        cnt = _count(keys_ref, nch, cw, lambda k, i: k >= mid)
        ge = cnt >= kf
        lo = jnp.where(ge, mid, lo)
        hi = jnp.where(cnt == kf, mid + 1, jnp.where(ge, hi, mid))
        return lo, hi, n_open(lo, hi), it + 1

    tau, _, _, _ = lax.while_loop(cond, body, (lo, hi, n_open(lo, hi), jnp.int32(0)))
    n_ge = _count(keys_ref, nch, cw, lambda k, i: k >= tau)
    has_k = tau > INT_MIN
    tied = has_k & (n_ge > kf)
    any_tied = jnp.max(jnp.where(tied, 1.0, 0.0)) > 0.5

    def resolve(_):
        need = kf - _count(keys_ref, nch, cw, lambda k, i: k > tau)
        lim = jnp.zeros((rows, 1), I32)
        for b in range(idx_bits - 1, -1, -1):
            cand = lim + (1 << b)
            cnt = _count(keys_ref, nch, cw, lambda k, i: (k == tau) & (i < cand))
            lim = jnp.where(cnt < need, cand, lim)
        return lim

    lim = lax.cond(any_tied, resolve, lambda _: jnp.zeros((rows, 1), I32), 0)
    jlim = jnp.where(tied, lim, jnp.where(has_k, jnp.int32(2 ** 31 - 1), jnp.int32(-1)))
    return tau, jlim


def _keys_to_bias(keys_ref, nch, cw, tau, jlim):
    def body(c, carry):
        off = pl.multiple_of(c * cw, cw)
        key = keys_ref[:, pl.ds(off, cw)]
        idx = off + lax.broadcasted_iota(I32, key.shape, 1)
        sel = (key > tau) | ((key == tau) & (idx <= jlim))
        keys_ref[:, pl.ds(off, cw)] = pltpu.bitcast(jnp.where(sel, 0.0, NEG), I32)
        return carry

    lax.fori_loop(0, nch, body, 0)


def _flash_update(sl, d, v, m_scr, acc_scr):
    reps = d.shape[1] // LANES
    m_old = m_scr[sl]
    m_new = jnp.maximum(m_old, jnp.max(d, axis=1, keepdims=True))
    alpha = jnp.exp2(m_old - m_new)
    p = jnp.exp2(d - jnp.tile(m_new, (1, reps)))
    v_ones = jnp.concatenate([v, jnp.ones(v.shape, BF16)], axis=1)
    acc_scr[sl] = jnp.tile(alpha, (1, 2)) * acc_scr[sl] + jnp.dot(p.astype(BF16), v_ones,
                                                                   preferred_element_type=F32)
    m_scr[sl] = m_new


def _flash_init(m_scr, acc_scr):
    m_scr[...] = jnp.full(m_scr.shape, NEG, F32)
    acc_scr[...] = jnp.zeros(acc_scr.shape, F32)


def _flash_result(acc, guard=False):
    l = acc[:, HEAD_DIM:]
    return acc[:, :HEAD_DIM] / (jnp.maximum(l, 1e-30) if guard else l)


def _causal_steps(n_q, chunks_of):
    qb, cb = [], []
    for i in range(n_q):
        for c in range(chunks_of(i)):
            qb.append(i)
            cb.append(c)
    return jnp.asarray(qb, I32), jnp.asarray(cb, I32)


IDX_TN = 256


def _dsa_prompt_kernel(qb_ref, cb_ref, qi_ref, misc_ref, kit_ref, qa_ref, k_ref, v_ref, o_ref,
                       keys_scr, qih_scr, wib_scr, d_scr, m_scr, acc_scr,
                       *, tq, tk, topk, idx_bits):
    step = pl.program_id(0)
    i = qb_ref[step]
    c = cb_ref[step]
    t0 = i * tq
    nch = (t0 + tq + tk - 1) // tk
    rt = 64

    @pl.when(c == 0)
    def _():
        for hh in range(IDX_HEADS):
            qih_scr[hh] = qi_ref[:, hh * IDX_DIM:(hh + 1) * IDX_DIM]
            wib_scr[hh] = jnp.broadcast_to(misc_ref[:, MISC_WI + hh:MISC_WI + hh + 1], (tq, LANES))
        qpos = t0 + lax.broadcasted_iota(I32, (tq, 1), 0)

        def sub_body(kc, carry):
            off = pl.multiple_of(kc * IDX_TN, IDX_TN)
            d_scr[...] = jnp.dot(qih_scr[...].reshape(IDX_HEADS * tq, IDX_DIM),
                                 kit_ref[:, pl.ds(off, IDX_TN)], preferred_element_type=F32)
            kpos = off + lax.broadcasted_iota(I32, (rt, IDX_TN), 1)
            for r in range(tq // rt):
                score = jnp.zeros((rt, IDX_TN), F32)
                for hh in range(IDX_HEADS):
                    w = wib_scr[hh, r * rt:(r + 1) * rt, :]
                    dd = d_scr[hh * tq + r * rt:hh * tq + (r + 1) * rt, :]
                    score = score + jnp.tile(w, (1, IDX_TN // LANES)) * jnp.maximum(dd, 0.0)
                keys_scr[r * rt:(r + 1) * rt, pl.ds(off, IDX_TN)] = jnp.where(
                    kpos <= qpos[r * rt:(r + 1) * rt], _sort_key(score), INT_MIN)
            return carry

        lax.fori_loop(0, nch * (tk // IDX_TN), sub_body, 0)
        tau, jl = _select_threshold(keys_scr, nch, tk, topk, idx_bits)
        _keys_to_bias(keys_scr, nch, tk, tau, jl)
        _flash_init(m_scr, acc_scr)

    bias = pltpu.bitcast(keys_scr[:, pl.ds(pl.multiple_of(c * tk, tk), tk)], F32)
    group = A_HEADS // A_KV_HEADS
    for kv in range(A_KV_HEADS):
        kk = k_ref[:, kv * HEAD_DIM:(kv + 1) * HEAD_DIM]
        vv = v_ref[:, kv * HEAD_DIM:(kv + 1) * HEAD_DIM]
        for g in range(group):
            h = kv * group + g
            d = _nt_dot(qa_ref[:, h * HEAD_DIM:(h + 1) * HEAD_DIM], kk) + bias
            _flash_update(h, d, vv, m_scr, acc_scr)

    @pl.when(c == nch - 1)
    def _():
        for h in range(A_HEADS):
            o_ref[:, h * HEAD_DIM:(h + 1) * HEAD_DIM] = _flash_result(acc_scr[h]).astype(o_ref.dtype)


def _dsa_prompt(slab, misc, kit, tq, tk):
    rows = slab.shape[0]
    topk = min(DSA_TOPK, rows // 4)
    qb, cb = _causal_steps(rows // tq, lambda i: (i * tq + tq + tk - 1) // tk)
    kern = functools.partial(_dsa_prompt_kernel, tq=tq, tk=tk, topk=topk,
                             idx_bits=max(1, (rows - 1).bit_length()))
    grid_spec = pltpu.PrefetchScalarGridSpec(
        num_scalar_prefetch=2,
        grid=(int(qb.shape[0]),),
        in_specs=[pl.BlockSpec((tq, I_Q_COLS), lambda s, qb, cb: (qb[s], COL_QI // I_Q_COLS)),
                  pl.BlockSpec((tq, LANES), lambda s, qb, cb: (qb[s], 0)),
                  pl.BlockSpec((IDX_DIM, rows), lambda s, qb, cb: (0, 0)),
                  pl.BlockSpec((tq, A_Q_COLS), lambda s, qb, cb: (qb[s], COL_QA // A_Q_COLS)),
                  pl.BlockSpec((tk, A_KV_COLS), lambda s, qb, cb: (cb[s], COL_KA // A_KV_COLS)),
                  pl.BlockSpec((tk, A_KV_COLS), lambda s, qb, cb: (cb[s], COL_VA // A_KV_COLS))],
        out_specs=pl.BlockSpec((tq, A_Q_COLS), lambda s, qb, cb: (qb[s], 0)),
        scratch_shapes=[pltpu.VMEM((tq, rows), I32),
                        pltpu.VMEM((IDX_HEADS, tq, IDX_DIM), BF16),
                        pltpu.VMEM((IDX_HEADS, tq, LANES), F32),
                        pltpu.VMEM((IDX_HEADS * tq, IDX_TN), F32),
                        pltpu.VMEM((A_HEADS, tq, LANES), F32),
                        pltpu.VMEM((A_HEADS, tq, 2 * HEAD_DIM), F32)])
    return pl.pallas_call(
        kern, grid_spec=grid_spec, name="dsa_prompt",
        out_shape=jax.ShapeDtypeStruct((rows, A_Q_COLS), BF16),
        compiler_params=pltpu.CompilerParams(
            dimension_semantics=("arbitrary",), vmem_limit_bytes=VMEM_LIMIT),
    )(qb, cb, slab, misc, kit, slab, slab, slab)


def _gate_topk(gate, n_valid, ksel):
    lane = lax.broadcasted_iota(I32, gate.shape, 1)
    lanef = lane.astype(F32)
    gate = jnp.where(lane < n_valid, gate, NEG)
    sel = jnp.zeros(gate.shape, F32)
    for _ in range(ksel):
        mx = jnp.max(gate, axis=1, keepdims=True)
        first = jnp.min(jnp.where(gate == mx, lanef, 1e9), axis=1, keepdims=True)
        hit = (lanef == first) & (mx > 0.5 * NEG)
        sel = jnp.where(hit, 1.0, sel)
        gate = jnp.where(hit, NEG, gate)
    return sel


def _block_bias_rows(sel):
    return ((sel - 1.0) * (-NEG)).astype(BF16)


def _block_expansion(n_keys, blk0, keys_on_rows):
    shape = (n_keys, LANES) if keys_on_rows else (LANES, n_keys)
    kdim, bdim = (0, 1) if keys_on_rows else (1, 0)
    kblk = lax.broadcasted_iota(I32, shape, kdim) // MOBA_BLOCK + blk0
    return jnp.where(kblk == lax.broadcasted_iota(I32, shape, bdim), 1.0, 0.0).astype(BF16)


def _block_means_kernel(k_ref, o_ref):
    o_ref[0] = jnp.sum(k_ref[...], axis=0, keepdims=True) * (1.0 / MOBA_BLOCK)


def _block_means(kb):
    nb = kb.shape[0] // MOBA_BLOCK
    out = pl.pallas_call(
        _block_means_kernel,
        grid=(nb,),
        in_specs=[pl.BlockSpec((MOBA_BLOCK, B_KV_COLS), lambda n: (n, 0))],
        out_specs=pl.BlockSpec((1, 1, B_KV_COLS), lambda n: (n, 0, 0)),
        out_shape=jax.ShapeDtypeStruct((nb, 1, B_KV_COLS), F32),
        name="block_means",
    )(kb)
    return out.reshape(nb, B_KV_COLS)


def _moba_prompt_kernel(qb_ref, cb_ref, q_ref, means_ref, k_ref, v_ref, o_ref,
                        qaug_scr, m_scr, acc_scr, *, tq, tk, ksel):
    step = pl.program_id(0)
    i = qb_ref[step]
    c = cb_ref[step]
    t0 = i * tq
    qblk = t0 // MOBA_BLOCK
    bpc = tk // MOBA_BLOCK
    nch = qblk // bpc + 1
    group = B_HEADS // B_KV_HEADS

    @pl.when(c == 0)
    def _():
        lane = lax.broadcasted_iota(I32, (tq, LANES), 1)
        for h in range(B_HEADS):
            kv = h // group
            q = q_ref[:, h * HEAD_DIM:(h + 1) * HEAD_DIM]
            gate = _nt_dot(q, means_ref[:, kv * HEAD_DIM:(kv + 1) * HEAD_DIM])
            sel = jnp.where(lane == qblk, 1.0, _gate_topk(gate, qblk, ksel))
            qaug_scr[h] = jnp.concatenate([q, _block_bias_rows(sel)], axis=1)
        _flash_init(m_scr, acc_scr)

    e_t = _block_expansion(tk, c * bpc, keys_on_rows=True)
    qlim = t0 + lax.broadcasted_iota(I32, (tq, 1), 0) + jnp.where(c == nch - 1, 0, 2 ** 30)
    kpos = c * tk + lax.broadcasted_iota(I32, (tq, tk), 1)
    causal = jnp.where(kpos <= qlim, 0.0, NEG)
    for kv in range(B_KV_HEADS):
        rhs = jnp.concatenate([k_ref[:, kv * HEAD_DIM:(kv + 1) * HEAD_DIM], e_t], axis=1)
        vv = v_ref[:, kv * HEAD_DIM:(kv + 1) * HEAD_DIM]
        for g in range(group):
            h = kv * group + g
            _flash_update(h, _nt_dot(qaug_scr[h], rhs) + causal, vv, m_scr, acc_scr)

    @pl.when(c == nch - 1)
    def _():
        for h in range(B_HEADS):
            o_ref[:, h * HEAD_DIM:(h + 1) * HEAD_DIM] = _flash_result(acc_scr[h]).astype(o_ref.dtype)


def _moba_prompt(slab, means_pad, tq, tk):
    rows = slab.shape[0]
    ksel = min(MOBA_TOPK, (rows - 1) // MOBA_BLOCK)
    assert tq == MOBA_BLOCK and tk % MOBA_BLOCK == 0 and means_pad.shape[0] == LANES
    qb, cb = _causal_steps(rows // tq, lambda i: (i * tq) // tk + 1)
    kern = functools.partial(_moba_prompt_kernel, tq=tq, tk=tk, ksel=ksel)
    grid_spec = pltpu.PrefetchScalarGridSpec(
        num_scalar_prefetch=2,
        grid=(int(qb.shape[0]),),
        in_specs=[pl.BlockSpec((tq, B_Q_COLS), lambda s, qb, cb: (qb[s], COL_QB // B_Q_COLS)),
                  pl.BlockSpec((LANES, B_KV_COLS), lambda s, qb, cb: (0, 0)),
                  pl.BlockSpec((tk, B_KV_COLS), lambda s, qb, cb: (cb[s], COL_KB // B_KV_COLS)),
                  pl.BlockSpec((tk, B_KV_COLS), lambda s, qb, cb: (cb[s], COL_VB // B_KV_COLS))],
        out_specs=pl.BlockSpec((tq, B_Q_COLS), lambda s, qb, cb: (qb[s], 0)),
        scratch_shapes=[pltpu.VMEM((B_HEADS, tq, 2 * HEAD_DIM), BF16),
                        pltpu.VMEM((B_HEADS, tq, LANES), F32),
                        pltpu.VMEM((B_HEADS, tq, 2 * HEAD_DIM), F32)])
    return pl.pallas_call(
        kern, grid_spec=grid_spec, name="moba_prompt",
        out_shape=jax.ShapeDtypeStruct((rows, B_Q_COLS), BF16),
        compiler_params=pltpu.CompilerParams(
            dimension_semantics=("arbitrary",), vmem_limit_bytes=VMEM_LIMIT),
    )(qb, cb, slab, means_pad, slab, slab)


TPAD = SUBLANES
SROWS = 8 * TPAD
KV_ROWS = SROWS // 4
PAGE_ROWS = PAGE_SIZE * 4


def _page_head(ref, kv):
    return ref[0, pl.ds(kv, PAGE_SIZE, stride=4), :]


def _paged_spec(shape, n_pages, per_step, first_step, n_steps, j):
    def index(b, s, pt):
        local = jnp.clip(s - first_step, 0, n_steps - 1)
        return (pt[b * n_pages + local * per_step + j], 0, 0)
    return pl.BlockSpec(shape, index)


def _dsa_sample_kernel(pt_ref, qi_ref, wi_ref, *rest, n_pages, pa, pb, t_new, topk, idx_bits, cw):
    kidx_refs, rest = rest[:pa], rest[pa:]
    kin_ref, qa_ref = rest[:2]
    k_refs, v_refs = rest[2:2 + pb], rest[2 + pb:2 + 2 * pb]
    kn_ref, vn_ref, o_ref, keys_scr, wib_scr, m_scr, acc_scr = rest[2 + 2 * pb:]
    s = pl.program_id(1)
    na, nb = n_pages // pa, n_pages // pb
    past = n_pages * PAGE_SIZE
    ntot = keys_scr.shape[1]
    trow = lax.broadcasted_iota(I32, (TPAD, 1), 0)

    def index_keys(kd, off):
        width = kd.shape[1]
        x = jnp.tile(wib_scr[...], (1, width // LANES)) * jnp.maximum(
            jnp.dot(qi_ref[0], kd, preferred_element_type=F32), 0.0)
        score = x[:TPAD]
        for hh in range(1, IDX_HEADS):
            score = score + x[hh * TPAD:(hh + 1) * TPAD]
        kpos = off + lax.broadcasted_iota(I32, (TPAD, width), 1)
        visible = (kpos <= past + trow) & (trow < t_new)
        keys_scr[:, pl.ds(off, width)] = jnp.where(visible, _sort_key(score), INT_MIN)

    @pl.when(s == 0)
    def _():
        wib_scr[...] = jnp.broadcast_to(wi_ref[0], wib_scr.shape)
        keys_scr[:, past:] = jnp.full((TPAD, ntot - past), INT_MIN, I32)

    @pl.when(s < na)
    def _():
        kd = jnp.concatenate([r[0].astype(BF16) for r in kidx_refs], axis=1)
        index_keys(kd, pl.multiple_of(s * (pa * PAGE_SIZE), pa * PAGE_SIZE))

    @pl.when(s == na - 1)
    def _():
        index_keys(kin_ref[0], past)
        tau, jl = _select_threshold(keys_scr, ntot // cw, cw, topk, idx_bits)
        _keys_to_bias(keys_scr, ntot // cw, cw, tau, jl)
        _flash_init(m_scr, acc_scr)

    def attend(key_of, val_of, off, width):
        bias = pltpu.bitcast(keys_scr[:, pl.ds(off, width)], F32)
        bias = jnp.concatenate([bias] * (KV_ROWS // TPAD), axis=0)
        for kv in range(A_KV_HEADS):
            rows = slice(kv * KV_ROWS, (kv + 1) * KV_ROWS)
            _flash_update(rows, _nt_dot(qa_ref[0, rows], key_of(kv)) + bias, val_of(kv), m_scr, acc_scr)

    def gather(refs, kv):
        return jnp.concatenate([_page_head(r, kv) for r in refs], axis=0).astype(BF16)

    @pl.when(s >= na)
    def _():
        off = pl.multiple_of((s - na) * (pb * PAGE_SIZE), pb * PAGE_SIZE)
        attend(lambda kv: gather(k_refs, kv), lambda kv: gather(v_refs, kv), off, pb * PAGE_SIZE)

    @pl.when(s == na + nb - 1)
    def _():
        attend(lambda kv: kn_ref[0, kv], lambda kv: vn_ref[0, kv], past, PAGE_SIZE)
        o_ref[0] = _flash_result(acc_scr[...], guard=True)


def _dsa_sample(pt_flat, n_pages, t_new, qi_s, wi_s, kidx_pages, ki_new_t, qa_s, k_pages, v_pages, ka_new, va_new,
                pa, pb):
    ns = qi_s.shape[0]
    past = n_pages * PAGE_SIZE
    topk = min(DSA_TOPK, (past + t_new) // 4)
    ntot = -(-(past + PAGE_SIZE) // TOPK_GROUPS) * TOPK_GROUPS
    cw = ntot
    na, nb = n_pages // pa, n_pages // pb
    seq3 = lambda s1, s2: pl.BlockSpec((1, s1, s2), lambda b, s, pt: (b, 0, 0))
    seq4 = pl.BlockSpec((1, A_KV_HEADS, PAGE_SIZE, HEAD_DIM), lambda b, s, pt: (b, 0, 0, 0))
    kern = functools.partial(_dsa_sample_kernel, n_pages=n_pages, pa=pa, pb=pb, t_new=t_new, topk=topk,
                             idx_bits=max(1, (ntot - 1).bit_length()), cw=cw)
    page_kv = lambda j: _paged_spec((1, PAGE_ROWS, HEAD_DIM), n_pages, pb, na, nb, j)
    grid_spec = pltpu.PrefetchScalarGridSpec(
        num_scalar_prefetch=1,
        grid=(ns, na + nb),
        in_specs=[seq3(IDX_HEADS * TPAD, IDX_DIM), seq3(IDX_HEADS * TPAD, 1)]
                 + [_paged_spec((1, IDX_DIM, PAGE_SIZE), n_pages, pa, 0, na, j) for j in range(pa)]
                 + [seq3(IDX_DIM, PAGE_SIZE), seq3(SROWS, HEAD_DIM)]
                 + [page_kv(j) for j in range(pb)] + [page_kv(j) for j in range(pb)]
                 + [seq4, seq4],
        out_specs=seq3(SROWS, HEAD_DIM),
        scratch_shapes=[pltpu.VMEM((TPAD, ntot), I32),
                        pltpu.VMEM((IDX_HEADS * TPAD, LANES), F32),
                        pltpu.VMEM((SROWS, LANES), F32),
                        pltpu.VMEM((SROWS, 2 * HEAD_DIM), F32)])
    return pl.pallas_call(
        kern, grid_spec=grid_spec, name="dsa_sample",
        out_shape=jax.ShapeDtypeStruct((ns, SROWS, HEAD_DIM), F32),
        compiler_params=pltpu.CompilerParams(
            dimension_semantics=("arbitrary", "arbitrary"), vmem_limit_bytes=VMEM_LIMIT),
    )(pt_flat, qi_s, wi_s, *([kidx_pages] * pa), ki_new_t, qa_s, *([k_pages] * pb), *([v_pages] * pb),
      ka_new, va_new)


def _moba_sample_kernel(pt_ref, q_ref, *rest, n_pages, pb, t_new, ksel):
    k_refs, v_refs = rest[:pb], rest[pb:2 * pb]
    kn_ref, vn_ref, o_ref, bsum_scr, s_scr, seln_scr, m_scr, acc_scr = rest[2 * pb:]
    s = pl.program_id(1)
    nst = n_pages // pb
    past = n_pages * PAGE_SIZE
    nbp = past // MOBA_BLOCK
    bps = pb * PAGE_SIZE // MOBA_BLOCK
    width = pb * PAGE_SIZE
    tok = lax.broadcasted_iota(I32, (KV_ROWS, 1), 0) % TPAD
    pad_bias = jnp.where(tok < t_new, 0.0, NEG)
    kv_rows = lambda kv: slice(kv * KV_ROWS, (kv + 1) * KV_ROWS)

    @pl.when(s == 0)
    def _():
        bsum_scr[...] = jnp.zeros(bsum_scr.shape, F32)

    @pl.when(s < nst)
    def _():
        off = pl.multiple_of(s * width, width)
        for kv in range(B_KV_HEADS):
            kf = jnp.concatenate([_page_head(r, kv) for r in k_refs], axis=0)
            bsum_scr[kv, pl.ds(pl.multiple_of(s * bps, bps), bps), :] = jnp.sum(
                kf.reshape(bps, MOBA_BLOCK, HEAD_DIM), axis=1)
            s_scr[kv_rows(kv), pl.ds(off, width)] = _nt_dot(q_ref[0, kv_rows(kv)], kf.astype(BF16))

    @pl.when(s == nst - 1)
    def _():
        for kv in range(B_KV_HEADS):
            q = q_ref[0, kv_rows(kv)]
            s_scr[kv_rows(kv), pl.ds(past, PAGE_SIZE)] = _nt_dot(q, kn_ref[0, kv])
            means = (bsum_scr[kv] * (1.0 / MOBA_BLOCK)).astype(BF16)
            seln_scr[kv_rows(kv)] = _block_bias_rows(_gate_topk(_nt_dot(q, means), nbp, ksel))
        _flash_init(m_scr, acc_scr)

    @pl.when(s >= nst)
    def _():
        sb = s - nst
        off = pl.multiple_of(sb * width, width)
        expand = _block_expansion(width, sb * bps, keys_on_rows=False)
        for kv in range(B_KV_HEADS):
            bias = jnp.dot(seln_scr[kv_rows(kv)], expand, preferred_element_type=F32) + pad_bias
            vv = jnp.concatenate([_page_head(r, kv) for r in v_refs], axis=0).astype(BF16)
            _flash_update(kv_rows(kv), s_scr[kv_rows(kv), pl.ds(off, width)] + bias, vv, m_scr, acc_scr)

    @pl.when(s == 2 * nst - 1)
    def _():
        kpos = past + lax.broadcasted_iota(I32, (KV_ROWS, PAGE_SIZE), 1)
        bias = jnp.where(kpos <= past + tok, 0.0, NEG) + pad_bias
        for kv in range(B_KV_HEADS):
            _flash_update(kv_rows(kv), s_scr[kv_rows(kv), pl.ds(past, PAGE_SIZE)] + bias, vn_ref[0, kv],
                          m_scr, acc_scr)
        o_ref[0] = _flash_result(acc_scr[...], guard=True)


def _moba_sample(pt_flat, n_pages, t_new, qb_s, k_pages, v_pages, kb_new, vb_new, pb):
    ns = qb_s.shape[0]
    past = n_pages * PAGE_SIZE
    nbp = past // MOBA_BLOCK
    assert nbp <= LANES and (pb * PAGE_SIZE // MOBA_BLOCK) % SUBLANES == 0
    ksel = min(MOBA_TOPK, nbp)
    nst = n_pages // pb
    seq3 = lambda s1, s2: pl.BlockSpec((1, s1, s2), lambda b, s, pt: (b, 0, 0))
    seq4 = pl.BlockSpec((1, B_KV_HEADS, PAGE_SIZE, HEAD_DIM), lambda b, s, pt: (b, 0, 0, 0))
    kern = functools.partial(_moba_sample_kernel, n_pages=n_pages, pb=pb, t_new=t_new, ksel=ksel)
    grid_spec = pltpu.PrefetchScalarGridSpec(
        num_scalar_prefetch=1,
        grid=(ns, 2 * nst),
        in_specs=[seq3(SROWS, HEAD_DIM)]
                 + [_paged_spec((1, PAGE_ROWS, HEAD_DIM), n_pages, pb, 0, nst, j) for j in range(pb)]
                 + [_paged_spec((1, PAGE_ROWS, HEAD_DIM), n_pages, pb, nst, nst, j) for j in range(pb)]
                 + [seq4, seq4],
        out_specs=seq3(SROWS, HEAD_DIM),
        scratch_shapes=[pltpu.VMEM((B_KV_HEADS, LANES, HEAD_DIM), F32),
                        pltpu.VMEM((SROWS, past + PAGE_SIZE), F32),
                        pltpu.VMEM((SROWS, LANES), BF16),
                        pltpu.VMEM((SROWS, LANES), F32),
                        pltpu.VMEM((SROWS, 2 * HEAD_DIM), F32)])
    return pl.pallas_call(
        kern, grid_spec=grid_spec, name="moba_sample",
        out_shape=jax.ShapeDtypeStruct((ns, SROWS, HEAD_DIM), F32),
        compiler_params=pltpu.CompilerParams(
            dimension_semantics=("arbitrary", "arbitrary"), vmem_limit_bytes=VMEM_LIMIT),
    )(pt_flat, qb_s, *([k_pages] * pb), *([v_pages] * pb), kb_new, vb_new)


def _memory_kv_kernel(mem_ref, g_ref, w_ref, k_ref, v_ref):
    h = _rms(mem_ref[...], g_ref[...]).astype(BF16)
    kv = jnp.dot(h, w_ref[...], preferred_element_type=F32)
    k_ref[...] = kv[:, :MEM_WIDTH]
    v_ref[...] = kv[:, MEM_WIDTH:]


def _memory_kv(mem, g, w_bf):
    m = mem.shape[0]
    return pl.pallas_call(
        _memory_kv_kernel,
        out_shape=[jax.ShapeDtypeStruct((m, MEM_WIDTH), F32), jax.ShapeDtypeStruct((m, MEM_WIDTH), F32)],
        name="memory_kv",
        compiler_params=pltpu.CompilerParams(vmem_limit_bytes=VMEM_LIMIT),
    )(mem, g, w_bf)


def _outproj_cross_kernel(x_ref, oa_ref, ob_ref, woa_ref, wob_ref, g_ref, wq_ref, mk_ref, mv_ref, wo_ref,
                          y_ref, *, rows_per_seq):
    x1 = x_ref[...] + jnp.dot(oa_ref[...], woa_ref[...], preferred_element_type=F32) \
        + jnp.dot(ob_ref[...], wob_ref[...], preferred_element_type=F32)
    hc = _rms(x1, g_ref[...]).astype(BF16)
    q = jnp.dot(hc, wq_ref[...], preferred_element_type=F32).astype(BF16)
    tm = q.shape[0]
    nk = mk_ref.shape[0]
    scale = MEM_HEAD_DIM ** -0.5
    if rows_per_seq is not None:
        rseq = (pl.program_id(0) * tm + lax.broadcasted_iota(I32, (tm, 1), 0)) // rows_per_seq
        kseq = lax.broadcasted_iota(I32, (tm, nk), 1) // MEM_TOKENS
        mask = kseq == rseq
    outs = []
    for h in range(MEM_HEADS):
        sl = slice(h * MEM_HEAD_DIM, (h + 1) * MEM_HEAD_DIM)
        s = _nt_dot(q[:, sl], mk_ref[:, sl]) * scale
        if rows_per_seq is not None:
            s = jnp.where(mask, s, NEG)
        m = jnp.max(s, axis=1, keepdims=True)
        e = jnp.exp(s - m)
        p = e / jnp.sum(e, axis=1, keepdims=True)
        outs.append(jnp.dot(p.astype(BF16), mv_ref[:, sl], preferred_element_type=F32))
    o = jnp.concatenate(outs, axis=1).astype(BF16)
    y_ref[...] = x1 + jnp.dot(o, wo_ref[...], preferred_element_type=F32)


def _outproj_cross(x, oa, ob, w_out_bf, g_cross, w_mq_bf, mk_bf, mv_bf, w_mo_bf, tm, rows_per_seq):
    rows = x.shape[0]
    nk = mk_bf.shape[0]
    row_spec = lambda w: pl.BlockSpec((tm, w), lambda i: (i, 0))
    full = lambda a, b: pl.BlockSpec((a, b), lambda i: (0, 0))
    kern = functools.partial(_outproj_cross_kernel, rows_per_seq=rows_per_seq)
    return pl.pallas_call(
        kern,
        grid=(rows // tm,),
        in_specs=[row_spec(D_MODEL), row_spec(A_Q_COLS), row_spec(B_Q_COLS),
                  pl.BlockSpec((A_Q_COLS, D_MODEL), lambda i: (0, 0)),
                  pl.BlockSpec((B_Q_COLS, D_MODEL), lambda i: (1, 0)),
                  full(1, D_MODEL), full(D_MODEL, MEM_WIDTH), full(nk, MEM_WIDTH), full(nk, MEM_WIDTH),
                  full(MEM_WIDTH, D_MODEL)],
        out_specs=row_spec(D_MODEL),
        out_shape=jax.ShapeDtypeStruct((rows, D_MODEL), F32),
        name="outproj_cross",
        compiler_params=pltpu.CompilerParams(
            dimension_semantics=("parallel",), vmem_limit_bytes=VMEM_LIMIT),
    )(x, oa, ob, w_out_bf, w_out_bf, g_cross, w_mq_bf, mk_bf, mv_bf, w_mo_bf)


def _mlp_final_kernel(x_ref, g_ref, wu_ref, wd_ref, gf_ref, y_ref, h_scr, acc_scr):
    f = pl.program_id(1)

    @pl.when(f == 0)
    def _():
        h_scr[...] = _rms(x_ref[...], g_ref[...]).astype(BF16)
        acc_scr[...] = jnp.zeros(acc_scr.shape, F32)

    u = jnp.maximum(jnp.dot(h_scr[...], wu_ref[...], preferred_element_type=F32), 0.0)
    acc_scr[...] += jnp.dot((u * u).astype(BF16), wd_ref[...], preferred_element_type=F32)

    @pl.when(f == pl.num_programs(1) - 1)
    def _():
        y_ref[...] = _rms(x_ref[...] + acc_scr[...], gf_ref[...])


def _mlp_final(x, g_ffn, w_up_bf, w_down_bf, g_final, tm, tf):
    rows = x.shape[0]
    d_ff = w_up_bf.shape[1]
    return pl.pallas_call(
        _mlp_final_kernel,
        grid=(rows // tm, d_ff // tf),
        in_specs=[pl.BlockSpec((tm, D_MODEL), lambda i, f: (i, 0)),
                  pl.BlockSpec((1, D_MODEL), lambda i, f: (0, 0)),
                  pl.BlockSpec((D_MODEL, tf), lambda i, f: (0, f)),
                  pl.BlockSpec((tf, D_MODEL), lambda i, f: (f, 0)),
                  pl.BlockSpec((1, D_MODEL), lambda i, f: (0, 0))],
        out_specs=pl.BlockSpec((tm, D_MODEL), lambda i, f: (i, 0)),
        out_shape=jax.ShapeDtypeStruct((rows, D_MODEL), F32),
        scratch_shapes=[pltpu.VMEM((tm, D_MODEL), BF16), pltpu.VMEM((tm, D_MODEL), F32)],
        name="mlp_final",
        compiler_params=pltpu.CompilerParams(
            dimension_semantics=("parallel", "arbitrary"), vmem_limit_bytes=VMEM_LIMIT),
    )(x, g_ffn, w_up_bf, w_down_bf, g_final)


def _row_tile(rows, want):
    t = min(rows, want)
    while rows % t:
        t //= 2
    return t


def _pad_rows(a, axis, size):
    pad = [(0, 0)] * a.ndim
    pad[axis] = (0, size - a.shape[axis])
    return jnp.pad(a, pad)


def _heads_first(a, ns, t_new, heads, dim, t_pad):
    a = a.reshape(ns, t_new, heads, dim).transpose(0, 2, 1, 3)
    return _pad_rows(a, 2, t_pad)


def kernel(x_prompt, x_sample, cache_k_a, cache_v_a, cache_kidx, cache_k_b, cache_v_b, cache_mem_k,
           cache_mem_v, page_table, mem_prompt, g_mix, w_in, g_kidx, b_kidx, w_out, g_cross, w_mq, g_mem,
           w_mkv, w_mo, g_ffn, w_up, w_down, g_final):
    batch, s_len, _ = x_prompt.shape
    ns, t_new, _ = x_sample.shape
    depth = w_in.shape[0]
    n_pages = page_table.shape[1]
    past = n_pages * PAGE_SIZE
    n_phys = cache_k_a.shape[1]
    assert batch == 1 and depth == 1
    assert s_len % 1024 == 0 and s_len // MOBA_BLOCK <= LANES
    assert past % MOBA_BLOCK == 0 and t_new <= TPAD
    l = 0
    row = lambda v: v.reshape(1, -1)

    w_in_p = _permute_w_in(w_in[l])
    gk = row(jnp.concatenate([g_kidx[l], jnp.zeros((LANES - IDX_DIM,), F32)]))
    bk = row(jnp.concatenate([b_kidx[l], jnp.zeros((LANES - IDX_DIM,), F32)]))
    w_out_bf = w_out[l].astype(BF16)
    w_mq_bf = w_mq[l].astype(BF16)
    w_mo_bf = w_mo[l].astype(BF16)
    w_up_bf = w_up[l].astype(BF16)
    w_down_bf = w_down[l].astype(BF16)

    xp = x_prompt.reshape(s_len, D_MODEL)
    tabs_p = _rope_tables(jnp.arange(s_len))
    slab, ka, va, kb, vb, misc, misc_bf = _proj(xp, row(g_mix[l]), w_in_p, tabs_p, gk, bk, _row_tile(s_len, 512))
    kit = misc_bf[:, :IDX_DIM].T
    oa = _dsa_prompt(slab, misc, kit, tq=256, tk=1024)
    means_pad = _pad_rows(_block_means(kb).astype(BF16), 0, LANES)
    ob = _moba_prompt(slab, means_pad, tq=MOBA_BLOCK, tk=1024)
    mk, mv = _memory_kv(mem_prompt.reshape(MEM_TOKENS, D_MODEL), row(g_mem[l]), w_mkv[l].astype(BF16))
    xp2 = _outproj_cross(xp, oa, ob, w_out_bf, row(g_cross[l]), w_mq_bf, mk.astype(BF16), mv.astype(BF16),
                         w_mo_bf, _row_tile(s_len, 256), None)
    y_prompt = _mlp_final(xp2, row(g_ffn[l]), w_up_bf, w_down_bf, row(g_final), _row_tile(s_len, 512), 512)

    rows_s = ns * t_new
    xs = x_sample.reshape(rows_s, D_MODEL)
    pos_s = jnp.tile(past + jnp.arange(t_new), ns)
    tabs_s = _rope_tables(pos_s)
    rows_pad = -(-rows_s // SUBLANES) * SUBLANES
    xs_pad = _pad_rows(xs, 0, rows_pad)
    tabs_s = tuple(_pad_rows(t, 0, rows_pad) for t in tabs_s)
    slab_s, ka_s, va_s, kb_s, vb_s, misc_s, _ = _proj(xs_pad, row(g_mix[l]), w_in_p, tabs_s, gk, bk,
                                                      _row_tile(rows_pad, 128))
    slab_s, ka_s, va_s, kb_s, vb_s, misc_s = [a[:rows_s] for a in (slab_s, ka_s, va_s, kb_s, vb_s, misc_s)]

    pt_flat = page_table.reshape(-1).astype(I32)
    pages_per_step = min(8, n_pages)
    assert n_pages % pages_per_step == 0
    page_view = lambda c: c[l].reshape(n_phys, PAGE_ROWS, HEAD_DIM)
    kidx_view = jnp.swapaxes(cache_kidx[l], 1, 2)
    cols = lambda c0, w: slab_s[:, c0:c0 + w]
    qi_s = _heads_first(cols(COL_QI, I_Q_COLS), ns, t_new, IDX_HEADS, IDX_DIM, TPAD)
    qi_s = qi_s.reshape(ns, IDX_HEADS * TPAD, IDX_DIM)
    wi_s = _heads_first(misc_s[:, MISC_WI:MISC_WI + IDX_HEADS], ns, t_new, IDX_HEADS, 1, TPAD)
    wi_s = wi_s.reshape(ns, IDX_HEADS * TPAD, 1)
    ki_new_t = _pad_rows(cols(COL_MISC, IDX_DIM).reshape(ns, t_new, IDX_DIM).transpose(0, 2, 1), 2, PAGE_SIZE)
    q_rows = lambda c0: _heads_first(cols(c0, A_Q_COLS), ns, t_new, A_HEADS, HEAD_DIM, TPAD).reshape(
        ns, SROWS, HEAD_DIM)
    new_kv = lambda c0: _heads_first(cols(c0, A_KV_COLS), ns, t_new, A_KV_HEADS, HEAD_DIM, PAGE_SIZE)
    oa_s = _dsa_sample(pt_flat, n_pages, t_new, qi_s, wi_s, kidx_view, ki_new_t, q_rows(COL_QA),
                       page_view(cache_k_a), page_view(cache_v_a), new_kv(COL_KA), new_kv(COL_VA),
                       pages_per_step, pages_per_step)
    moba_pages = min(16, n_pages)
    assert n_pages % moba_pages == 0
    ob_s = _moba_sample(pt_flat, n_pages, t_new, q_rows(COL_QB), page_view(cache_k_b), page_view(cache_v_b),
                        new_kv(COL_KB), new_kv(COL_VB), moba_pages)
    tokens_first = lambda o: o.reshape(ns, A_HEADS, TPAD, HEAD_DIM)[:, :, :t_new].transpose(0, 2, 1, 3).reshape(
        rows_s, -1).astype(BF16)
    oa_s2 = _pad_rows(tokens_first(oa_s), 0, rows_pad)
    ob_s2 = _pad_rows(tokens_first(ob_s), 0, rows_pad)
    mk_s = cache_mem_k[l].reshape(ns * MEM_TOKENS, MEM_WIDTH).astype(BF16)
    mv_s = cache_mem_v[l].reshape(ns * MEM_TOKENS, MEM_WIDTH).astype(BF16)
    xs2 = _outproj_cross(xs_pad, oa_s2, ob_s2, w_out_bf, row(g_cross[l]), w_mq_bf, mk_s, mv_s, w_mo_bf,
                         rows_pad, t_new)
    y_sample = _mlp_final(xs2, row(g_ffn[l]), w_up_bf, w_down_bf, row(g_final), rows_pad, 512)[:rows_s]

    kv5 = lambda a, n, t: a.reshape(1, n, t, A_KV_HEADS, HEAD_DIM)
    return (y_prompt.reshape(batch, s_len, D_MODEL), y_sample.reshape(ns, t_new, D_MODEL),
            kv5(ka, batch, s_len), kv5(va, batch, s_len),
            misc[:, :IDX_DIM].reshape(1, batch, s_len, IDX_DIM),
            kv5(kb, batch, s_len), kv5(vb, batch, s_len),
            mk.reshape(1, batch, MEM_TOKENS, MEM_HEADS, MEM_HEAD_DIM),
            mv.reshape(1, batch, MEM_TOKENS, MEM_HEADS, MEM_HEAD_DIM),
            kv5(ka_s, ns, t_new), kv5(va_s, ns, t_new),
            misc_s[:, :IDX_DIM].reshape(1, ns, t_new, IDX_DIM),
            kv5(kb_s, ns, t_new), kv5(vb_s, ns, t_new))
```

```python
import functools

import jax
import jax.numpy as jnp
from jax import lax
from jax.experimental import pallas as pl
from jax.experimental.pallas import tpu as pltpu

F32 = jnp.float32
BF16 = jnp.bfloat16
I32 = jnp.int32

D_MODEL = 2048
HEAD_DIM = 128
A_HEADS = 8
A_KV_HEADS = 4
B_HEADS = 8
B_KV_HEADS = 4
IDX_HEADS = 16
IDX_DIM = 64
DSA_TOPK = 256
MOBA_BLOCK = 256
MOBA_TOPK = 3
MEM_TOKENS = 256
MEM_HEADS = 4
MEM_HEAD_DIM = 128
PAGE_SIZE = 128
ROPE_THETA = 500000.0
NORM_EPS = 1e-6

A_Q_COLS = A_HEADS * HEAD_DIM
A_KV_COLS = A_KV_HEADS * HEAD_DIM
I_Q_COLS = IDX_HEADS * IDX_DIM
B_Q_COLS = B_HEADS * HEAD_DIM
B_KV_COLS = B_KV_HEADS * HEAD_DIM
MEM_WIDTH = MEM_HEADS * MEM_HEAD_DIM

LANES = 128
SUBLANES = 8
VMEM_LIMIT = 56 * 1024 * 1024
INT_MIN = -(2 ** 31)
NEG = -1e30
LOG2E = 1.4426950408889634
SOFTMAX_EXP2_SCALE = HEAD_DIM ** -0.5 * LOG2E

PROJ_TN = 512
COL_QA, COL_KA, COL_VA, COL_QI, COL_QB, COL_KB, COL_VB, COL_MISC = 0, 1024, 1536, 2048, 3072, 4096, 4608, 5120
SLAB_COLS = COL_MISC + PROJ_TN
MISC_WI = IDX_DIM


def _nt_dot(a, b):
    return lax.dot_general(a, b, (((1,), (1,)), ((), ())), preferred_element_type=F32)


def _rms(x, g):
    return x * lax.rsqrt(jnp.mean(x * x, axis=-1, keepdims=True) + NORM_EPS) * g


def _rope(x, cos_t, sin_t, half, period):
    width = x.shape[1]
    reps = width // LANES
    c = jnp.concatenate([cos_t] * reps, axis=1) if reps > 1 else cos_t
    s = jnp.concatenate([sin_t] * reps, axis=1) if reps > 1 else sin_t
    lane = lax.broadcasted_iota(I32, x.shape, 1)
    first = (lane & (period - 1)) < half
    up = pltpu.roll(x, width - half, axis=1)
    dn = pltpu.roll(x, half, axis=1)
    return x * c + jnp.where(first, up, dn) * s


def _proj_kernel(x_ref, g_ref, w_ref, c128_ref, s128_ref, c64_ref, s64_ref, gk_ref, bk_ref,
                 slab_ref, ka_ref, va_ref, kb_ref, vb_ref, misc_ref, miscb_ref, h_scr):
    j = pl.program_id(1)

    @pl.when(j == 0)
    def _():
        h_scr[...] = _rms(x_ref[...], g_ref[...]).astype(BF16)

    acc = jnp.dot(h_scr[...], w_ref[...], preferred_element_type=F32)

    def rope128(v):
        return _rope(v, c128_ref[...], s128_ref[...], HEAD_DIM // 8, HEAD_DIM)

    is_q128 = (j == COL_QA // PROJ_TN) | (j == COL_QA // PROJ_TN + 1) | \
              (j == COL_QB // PROJ_TN) | (j == COL_QB // PROJ_TN + 1)

    @pl.when(is_q128)
    def _():
        slab_ref[...] = (rope128(acc) * SOFTMAX_EXP2_SCALE).astype(BF16)

    @pl.when(j == COL_KA // PROJ_TN)
    def _():
        r = rope128(acc)
        ka_ref[...] = r
        slab_ref[...] = r.astype(BF16)

    @pl.when(j == COL_KB // PROJ_TN)
    def _():
        r = rope128(acc)
        kb_ref[...] = r
        slab_ref[...] = r.astype(BF16)

    @pl.when(j == COL_VA // PROJ_TN)
    def _():
        va_ref[...] = acc
        slab_ref[...] = acc.astype(BF16)

    @pl.when(j == COL_VB // PROJ_TN)
    def _():
        vb_ref[...] = acc
        slab_ref[...] = acc.astype(BF16)

    @pl.when((j == COL_QI // PROJ_TN) | (j == COL_QI // PROJ_TN + 1))
    def _():
        slab_ref[...] = _rope(acc, c64_ref[...], s64_ref[...], IDX_DIM // 8, IDX_DIM).astype(BF16)

    @pl.when(j == COL_MISC // PROJ_TN)
    def _():
        y = acc[:, :LANES]
        lane = lax.broadcasted_iota(I32, y.shape, 1)
        is_ki = lane < IDX_DIM
        mu = jnp.sum(jnp.where(is_ki, y, 0.0), axis=-1, keepdims=True) * (1.0 / IDX_DIM)
        xc = y - mu
        var = jnp.sum(jnp.where(is_ki, xc * xc, 0.0), axis=-1, keepdims=True) * (1.0 / IDX_DIM)
        ln = xc * lax.rsqrt(var + NORM_EPS) * gk_ref[...] + bk_ref[...]
        ki = _rope(ln, c64_ref[...], s64_ref[...], IDX_DIM // 8, IDX_DIM)
        wi = y * (IDX_HEADS ** -0.5 * IDX_DIM ** -0.5)
        out = jnp.where(is_ki, ki, wi)
        misc_ref[...] = out
        miscb_ref[...] = out.astype(BF16)
        slab_ref[...] = jnp.concatenate(
            [out.astype(BF16), jnp.zeros((out.shape[0], PROJ_TN - LANES), BF16)], axis=1)


def _proj(x, g, w_perm, tabs, gk, bk, tm):
    rows = x.shape[0]
    c128, s128, c64, s64 = tabs
    n_col = SLAB_COLS // PROJ_TN
    row_spec = lambda w: pl.BlockSpec((tm, w), lambda i, j: (i, 0))
    vec_spec = lambda w: pl.BlockSpec((1, w), lambda i, j: (0, 0))
    return pl.pallas_call(
        _proj_kernel,
        grid=(rows // tm, n_col),
        in_specs=[row_spec(D_MODEL), vec_spec(D_MODEL),
                  pl.BlockSpec((D_MODEL, PROJ_TN), lambda i, j: (0, j)),
                  row_spec(LANES), row_spec(LANES), row_spec(LANES), row_spec(LANES),
                  vec_spec(LANES), vec_spec(LANES)],
        out_specs=[pl.BlockSpec((tm, PROJ_TN), lambda i, j: (i, j)),
                   row_spec(A_KV_COLS), row_spec(A_KV_COLS), row_spec(B_KV_COLS), row_spec(B_KV_COLS),
                   row_spec(LANES), row_spec(LANES)],
        out_shape=[jax.ShapeDtypeStruct((rows, SLAB_COLS), BF16),
                   jax.ShapeDtypeStruct((rows, A_KV_COLS), F32),
                   jax.ShapeDtypeStruct((rows, A_KV_COLS), F32),
                   jax.ShapeDtypeStruct((rows, B_KV_COLS), F32),
                   jax.ShapeDtypeStruct((rows, B_KV_COLS), F32),
                   jax.ShapeDtypeStruct((rows, LANES), F32),
                   jax.ShapeDtypeStruct((rows, LANES), BF16)],
        scratch_shapes=[pltpu.VMEM((tm, D_MODEL), BF16)],
        name="proj",
        compiler_params=pltpu.CompilerParams(
            dimension_semantics=("parallel", "arbitrary"), vmem_limit_bytes=VMEM_LIMIT),
    )(x, g, w_perm, c128, s128, c64, s64, gk, bk)


def _rope_tables(pos):
    posf = pos.astype(F32)[:, None]

    def tables(dh):
        rot = dh // 4
        half = rot // 2
        inv = jnp.power(jnp.float32(ROPE_THETA), -jnp.arange(half, dtype=F32) * (2.0 / rot))
        ang = posf * inv[None, :]
        cos, sin = jnp.cos(ang), jnp.sin(ang)
        ones = jnp.ones((pos.shape[0], dh - rot), F32)
        c = jnp.concatenate([cos, cos, ones], axis=1)
        s = jnp.concatenate([-sin, sin, 0.0 * ones], axis=1)
        reps = LANES // dh
        return jnp.tile(c, (1, reps)), jnp.tile(s, (1, reps))

    c128, s128 = tables(HEAD_DIM)
    c64, s64 = tables(IDX_DIM)
    return c128, s128, c64, s64


def _permute_w_in(w_in):
    sizes = (A_Q_COLS, A_KV_COLS, A_KV_COLS, I_Q_COLS, IDX_HEADS, IDX_DIM, B_Q_COLS, B_KV_COLS, B_KV_COLS)
    offs = [0]
    for s in sizes:
        offs.append(offs[-1] + s)
    qa, ka, va, qi, wi, ki, qb, kb, vb = [w_in[:, offs[n]:offs[n + 1]] for n in range(len(sizes))]
    pad = jnp.zeros((w_in.shape[0], PROJ_TN - IDX_DIM - IDX_HEADS), w_in.dtype)
    return jnp.concatenate([qa, ka, va, qi, qb, kb, vb, ki, wi, pad], axis=1).astype(BF16)


TOPK_GROUPS = 256


def _sort_key(score):
    score = jnp.where(score == 0.0, 0.0, score)
    bits = pltpu.bitcast(score, I32)
    return bits ^ ((bits >> 31) & 0x7FFFFFFF)


def _fold_lanes(op, x, width):
    parts = [x[:, u * width:(u + 1) * width] for u in range(x.shape[1] // width)]
    while len(parts) > 1:
        nxt = [op(parts[u], parts[u + 1]) for u in range(0, len(parts) - 1, 2)]
        if len(parts) % 2:
            nxt.append(parts[-1])
        parts = nxt
    return parts[0]


def _lane_allreduce(op, x):
    shift = LANES // 2
    while shift >= 1:
        x = op(x, pltpu.roll(x, shift, axis=1))
        shift //= 2
    return x


def _count(keys_ref, nch, cw, pred):
    rows = keys_ref.shape[0]

    def body(c, acc):
        off = pl.multiple_of(c * cw, cw)
        blk = keys_ref[:, pl.ds(off, cw)]
        idx = off + lax.broadcasted_iota(I32, blk.shape, 1)
        return acc + _fold_lanes(jnp.add, jnp.where(pred(blk, idx), 1.0, 0.0), LANES)

    acc = lax.fori_loop(0, nch, body, jnp.zeros((rows, LANES), F32))
    return jnp.sum(acc, axis=1, keepdims=True)


def _row_bounds(keys_ref, nch, cw):
    rows = keys_ref.shape[0]

    def body(c, gmax):
        off = pl.multiple_of(c * cw, cw)
        return jnp.maximum(gmax, _fold_lanes(jnp.maximum, keys_ref[:, pl.ds(off, cw)], TOPK_GROUPS))

    gmax = lax.fori_loop(0, nch, body, jnp.full((rows, TOPK_GROUPS), INT_MIN, I32))
    lo = _lane_allreduce(jnp.minimum, _fold_lanes(jnp.minimum, gmax, LANES))[:, :1]
    top = _lane_allreduce(jnp.maximum, _fold_lanes(jnp.maximum, gmax, LANES))[:, :1]
    return lo, top


def _select_threshold(keys_ref, nch, cw, topk, idx_bits):
    rows = keys_ref.shape[0]
    kf = float(topk)
    lo, top = _row_bounds(keys_ref, nch, cw)
    hi = top + 1

    def n_open(lo, hi):
        return jnp.max(jnp.where(hi - 1 > lo, 1.0, 0.0))

    def cond(st):
        return (st[2] > 0.5) & (st[3] < 40)

    def body(st):
        lo, hi, _, it = st
        mid = (lo & hi) + ((lo ^ hi) >> 1)
        cnt = _count(keys_ref, nch, cw, lambda k, i: k >= mid)
        ge = cnt >= kf
        lo = jnp.where(ge, mid, lo)
        hi = jnp.where(cnt == kf, mid + 1, jnp.where(ge, hi, mid))
        return lo, hi, n_open(lo, hi), it + 1

    tau, _, _, _ = lax.while_loop(cond, body, (lo, hi, n_open(lo, hi), jnp.int32(0)))
    n_ge = _count(keys_ref, nch, cw, lambda k, i: k >= tau)
    has_k = tau > INT_MIN
    tied = has_k & (n_ge > kf)
    any_tied = jnp.max(jnp.where(tied, 1.0, 0.0)) > 0.5

    def resolve(_):
        need = kf - _count(keys_ref, nch, cw, lambda k, i: k > tau)
        lim = jnp.zeros((rows, 1), I32)
        for b in range(idx_bits - 1, -1, -1):
            cand = lim + (1 << b)
            cnt = _count(keys_ref, nch, cw, lambda k, i: (k == tau) & (i < cand))
            lim = jnp.where(cnt < need, cand, lim)
        return lim

    lim = lax.cond(any_tied, resolve, lambda _: jnp.zeros((rows, 1), I32), 0)
    jlim = jnp.where(tied, lim, jnp.where(has_k, jnp.int32(2 ** 31 - 1), jnp.int32(-1)))
    return tau, jlim


def _keys_to_bias(keys_ref, nch, cw, tau, jlim):
    def body(c, carry):
        off = pl.multiple_of(c * cw, cw)
        key = keys_ref[:, pl.ds(off, cw)]
        idx = off + lax.broadcasted_iota(I32, key.shape, 1)
        sel = (key > tau) | ((key == tau) & (idx <= jlim))
        keys_ref[:, pl.ds(off, cw)] = pltpu.bitcast(jnp.where(sel, 0.0, NEG), I32)
        return carry

    lax.fori_loop(0, nch, body, 0)


def _flash_update(sl, d, v, m_scr, acc_scr):
    reps = d.shape[1] // LANES
    m_old = m_scr[sl]
    m_new = jnp.maximum(m_old, jnp.max(d, axis=1, keepdims=True))
    alpha = jnp.exp2(m_old - m_new)
    p = jnp.exp2(d - jnp.tile(m_new, (1, reps)))
    v_ones = jnp.concatenate([v, jnp.ones(v.shape, BF16)], axis=1)
    acc_scr[sl] = jnp.tile(alpha, (1, 2)) * acc_scr[sl] + jnp.dot(p.astype(BF16), v_ones,
                                                                   preferred_element_type=F32)
    m_scr[sl] = m_new


def _flash_init(m_scr, acc_scr):
    m_scr[...] = jnp.full(m_scr.shape, NEG, F32)
    acc_scr[...] = jnp.zeros(acc_scr.shape, F32)


def _flash_result(acc, guard=False):
    l = acc[:, HEAD_DIM:]
    return acc[:, :HEAD_DIM] / (jnp.maximum(l, 1e-30) if guard else l)


def _causal_steps(n_q, chunks_of):
    qb, cb = [], []
    for i in range(n_q):
        for c in range(chunks_of(i)):
            qb.append(i)
            cb.append(c)
    return jnp.asarray(qb, I32), jnp.asarray(cb, I32)


IDX_TN = 256


def _dsa_prompt_kernel(qb_ref, cb_ref, qi_ref, misc_ref, kit_ref, qa_ref, k_ref, v_ref, o_ref,
                       keys_scr, qih_scr, wib_scr, d_scr, m_scr, acc_scr,
                       *, tq, tk, topk, idx_bits):
    step = pl.program_id(0)
    i = qb_ref[step]
    c = cb_ref[step]
    t0 = i * tq
    nch = (t0 + tq + tk - 1) // tk
    rt = 64
    n_chunks = keys_scr.shape[1] // tk
    scan = max(w for w in (1, 2) if n_chunks % w == 0)

    @pl.when(c == 0)
    def _():
        for hh in range(IDX_HEADS):
            qih_scr[hh] = qi_ref[:, hh * IDX_DIM:(hh + 1) * IDX_DIM]
            wib_scr[hh] = jnp.broadcast_to(misc_ref[:, MISC_WI + hh:MISC_WI + hh + 1], (tq, LANES))
        qpos = t0 + lax.broadcasted_iota(I32, (tq, 1), 0)

        def sub_body(kc, carry):
            off = pl.multiple_of(kc * IDX_TN, IDX_TN)
            d_scr[...] = jnp.dot(qih_scr[...].reshape(IDX_HEADS * tq, IDX_DIM),
                                 kit_ref[:, pl.ds(off, IDX_TN)], preferred_element_type=F32)
            kpos = off + lax.broadcasted_iota(I32, (rt, IDX_TN), 1)
            for r in range(tq // rt):
                score = jnp.zeros((rt, IDX_TN), F32)
                for hh in range(IDX_HEADS):
                    w = wib_scr[hh, r * rt:(r + 1) * rt, :]
                    dd = d_scr[hh * tq + r * rt:hh * tq + (r + 1) * rt, :]
                    score = score + jnp.tile(w, (1, IDX_TN // LANES)) * jnp.maximum(dd, 0.0)
                keys_scr[r * rt:(r + 1) * rt, pl.ds(off, IDX_TN)] = jnp.where(
                    kpos <= qpos[r * rt:(r + 1) * rt], _sort_key(score), INT_MIN)
            return carry

        lax.fori_loop(0, nch * (tk // IDX_TN), sub_body, 0)

        n_scan = (nch + scan - 1) // scan

        def pad_body(kc, carry):
            keys_scr[:, pl.ds(pl.multiple_of(kc * tk, tk), tk)] = jnp.full((tq, tk), INT_MIN, I32)
            return carry

        lax.fori_loop(nch, n_scan * scan, pad_body, 0)
        tau, jl = _select_threshold(keys_scr, n_scan, scan * tk, topk, idx_bits)
        _keys_to_bias(keys_scr, n_scan, scan * tk, tau, jl)
        _flash_init(m_scr, acc_scr)

    bias = pltpu.bitcast(keys_scr[:, pl.ds(pl.multiple_of(c * tk, tk), tk)], F32)
    group = A_HEADS // A_KV_HEADS

    def scores(h):
        kv = h // group
        return _nt_dot(qa_ref[:, h * HEAD_DIM:(h + 1) * HEAD_DIM],
                       k_ref[:, kv * HEAD_DIM:(kv + 1) * HEAD_DIM]) + bias

    d_next = scores(0)
    for h in range(A_HEADS):
        d = d_next
        if h + 1 < A_HEADS:
            d_next = scores(h + 1)
        kv = h // group
        _flash_update(h, d, v_ref[:, kv * HEAD_DIM:(kv + 1) * HEAD_DIM], m_scr, acc_scr)

    @pl.when(c == nch - 1)
    def _():
        for h in range(A_HEADS):
            o_ref[:, h * HEAD_DIM:(h + 1) * HEAD_DIM] = _flash_result(acc_scr[h]).astype(o_ref.dtype)


def _dsa_prompt(slab, misc, kit, tq, tk):
    rows = slab.shape[0]
    topk = min(DSA_TOPK, rows // 4)
    qb, cb = _causal_steps(rows // tq, lambda i: (i * tq + tq + tk - 1) // tk)
    kern = functools.partial(_dsa_prompt_kernel, tq=tq, tk=tk, topk=topk,
                             idx_bits=max(1, (rows - 1).bit_length()))
    grid_spec = pltpu.PrefetchScalarGridSpec(
        num_scalar_prefetch=2,
        grid=(int(qb.shape[0]),),
        in_specs=[pl.BlockSpec((tq, I_Q_COLS), lambda s, qb, cb: (qb[s], COL_QI // I_Q_COLS)),
                  pl.BlockSpec((tq, LANES), lambda s, qb, cb: (qb[s], 0)),
                  pl.BlockSpec((IDX_DIM, rows), lambda s, qb, cb: (0, 0)),
                  pl.BlockSpec((tq, A_Q_COLS), lambda s, qb, cb: (qb[s], COL_QA // A_Q_COLS)),
                  pl.BlockSpec((tk, A_KV_COLS), lambda s, qb, cb: (cb[s], COL_KA // A_KV_COLS)),
                  pl.BlockSpec((tk, A_KV_COLS), lambda s, qb, cb: (cb[s], COL_VA // A_KV_COLS))],
        out_specs=pl.BlockSpec((tq, A_Q_COLS), lambda s, qb, cb: (qb[s], 0)),
        scratch_shapes=[pltpu.VMEM((tq, rows), I32),
                        pltpu.VMEM((IDX_HEADS, tq, IDX_DIM), BF16),
                        pltpu.VMEM((IDX_HEADS, tq, LANES), F32),
                        pltpu.VMEM((IDX_HEADS * tq, IDX_TN), F32),
                        pltpu.VMEM((A_HEADS, tq, LANES), F32),
                        pltpu.VMEM((A_HEADS, tq, 2 * HEAD_DIM), F32)])
    return pl.pallas_call(
        kern, grid_spec=grid_spec, name="dsa_prompt",
        out_shape=jax.ShapeDtypeStruct((rows, A_Q_COLS), BF16),
        compiler_params=pltpu.CompilerParams(
            dimension_semantics=("arbitrary",), vmem_limit_bytes=VMEM_LIMIT),
    )(qb, cb, slab, misc, kit, slab, slab, slab)


def _gate_topk(gate, n_valid, ksel):
    lane = lax.broadcasted_iota(I32, gate.shape, 1)
    lanef = lane.astype(F32)
    gate = jnp.where(lane < n_valid, gate, NEG)
    sel = jnp.zeros(gate.shape, F32)
    for _ in range(ksel):
        mx = jnp.max(gate, axis=1, keepdims=True)
        first = jnp.min(jnp.where(gate == mx, lanef, 1e9), axis=1, keepdims=True)
        hit = (lanef == first) & (mx > 0.5 * NEG)
        sel = jnp.where(hit, 1.0, sel)
        gate = jnp.where(hit, NEG, gate)
    return sel


def _block_bias_rows(sel):
    return ((sel - 1.0) * (-NEG)).astype(BF16)


def _block_expansion(n_keys, blk0, keys_on_rows):
    shape = (n_keys, LANES) if keys_on_rows else (LANES, n_keys)
    kdim, bdim = (0, 1) if keys_on_rows else (1, 0)
    kblk = lax.broadcasted_iota(I32, shape, kdim) // MOBA_BLOCK + blk0
    return jnp.where(kblk == lax.broadcasted_iota(I32, shape, bdim), 1.0, 0.0).astype(BF16)


def _block_means_kernel(k_ref, o_ref):
    o_ref[0] = jnp.sum(k_ref[...], axis=0, keepdims=True) * (1.0 / MOBA_BLOCK)


def _block_means(kb):
    nb = kb.shape[0] // MOBA_BLOCK
    out = pl.pallas_call(
        _block_means_kernel,
        grid=(nb,),
        in_specs=[pl.BlockSpec((MOBA_BLOCK, B_KV_COLS), lambda n: (n, 0))],
        out_specs=pl.BlockSpec((1, 1, B_KV_COLS), lambda n: (n, 0, 0)),
        out_shape=jax.ShapeDtypeStruct((nb, 1, B_KV_COLS), F32),
        name="block_means",
    )(kb)
    return out.reshape(nb, B_KV_COLS)


def _moba_prompt_kernel(qb_ref, cb_ref, q_ref, means_ref, k_ref, v_ref, o_ref,
                        qaug_scr, m_scr, acc_scr, *, tq, tk, ksel):
    step = pl.program_id(0)
    i = qb_ref[step]
    c = cb_ref[step]
    t0 = i * tq
    qblk = t0 // MOBA_BLOCK
    bpc = tk // MOBA_BLOCK
    nch = qblk // bpc + 1
    group = B_HEADS // B_KV_HEADS

    @pl.when(c == 0)
    def _():
        lane = lax.broadcasted_iota(I32, (tq, LANES), 1)
        for h in range(B_HEADS):
            kv = h // group
            q = q_ref[:, h * HEAD_DIM:(h + 1) * HEAD_DIM]
            gate = _nt_dot(q, means_ref[:, kv * HEAD_DIM:(kv + 1) * HEAD_DIM])
            sel = jnp.where(lane == qblk, 1.0, _gate_topk(gate, qblk, ksel))
            qaug_scr[h] = jnp.concatenate([q, _block_bias_rows(sel)], axis=1)
        _flash_init(m_scr, acc_scr)

    e_t = _block_expansion(tk, c * bpc, keys_on_rows=True)
    qlim = t0 + lax.broadcasted_iota(I32, (tq, 1), 0) + jnp.where(c == nch - 1, 0, 2 ** 30)
    kpos = c * tk + lax.broadcasted_iota(I32, (tq, tk), 1)
    causal = jnp.where(kpos <= qlim, 0.0, NEG)
    for kv in range(B_KV_HEADS):
        rhs = jnp.concatenate([k_ref[:, kv * HEAD_DIM:(kv + 1) * HEAD_DIM], e_t], axis=1)
        vv = v_ref[:, kv * HEAD_DIM:(kv + 1) * HEAD_DIM]
        for g in range(group):
            h = kv * group + g
            _flash_update(h, _nt_dot(qaug_scr[h], rhs) + causal, vv, m_scr, acc_scr)

    @pl.when(c == nch - 1)
    def _():
        for h in range(B_HEADS):
            o_ref[:, h * HEAD_DIM:(h + 1) * HEAD_DIM] = _flash_result(acc_scr[h]).astype(o_ref.dtype)


def _moba_prompt(slab, means_pad, tq, tk):
    rows = slab.shape[0]
    ksel = min(MOBA_TOPK, (rows - 1) // MOBA_BLOCK)
    assert tq == MOBA_BLOCK and tk % MOBA_BLOCK == 0 and means_pad.shape[0] == LANES
    qb, cb = _causal_steps(rows // tq, lambda i: (i * tq) // tk + 1)
    kern = functools.partial(_moba_prompt_kernel, tq=tq, tk=tk, ksel=ksel)
    grid_spec = pltpu.PrefetchScalarGridSpec(
        num_scalar_prefetch=2,
        grid=(int(qb.shape[0]),),
        in_specs=[pl.BlockSpec((tq, B_Q_COLS), lambda s, qb, cb: (qb[s], COL_QB // B_Q_COLS)),
                  pl.BlockSpec((LANES, B_KV_COLS), lambda s, qb, cb: (0, 0)),
                  pl.BlockSpec((tk, B_KV_COLS), lambda s, qb, cb: (cb[s], COL_KB // B_KV_COLS)),
                  pl.BlockSpec((tk, B_KV_COLS), lambda s, qb, cb: (cb[s], COL_VB // B_KV_COLS))],
        out_specs=pl.BlockSpec((tq, B_Q_COLS), lambda s, qb, cb: (qb[s], 0)),
        scratch_shapes=[pltpu.VMEM((B_HEADS, tq, 2 * HEAD_DIM), BF16),
                        pltpu.VMEM((B_HEADS, tq, LANES), F32),
                        pltpu.VMEM((B_HEADS, tq, 2 * HEAD_DIM), F32)])
    return pl.pallas_call(
        kern, grid_spec=grid_spec, name="moba_prompt",
        out_shape=jax.ShapeDtypeStruct((rows, B_Q_COLS), BF16),
        compiler_params=pltpu.CompilerParams(
            dimension_semantics=("arbitrary",), vmem_limit_bytes=VMEM_LIMIT),
    )(qb, cb, slab, means_pad, slab, slab)


TPAD = SUBLANES
SROWS = 8 * TPAD
KV_ROWS = SROWS // 4
PAGE_ROWS = PAGE_SIZE * 4


def _page_head(ref, kv):
    return ref[0, pl.ds(kv, PAGE_SIZE, stride=4), :]


def _paged_spec(shape, n_pages, per_step, first_step, n_steps, j):
    def index(b, s, pt):
        local = jnp.clip(s - first_step, 0, n_steps - 1)
        return (pt[b * n_pages + local * per_step + j], 0, 0)
    return pl.BlockSpec(shape, index)


def _dsa_sample_kernel(pt_ref, qi_ref, wi_ref, *rest, n_pages, pa, pb, t_new, topk, idx_bits, cw):
    kidx_refs, rest = rest[:pa], rest[pa:]
    kin_ref, qa_ref = rest[:2]
    k_refs, v_refs = rest[2:2 + pb], rest[2 + pb:2 + 2 * pb]
    kn_ref, vn_ref, o_ref, keys_scr, wib_scr, m_scr, acc_scr = rest[2 + 2 * pb:]
    s = pl.program_id(1)
    na, nb = n_pages // pa, n_pages // pb
    past = n_pages * PAGE_SIZE
    ntot = keys_scr.shape[1]
    trow = lax.broadcasted_iota(I32, (TPAD, 1), 0)

    def index_keys(kd, off):
        width = kd.shape[1]
        x = jnp.tile(wib_scr[...], (1, width // LANES)) * jnp.maximum(
            jnp.dot(qi_ref[0], kd, preferred_element_type=F32), 0.0)
        score = x[:TPAD]
        for hh in range(1, IDX_HEADS):
            score = score + x[hh * TPAD:(hh + 1) * TPAD]
        kpos = off + lax.broadcasted_iota(I32, (TPAD, width), 1)
        visible = (kpos <= past + trow) & (trow < t_new)
        keys_scr[:, pl.ds(off, width)] = jnp.where(visible, _sort_key(score), INT_MIN)

    @pl.when(s == 0)
    def _():
        wib_scr[...] = jnp.broadcast_to(wi_ref[0], wib_scr.shape)
        keys_scr[:, past:] = jnp.full((TPAD, ntot - past), INT_MIN, I32)

    @pl.when(s < na)
    def _():
        kd = jnp.concatenate([r[0].astype(BF16) for r in kidx_refs], axis=1)
        index_keys(kd, pl.multiple_of(s * (pa * PAGE_SIZE), pa * PAGE_SIZE))

    @pl.when(s == na - 1)
    def _():
        index_keys(kin_ref[0], past)
        tau, jl = _select_threshold(keys_scr, ntot // cw, cw, topk, idx_bits)
        _keys_to_bias(keys_scr, ntot // cw, cw, tau, jl)
        _flash_init(m_scr, acc_scr)

    def attend(key_of, val_of, off, width):
        bias = pltpu.bitcast(keys_scr[:, pl.ds(off, width)], F32)
        bias = jnp.concatenate([bias] * (KV_ROWS // TPAD), axis=0)
        for kv in range(A_KV_HEADS):
            rows = slice(kv * KV_ROWS, (kv + 1) * KV_ROWS)
            _flash_update(rows, _nt_dot(qa_ref[0, rows], key_of(kv)) + bias, val_of(kv), m_scr, acc_scr)

    def gather(refs, kv):
        return jnp.concatenate([_page_head(r, kv) for r in refs], axis=0).astype(BF16)

    @pl.when(s >= na)
    def _():
        off = pl.multiple_of((s - na) * (pb * PAGE_SIZE), pb * PAGE_SIZE)
        attend(lambda kv: gather(k_refs, kv), lambda kv: gather(v_refs, kv), off, pb * PAGE_SIZE)

    @pl.when(s == na + nb - 1)
    def _():
        attend(lambda kv: kn_ref[0, kv], lambda kv: vn_ref[0, kv], past, PAGE_SIZE)
        o_ref[0] = _flash_result(acc_scr[...], guard=True)


def _dsa_sample(pt_flat, n_pages, t_new, qi_s, wi_s, kidx_pages, ki_new_t, qa_s, k_pages, v_pages, ka_new, va_new,
                pa, pb):
    ns = qi_s.shape[0]
    past = n_pages * PAGE_SIZE
    topk = min(DSA_TOPK, (past + t_new) // 4)
    ntot = -(-(past + PAGE_SIZE) // TOPK_GROUPS) * TOPK_GROUPS
    cw = ntot
    na, nb = n_pages // pa, n_pages // pb
    seq3 = lambda s1, s2: pl.BlockSpec((1, s1, s2), lambda b, s, pt: (b, 0, 0))
    seq4 = pl.BlockSpec((1, A_KV_HEADS, PAGE_SIZE, HEAD_DIM), lambda b, s, pt: (b, 0, 0, 0))
    kern = functools.partial(_dsa_sample_kernel, n_pages=n_pages, pa=pa, pb=pb, t_new=t_new, topk=topk,
                             idx_bits=max(1, (ntot - 1).bit_length()), cw=cw)
    page_kv = lambda j: _paged_spec((1, PAGE_ROWS, HEAD_DIM), n_pages, pb, na, nb, j)
    grid_spec = pltpu.PrefetchScalarGridSpec(
        num_scalar_prefetch=1,
        grid=(ns, na + nb),
        in_specs=[seq3(IDX_HEADS * TPAD, IDX_DIM), seq3(IDX_HEADS * TPAD, 1)]
                 + [_paged_spec((1, IDX_DIM, PAGE_SIZE), n_pages, pa, 0, na, j) for j in range(pa)]
                 + [seq3(IDX_DIM, PAGE_SIZE), seq3(SROWS, HEAD_DIM)]
                 + [page_kv(j) for j in range(pb)] + [page_kv(j) for j in range(pb)]
                 + [seq4, seq4],
        out_specs=seq3(SROWS, HEAD_DIM),
        scratch_shapes=[pltpu.VMEM((TPAD, ntot), I32),
                        pltpu.VMEM((IDX_HEADS * TPAD, LANES), F32),
                        pltpu.VMEM((SROWS, LANES), F32),
                        pltpu.VMEM((SROWS, 2 * HEAD_DIM), F32)])
    return pl.pallas_call(
        kern, grid_spec=grid_spec, name="dsa_sample",
        out_shape=jax.ShapeDtypeStruct((ns, SROWS, HEAD_DIM), F32),
        compiler_params=pltpu.CompilerParams(
            dimension_semantics=("arbitrary", "arbitrary"), vmem_limit_bytes=VMEM_LIMIT),
    )(pt_flat, qi_s, wi_s, *([kidx_pages] * pa), ki_new_t, qa_s, *([k_pages] * pb), *([v_pages] * pb),
      ka_new, va_new)


def _moba_sample_kernel(pt_ref, q_ref, *rest, n_pages, pb, t_new, ksel):
    k_refs, v_refs = rest[:pb], rest[pb:2 * pb]
    kn_ref, vn_ref, o_ref, bsum_scr, s_scr, seln_scr, m_scr, acc_scr = rest[2 * pb:]
    s = pl.program_id(1)
    nst = n_pages // pb
    past = n_pages * PAGE_SIZE
    nbp = past // MOBA_BLOCK
    bps = pb * PAGE_SIZE // MOBA_BLOCK
    width = pb * PAGE_SIZE
    tok = lax.broadcasted_iota(I32, (KV_ROWS, 1), 0) % TPAD
    pad_bias = jnp.where(tok < t_new, 0.0, NEG)
    kv_rows = lambda kv: slice(kv * KV_ROWS, (kv + 1) * KV_ROWS)

    @pl.when(s == 0)
    def _():
        bsum_scr[...] = jnp.zeros(bsum_scr.shape, F32)

    @pl.when(s < nst)
    def _():
        off = pl.multiple_of(s * width, width)
        for kv in range(B_KV_HEADS):
            kf = jnp.concatenate([_page_head(r, kv) for r in k_refs], axis=0)
            bsum_scr[kv, pl.ds(pl.multiple_of(s * bps, bps), bps), :] = jnp.sum(
                kf.reshape(bps, MOBA_BLOCK, HEAD_DIM), axis=1)
            s_scr[kv_rows(kv), pl.ds(off, width)] = _nt_dot(q_ref[0, kv_rows(kv)], kf.astype(BF16))

    @pl.when(s == nst - 1)
    def _():
        for kv in range(B_KV_HEADS):
            q = q_ref[0, kv_rows(kv)]
            s_scr[kv_rows(kv), pl.ds(past, PAGE_SIZE)] = _nt_dot(q, kn_ref[0, kv])
            means = (bsum_scr[kv] * (1.0 / MOBA_BLOCK)).astype(BF16)
            seln_scr[kv_rows(kv)] = _block_bias_rows(_gate_topk(_nt_dot(q, means), nbp, ksel))
        _flash_init(m_scr, acc_scr)

    @pl.when(s >= nst)
    def _():
        sb = s - nst
        off = pl.multiple_of(sb * width, width)
        expand = _block_expansion(width, sb * bps, keys_on_rows=False)
        for kv in range(B_KV_HEADS):
            bias = jnp.dot(seln_scr[kv_rows(kv)], expand, preferred_element_type=F32) + pad_bias
            vv = jnp.concatenate([_page_head(r, kv) for r in v_refs], axis=0).astype(BF16)
            _flash_update(kv_rows(kv), s_scr[kv_rows(kv), pl.ds(off, width)] + bias, vv, m_scr, acc_scr)

    @pl.when(s == 2 * nst - 1)
    def _():
        kpos = past + lax.broadcasted_iota(I32, (KV_ROWS, PAGE_SIZE), 1)
        bias = jnp.where(kpos <= past + tok, 0.0, NEG) + pad_bias
        for kv in range(B_KV_HEADS):
            _flash_update(kv_rows(kv), s_scr[kv_rows(kv), pl.ds(past, PAGE_SIZE)] + bias, vn_ref[0, kv],
                          m_scr, acc_scr)
        o_ref[0] = _flash_result(acc_scr[...], guard=True)


def _moba_sample(pt_flat, n_pages, t_new, qb_s, k_pages, v_pages, kb_new, vb_new, pb):
    ns = qb_s.shape[0]
    past = n_pages * PAGE_SIZE
    nbp = past // MOBA_BLOCK
    assert nbp <= LANES and (pb * PAGE_SIZE // MOBA_BLOCK) % SUBLANES == 0
    ksel = min(MOBA_TOPK, nbp)
    nst = n_pages // pb
    seq3 = lambda s1, s2: pl.BlockSpec((1, s1, s2), lambda b, s, pt: (b, 0, 0))
    seq4 = pl.BlockSpec((1, B_KV_HEADS, PAGE_SIZE, HEAD_DIM), lambda b, s, pt: (b, 0, 0, 0))
    kern = functools.partial(_moba_sample_kernel, n_pages=n_pages, pb=pb, t_new=t_new, ksel=ksel)
    grid_spec = pltpu.PrefetchScalarGridSpec(
        num_scalar_prefetch=1,
        grid=(ns, 2 * nst),
        in_specs=[seq3(SROWS, HEAD_DIM)]
                 + [_paged_spec((1, PAGE_ROWS, HEAD_DIM), n_pages, pb, 0, nst, j) for j in range(pb)]
                 + [_paged_spec((1, PAGE_ROWS, HEAD_DIM), n_pages, pb, nst, nst, j) for j in range(pb)]
                 + [seq4, seq4],
        out_specs=seq3(SROWS, HEAD_DIM),
        scratch_shapes=[pltpu.VMEM((B_KV_HEADS, LANES, HEAD_DIM), F32),
                        pltpu.VMEM((SROWS, past + PAGE_SIZE), F32),
                        pltpu.VMEM((SROWS, LANES), BF16),
                        pltpu.VMEM((SROWS, LANES), F32),
                        pltpu.VMEM((SROWS, 2 * HEAD_DIM), F32)])
    return pl.pallas_call(
        kern, grid_spec=grid_spec, name="moba_sample",
        out_shape=jax.ShapeDtypeStruct((ns, SROWS, HEAD_DIM), F32),
        compiler_params=pltpu.CompilerParams(
            dimension_semantics=("arbitrary", "arbitrary"), vmem_limit_bytes=VMEM_LIMIT),
    )(pt_flat, qb_s, *([k_pages] * pb), *([v_pages] * pb), kb_new, vb_new)


def _memory_kv_kernel(mem_ref, g_ref, w_ref, k_ref, v_ref):
    h = _rms(mem_ref[...], g_ref[...]).astype(BF16)
    kv = jnp.dot(h, w_ref[...], preferred_element_type=F32)
    k_ref[...] = kv[:, :MEM_WIDTH]
    v_ref[...] = kv[:, MEM_WIDTH:]


def _memory_kv(mem, g, w_bf):
    m = mem.shape[0]
    return pl.pallas_call(
        _memory_kv_kernel,
        out_shape=[jax.ShapeDtypeStruct((m, MEM_WIDTH), F32), jax.ShapeDtypeStruct((m, MEM_WIDTH), F32)],
        name="memory_kv",
        compiler_params=pltpu.CompilerParams(vmem_limit_bytes=VMEM_LIMIT),
    )(mem, g, w_bf)


def _outproj_cross_kernel(x_ref, oa_ref, ob_ref, woa_ref, wob_ref, g_ref, wq_ref, mk_ref, mv_ref, wo_ref,
                          y_ref, *, rows_per_seq):
    x1 = x_ref[...] + jnp.dot(oa_ref[...], woa_ref[...], preferred_element_type=F32) \
        + jnp.dot(ob_ref[...], wob_ref[...], preferred_element_type=F32)
    hc = _rms(x1, g_ref[...]).astype(BF16)
    q = jnp.dot(hc, wq_ref[...], preferred_element_type=F32).astype(BF16)
    tm = q.shape[0]
    nk = mk_ref.shape[0]
    scale = MEM_HEAD_DIM ** -0.5
    if rows_per_seq is not None:
        rseq = (pl.program_id(0) * tm + lax.broadcasted_iota(I32, (tm, 1), 0)) // rows_per_seq
        kseq = lax.broadcasted_iota(I32, (tm, nk), 1) // MEM_TOKENS
        mask = kseq == rseq
    outs = []
    for h in range(MEM_HEADS):
        sl = slice(h * MEM_HEAD_DIM, (h + 1) * MEM_HEAD_DIM)
        s = _nt_dot(q[:, sl], mk_ref[:, sl]) * scale
        if rows_per_seq is not None:
            s = jnp.where(mask, s, NEG)
        m = jnp.max(s, axis=1, keepdims=True)
        e = jnp.exp(s - m)
        p = e / jnp.sum(e, axis=1, keepdims=True)
        outs.append(jnp.dot(p.astype(BF16), mv_ref[:, sl], preferred_element_type=F32))
    o = jnp.concatenate(outs, axis=1).astype(BF16)
    y_ref[...] = x1 + jnp.dot(o, wo_ref[...], preferred_element_type=F32)


def _outproj_cross(x, oa, ob, w_out_bf, g_cross, w_mq_bf, mk_bf, mv_bf, w_mo_bf, tm, rows_per_seq):
    rows = x.shape[0]
    nk = mk_bf.shape[0]
    row_spec = lambda w: pl.BlockSpec((tm, w), lambda i: (i, 0))
    full = lambda a, b: pl.BlockSpec((a, b), lambda i: (0, 0))
    kern = functools.partial(_outproj_cross_kernel, rows_per_seq=rows_per_seq)
    return pl.pallas_call(
        kern,
        grid=(rows // tm,),
        in_specs=[row_spec(D_MODEL), row_spec(A_Q_COLS), row_spec(B_Q_COLS),
                  pl.BlockSpec((A_Q_COLS, D_MODEL), lambda i: (0, 0)),
                  pl.BlockSpec((B_Q_COLS, D_MODEL), lambda i: (1, 0)),
                  full(1, D_MODEL), full(D_MODEL, MEM_WIDTH), full(nk, MEM_WIDTH), full(nk, MEM_WIDTH),
                  full(MEM_WIDTH, D_MODEL)],
        out_specs=row_spec(D_MODEL),
        out_shape=jax.ShapeDtypeStruct((rows, D_MODEL), F32),
        name="outproj_cross",
        compiler_params=pltpu.CompilerParams(
            dimension_semantics=("parallel",), vmem_limit_bytes=VMEM_LIMIT),
    )(x, oa, ob, w_out_bf, w_out_bf, g_cross, w_mq_bf, mk_bf, mv_bf, w_mo_bf)


def _mlp_final_kernel(x_ref, g_ref, wu_ref, wd_ref, gf_ref, y_ref, h_scr, acc_scr):
    f = pl.program_id(1)

    @pl.when(f == 0)
    def _():
        h_scr[...] = _rms(x_ref[...], g_ref[...]).astype(BF16)
        acc_scr[...] = jnp.zeros(acc_scr.shape, F32)

    u = jnp.maximum(jnp.dot(h_scr[...], wu_ref[...], preferred_element_type=F32), 0.0)
    acc_scr[...] += jnp.dot((u * u).astype(BF16), wd_ref[...], preferred_element_type=F32)

    @pl.when(f == pl.num_programs(1) - 1)
    def _():
        y_ref[...] = _rms(x_ref[...] + acc_scr[...], gf_ref[...])


def _mlp_final(x, g_ffn, w_up_bf, w_down_bf, g_final, tm, tf):
    rows = x.shape[0]
    d_ff = w_up_bf.shape[1]
    return pl.pallas_call(
        _mlp_final_kernel,
        grid=(rows // tm, d_ff // tf),
        in_specs=[pl.BlockSpec((tm, D_MODEL), lambda i, f: (i, 0)),
                  pl.BlockSpec((1, D_MODEL), lambda i, f: (0, 0)),
                  pl.BlockSpec((D_MODEL, tf), lambda i, f: (0, f)),
                  pl.BlockSpec((tf, D_MODEL), lambda i, f: (f, 0)),
                  pl.BlockSpec((1, D_MODEL), lambda i, f: (0, 0))],
        out_specs=pl.BlockSpec((tm, D_MODEL), lambda i, f: (i, 0)),
        out_shape=jax.ShapeDtypeStruct((rows, D_MODEL), F32),
        scratch_shapes=[pltpu.VMEM((tm, D_MODEL), BF16), pltpu.VMEM((tm, D_MODEL), F32)],
        name="mlp_final",
        compiler_params=pltpu.CompilerParams(
            dimension_semantics=("parallel", "arbitrary"), vmem_limit_bytes=VMEM_LIMIT),
    )(x, g_ffn, w_up_bf, w_down_bf, g_final)


def _row_tile(rows, want):
    t = min(rows, want)
    while rows % t:
        t //= 2
    return t


def _pad_rows(a, axis, size):
    pad = [(0, 0)] * a.ndim
    pad[axis] = (0, size - a.shape[axis])
    return jnp.pad(a, pad)


def _heads_first(a, ns, t_new, heads, dim, t_pad):
    a = a.reshape(ns, t_new, heads, dim).transpose(0, 2, 1, 3)
    return _pad_rows(a, 2, t_pad)


def kernel(x_prompt, x_sample, cache_k_a, cache_v_a, cache_kidx, cache_k_b, cache_v_b, cache_mem_k,
           cache_mem_v, page_table, mem_prompt, g_mix, w_in, g_kidx, b_kidx, w_out, g_cross, w_mq, g_mem,
           w_mkv, w_mo, g_ffn, w_up, w_down, g_final):
    batch, s_len, _ = x_prompt.shape
    ns, t_new, _ = x_sample.shape
    depth = w_in.shape[0]
    n_pages = page_table.shape[1]
    past = n_pages * PAGE_SIZE
    n_phys = cache_k_a.shape[1]
    assert batch == 1 and depth == 1
    assert s_len % 1024 == 0 and s_len // MOBA_BLOCK <= LANES
    assert past % MOBA_BLOCK == 0 and t_new <= TPAD
    l = 0
    row = lambda v: v.reshape(1, -1)

    w_in_p = _permute_w_in(w_in[l])
    gk = row(jnp.concatenate([g_kidx[l], jnp.zeros((LANES - IDX_DIM,), F32)]))
    bk = row(jnp.concatenate([b_kidx[l], jnp.zeros((LANES - IDX_DIM,), F32)]))
    w_out_bf = w_out[l].astype(BF16)
    w_mq_bf = w_mq[l].astype(BF16)
    w_mo_bf = w_mo[l].astype(BF16)
    w_up_bf = w_up[l].astype(BF16)
    w_down_bf = w_down[l].astype(BF16)

    xp = x_prompt.reshape(s_len, D_MODEL)
    tabs_p = _rope_tables(jnp.arange(s_len))
    slab, ka, va, kb, vb, misc, misc_bf = _proj(xp, row(g_mix[l]), w_in_p, tabs_p, gk, bk, _row_tile(s_len, 512))
    kit = misc_bf[:, :IDX_DIM].T
    oa = _dsa_prompt(slab, misc, kit, tq=256, tk=1024)
    means_pad = _pad_rows(_block_means(kb).astype(BF16), 0, LANES)
    ob = _moba_prompt(slab, means_pad, tq=MOBA_BLOCK, tk=1024)
    mk, mv = _memory_kv(mem_prompt.reshape(MEM_TOKENS, D_MODEL), row(g_mem[l]), w_mkv[l].astype(BF16))
    xp2 = _outproj_cross(xp, oa, ob, w_out_bf, row(g_cross[l]), w_mq_bf, mk.astype(BF16), mv.astype(BF16),
                         w_mo_bf, _row_tile(s_len, 256), None)
    y_prompt = _mlp_final(xp2, row(g_ffn[l]), w_up_bf, w_down_bf, row(g_final), _row_tile(s_len, 512), 1024)

    rows_s = ns * t_new
    xs = x_sample.reshape(rows_s, D_MODEL)
    pos_s = jnp.tile(past + jnp.arange(t_new), ns)
    tabs_s = _rope_tables(pos_s)
    rows_pad = -(-rows_s // SUBLANES) * SUBLANES
    xs_pad = _pad_rows(xs, 0, rows_pad)
    tabs_s = tuple(_pad_rows(t, 0, rows_pad) for t in tabs_s)
    slab_s, ka_s, va_s, kb_s, vb_s, misc_s, _ = _proj(xs_pad, row(g_mix[l]), w_in_p, tabs_s, gk, bk,
                                                      _row_tile(rows_pad, 128))
    slab_s, ka_s, va_s, kb_s, vb_s, misc_s = [a[:rows_s] for a in (slab_s, ka_s, va_s, kb_s, vb_s, misc_s)]

    pt_flat = page_table.reshape(-1).astype(I32)
    index_pages, attend_pages = min(32, n_pages), min(16, n_pages)
    assert n_pages % index_pages == 0 and n_pages % attend_pages == 0
    page_view = lambda c: c[l].reshape(n_phys, PAGE_ROWS, HEAD_DIM)
    kidx_view = jnp.swapaxes(cache_kidx[l], 1, 2)
    cols = lambda c0, w: slab_s[:, c0:c0 + w]
    qi_s = _heads_first(cols(COL_QI, I_Q_COLS), ns, t_new, IDX_HEADS, IDX_DIM, TPAD)
    qi_s = qi_s.reshape(ns, IDX_HEADS * TPAD, IDX_DIM)
    wi_s = _heads_first(misc_s[:, MISC_WI:MISC_WI + IDX_HEADS], ns, t_new, IDX_HEADS, 1, TPAD)
    wi_s = wi_s.reshape(ns, IDX_HEADS * TPAD, 1)
    ki_new_t = _pad_rows(cols(COL_MISC, IDX_DIM).reshape(ns, t_new, IDX_DIM).transpose(0, 2, 1), 2, PAGE_SIZE)
    q_rows = lambda c0: _heads_first(cols(c0, A_Q_COLS), ns, t_new, A_HEADS, HEAD_DIM, TPAD).reshape(
        ns, SROWS, HEAD_DIM)
    new_kv = lambda c0: _heads_first(cols(c0, A_KV_COLS), ns, t_new, A_KV_HEADS, HEAD_DIM, PAGE_SIZE)
    oa_s = _dsa_sample(pt_flat, n_pages, t_new, qi_s, wi_s, kidx_view, ki_new_t, q_rows(COL_QA),
                       page_view(cache_k_a), page_view(cache_v_a), new_kv(COL_KA), new_kv(COL_VA),
                       index_pages, attend_pages)
    ob_s = _moba_sample(pt_flat, n_pages, t_new, q_rows(COL_QB), page_view(cache_k_b), page_view(cache_v_b),
                        new_kv(COL_KB), new_kv(COL_VB), attend_pages)
    tokens_first = lambda o: o.reshape(ns, A_HEADS, TPAD, HEAD_DIM)[:, :, :t_new].transpose(0, 2, 1, 3).reshape(
        rows_s, -1).astype(BF16)
    oa_s2 = _pad_rows(tokens_first(oa_s), 0, rows_pad)
    ob_s2 = _pad_rows(tokens_first(ob_s), 0, rows_pad)
    mk_s = cache_mem_k[l].reshape(ns * MEM_TOKENS, MEM_WIDTH).astype(BF16)
    mv_s = cache_mem_v[l].reshape(ns * MEM_TOKENS, MEM_WIDTH).astype(BF16)
    xs2 = _outproj_cross(xs_pad, oa_s2, ob_s2, w_out_bf, row(g_cross[l]), w_mq_bf, mk_s, mv_s, w_mo_bf,
                         rows_pad, t_new)
    y_sample = _mlp_final(xs2, row(g_ffn[l]), w_up_bf, w_down_bf, row(g_final), rows_pad, 512)[:rows_s]

    kv5 = lambda a, n, t: a.reshape(1, n, t, A_KV_HEADS, HEAD_DIM)
    return (y_prompt.reshape(batch, s_len, D_MODEL), y_sample.reshape(ns, t_new, D_MODEL),
            kv5(ka, batch, s_len), kv5(va, batch, s_len),
            misc[:, :IDX_DIM].reshape(1, batch, s_len, IDX_DIM),
            kv5(kb, batch, s_len), kv5(vb, batch, s_len),
            mk.reshape(1, batch, MEM_TOKENS, MEM_HEADS, MEM_HEAD_DIM),
            mv.reshape(1, batch, MEM_TOKENS, MEM_HEADS, MEM_HEAD_DIM),
            kv5(ka_s, ns, t_new), kv5(va_s, ns, t_new),
            misc_s[:, :IDX_DIM].reshape(1, ns, t_new, IDX_DIM),
            kv5(kb_s, ns, t_new), kv5(vb_s, ns, t_new))
```

```python
import functools

import jax
import jax.numpy as jnp
from jax import lax
from jax.experimental import pallas as pl
from jax.experimental.pallas import tpu as pltpu

F32 = jnp.float32
BF16 = jnp.bfloat16
I32 = jnp.int32

D_MODEL = 2048
HEAD_DIM = 128
A_HEADS = 8
A_KV_HEADS = 4
B_HEADS = 8
B_KV_HEADS = 4
IDX_HEADS = 16
IDX_DIM = 64
DSA_TOPK = 256
MOBA_BLOCK = 256
MOBA_TOPK = 3
MEM_TOKENS = 256
MEM_HEADS = 4
MEM_HEAD_DIM = 128
PAGE_SIZE = 128
ROPE_THETA = 500000.0
NORM_EPS = 1e-6

A_Q_COLS = A_HEADS * HEAD_DIM
A_KV_COLS = A_KV_HEADS * HEAD_DIM
I_Q_COLS = IDX_HEADS * IDX_DIM
B_Q_COLS = B_HEADS * HEAD_DIM
B_KV_COLS = B_KV_HEADS * HEAD_DIM
MEM_WIDTH = MEM_HEADS * MEM_HEAD_DIM

LANES = 128
SUBLANES = 8
VMEM_LIMIT = 56 * 1024 * 1024
INT_MIN = -(2 ** 31)
NEG = -1e30
LOG2E = 1.4426950408889634
SOFTMAX_EXP2_SCALE = HEAD_DIM ** -0.5 * LOG2E

PROJ_TN = 512
COL_QA, COL_KA, COL_VA, COL_QI, COL_QB, COL_KB, COL_VB, COL_MISC = 0, 1024, 1536, 2048, 3072, 4096, 4608, 5120
SLAB_COLS = COL_MISC + PROJ_TN
MISC_WI = IDX_DIM


def _nt_dot(a, b):
    return lax.dot_general(a, b, (((1,), (1,)), ((), ())), preferred_element_type=F32)


def _rms(x, g):
    return x * lax.rsqrt(jnp.mean(x * x, axis=-1, keepdims=True) + NORM_EPS) * g


def _rope(x, cos_t, sin_t, half, period):
    width = x.shape[1]
    reps = width // LANES
    c = jnp.concatenate([cos_t] * reps, axis=1) if reps > 1 else cos_t
    s = jnp.concatenate([sin_t] * reps, axis=1) if reps > 1 else sin_t
    lane = lax.broadcasted_iota(I32, x.shape, 1)
    first = (lane & (period - 1)) < half
    up = pltpu.roll(x, width - half, axis=1)
    dn = pltpu.roll(x, half, axis=1)
    return x * c + jnp.where(first, up, dn) * s


def _proj_kernel(x_ref, g_ref, w_ref, c128_ref, s128_ref, c64_ref, s64_ref, gk_ref, bk_ref,
                 slab_ref, ka_ref, va_ref, kb_ref, vb_ref, misc_ref, miscb_ref, h_scr):
    j = pl.program_id(1)

    @pl.when(j == 0)
    def _():
        h_scr[...] = _rms(x_ref[...], g_ref[...]).astype(BF16)

    acc = jnp.dot(h_scr[...], w_ref[...], preferred_element_type=F32)

    def rope128(v):
        return _rope(v, c128_ref[...], s128_ref[...], HEAD_DIM // 8, HEAD_DIM)

    def store_heads(ref, v):
        for kv in range(A_KV_HEADS):
            ref[pl.ds(kv, v.shape[0], stride=A_KV_HEADS), :] = v[:, kv * HEAD_DIM:(kv + 1) * HEAD_DIM]

    is_q128 = (j == COL_QA // PROJ_TN) | (j == COL_QA // PROJ_TN + 1) | \
              (j == COL_QB // PROJ_TN) | (j == COL_QB // PROJ_TN + 1)

    @pl.when(is_q128)
    def _():
        slab_ref[...] = (rope128(acc) * SOFTMAX_EXP2_SCALE).astype(BF16)

    @pl.when(j == COL_KA // PROJ_TN)
    def _():
        r = rope128(acc)
        store_heads(ka_ref, r)
        slab_ref[...] = r.astype(BF16)

    @pl.when(j == COL_KB // PROJ_TN)
    def _():
        r = rope128(acc)
        store_heads(kb_ref, r)
        slab_ref[...] = r.astype(BF16)

    @pl.when(j == COL_VA // PROJ_TN)
    def _():
        store_heads(va_ref, acc)
        slab_ref[...] = acc.astype(BF16)

    @pl.when(j == COL_VB // PROJ_TN)
    def _():
        store_heads(vb_ref, acc)
        slab_ref[...] = acc.astype(BF16)

    @pl.when((j == COL_QI // PROJ_TN) | (j == COL_QI // PROJ_TN + 1))
    def _():
        slab_ref[...] = _rope(acc, c64_ref[...], s64_ref[...], IDX_DIM // 8, IDX_DIM).astype(BF16)

    @pl.when(j == COL_MISC // PROJ_TN)
    def _():
        y = acc[:, :LANES]
        lane = lax.broadcasted_iota(I32, y.shape, 1)
        is_ki = lane < IDX_DIM
        mu = jnp.sum(jnp.where(is_ki, y, 0.0), axis=-1, keepdims=True) * (1.0 / IDX_DIM)
        xc = y - mu
        var = jnp.sum(jnp.where(is_ki, xc * xc, 0.0), axis=-1, keepdims=True) * (1.0 / IDX_DIM)
        ln = xc * lax.rsqrt(var + NORM_EPS) * gk_ref[...] + bk_ref[...]
        ki = _rope(ln, c64_ref[...], s64_ref[...], IDX_DIM // 8, IDX_DIM)
        wi = y * (IDX_HEADS ** -0.5 * IDX_DIM ** -0.5)
        out = jnp.where(is_ki, ki, wi)
        misc_ref[...] = out
        miscb_ref[...] = out.astype(BF16)
        slab_ref[...] = jnp.concatenate(
            [out.astype(BF16), jnp.zeros((out.shape[0], PROJ_TN - LANES), BF16)], axis=1)


def _proj(x, g, w_perm, tabs, gk, bk, tm):
    rows = x.shape[0]
    c128, s128, c64, s64 = tabs
    n_col = SLAB_COLS // PROJ_TN
    row_spec = lambda w: pl.BlockSpec((tm, w), lambda i, j: (i, 0))
    vec_spec = lambda w: pl.BlockSpec((1, w), lambda i, j: (0, 0))
    head_rows_spec = pl.BlockSpec((tm * A_KV_HEADS, HEAD_DIM), lambda i, j: (i, 0))
    head_rows = jax.ShapeDtypeStruct((rows * A_KV_HEADS, HEAD_DIM), F32)
    return pl.pallas_call(
        _proj_kernel,
        grid=(rows // tm, n_col),
        in_specs=[row_spec(D_MODEL), vec_spec(D_MODEL),
                  pl.BlockSpec((D_MODEL, PROJ_TN), lambda i, j: (0, j)),
                  row_spec(LANES), row_spec(LANES), row_spec(LANES), row_spec(LANES),
                  vec_spec(LANES), vec_spec(LANES)],
        out_specs=[pl.BlockSpec((tm, PROJ_TN), lambda i, j: (i, j)),
                   head_rows_spec, head_rows_spec, head_rows_spec, head_rows_spec,
                   row_spec(LANES), row_spec(LANES)],
        out_shape=[jax.ShapeDtypeStruct((rows, SLAB_COLS), BF16),
                   head_rows, head_rows, head_rows, head_rows,
                   jax.ShapeDtypeStruct((rows, LANES), F32),
                   jax.ShapeDtypeStruct((rows, LANES), BF16)],
        scratch_shapes=[pltpu.VMEM((tm, D_MODEL), BF16)],
        name="proj",
        compiler_params=pltpu.CompilerParams(
            dimension_semantics=("parallel", "arbitrary"), vmem_limit_bytes=VMEM_LIMIT),
    )(x, g, w_perm, c128, s128, c64, s64, gk, bk)


def _rope_tables(pos):
    posf = pos.astype(F32)[:, None]

    def tables(dh):
        rot = dh // 4
        half = rot // 2
        inv = jnp.power(jnp.float32(ROPE_THETA), -jnp.arange(half, dtype=F32) * (2.0 / rot))
        ang = posf * inv[None, :]
        cos, sin = jnp.cos(ang), jnp.sin(ang)
        ones = jnp.ones((pos.shape[0], dh - rot), F32)
        c = jnp.concatenate([cos, cos, ones], axis=1)
        s = jnp.concatenate([-sin, sin, 0.0 * ones], axis=1)
        reps = LANES // dh
        return jnp.tile(c, (1, reps)), jnp.tile(s, (1, reps))

    c128, s128 = tables(HEAD_DIM)
    c64, s64 = tables(IDX_DIM)
    return c128, s128, c64, s64


def _permute_w_in(w_in):
    sizes = (A_Q_COLS, A_KV_COLS, A_KV_COLS, I_Q_COLS, IDX_HEADS, IDX_DIM, B_Q_COLS, B_KV_COLS, B_KV_COLS)
    offs = [0]
    for s in sizes:
        offs.append(offs[-1] + s)
    qa, ka, va, qi, wi, ki, qb, kb, vb = [w_in[:, offs[n]:offs[n + 1]] for n in range(len(sizes))]
    pad = jnp.zeros((w_in.shape[0], PROJ_TN - IDX_DIM - IDX_HEADS), w_in.dtype)
    return jnp.concatenate([qa, ka, va, qi, qb, kb, vb, ki, wi, pad], axis=1).astype(BF16)


TOPK_GROUPS = 256


def _sort_key(score):
    score = jnp.where(score == 0.0, 0.0, score)
    bits = pltpu.bitcast(score, I32)
    return bits ^ ((bits >> 31) & 0x7FFFFFFF)


def _fold_lanes(op, x, width):
    parts = [x[:, u * width:(u + 1) * width] for u in range(x.shape[1] // width)]
    while len(parts) > 1:
        nxt = [op(parts[u], parts[u + 1]) for u in range(0, len(parts) - 1, 2)]
        if len(parts) % 2:
            nxt.append(parts[-1])
        parts = nxt
    return parts[0]


def _lane_allreduce(op, x):
    shift = LANES // 2
    while shift >= 1:
        x = op(x, pltpu.roll(x, shift, axis=1))
        shift //= 2
    return x


def _count(keys_ref, nch, cw, pred):
    rows = keys_ref.shape[0]

    def body(c, acc):
        off = pl.multiple_of(c * cw, cw)
        blk = keys_ref[:, pl.ds(off, cw)]
        idx = off + lax.broadcasted_iota(I32, blk.shape, 1)
        return acc + _fold_lanes(jnp.add, jnp.where(pred(blk, idx), 1.0, 0.0), LANES)

    acc = lax.fori_loop(0, nch, body, jnp.zeros((rows, LANES), F32))
    return jnp.sum(acc, axis=1, keepdims=True)


def _row_bounds(keys_ref, nch, cw):
    rows = keys_ref.shape[0]

    def body(c, gmax):
        off = pl.multiple_of(c * cw, cw)
        return jnp.maximum(gmax, _fold_lanes(jnp.maximum, keys_ref[:, pl.ds(off, cw)], TOPK_GROUPS))

    gmax = lax.fori_loop(0, nch, body, jnp.full((rows, TOPK_GROUPS), INT_MIN, I32))
    lo = _lane_allreduce(jnp.minimum, _fold_lanes(jnp.minimum, gmax, LANES))[:, :1]
    top = _lane_allreduce(jnp.maximum, _fold_lanes(jnp.maximum, gmax, LANES))[:, :1]
    return lo, top


def _select_threshold(keys_ref, nch, cw, topk, idx_bits):
    rows = keys_ref.shape[0]
    kf = float(topk)
    lo, top = _row_bounds(keys_ref, nch, cw)
    hi = top + 1

    def n_open(lo, hi):
        return jnp.max(jnp.where(hi - 1 > lo, 1.0, 0.0))

    def cond(st):
        return (st[2] > 0.5) & (st[3] < 40)

    def body(st):
        lo, hi, _, it = st
        mid = (lo & hi) + ((lo ^ hi) >> 1)
        cnt = _count(keys_ref, nch, cw, lambda k, i: k >= mid)
        ge = cnt >= kf
        lo = jnp.where(ge, mid, lo)
        hi = jnp.where(cnt == kf, mid + 1, jnp.where(ge, hi, mid))
        return lo, hi, n_open(lo, hi), it + 1

    tau, _, _, _ = lax.while_loop(cond, body, (lo, hi, n_open(lo, hi), jnp.int32(0)))
    n_ge = _count(keys_ref, nch, cw, lambda k, i: k >= tau)
    has_k = tau > INT_MIN
    tied = has_k & (n_ge > kf)
    any_tied = jnp.max(jnp.where(tied, 1.0, 0.0)) > 0.5

    def resolve(_):
        need = kf - _count(keys_ref, nch, cw, lambda k, i: k > tau)
        lim = jnp.zeros((rows, 1), I32)
        for b in range(idx_bits - 1, -1, -1):
            cand = lim + (1 << b)
            cnt = _count(keys_ref, nch, cw, lambda k, i: (k == tau) & (i < cand))
            lim = jnp.where(cnt < need, cand, lim)
        return lim

    lim = lax.cond(any_tied, resolve, lambda _: jnp.zeros((rows, 1), I32), 0)
    jlim = jnp.where(tied, lim, jnp.where(has_k, jnp.int32(2 ** 31 - 1), jnp.int32(-1)))
    return tau, jlim


def _keys_to_bias(keys_ref, bias_ref, nch, cw, tau, jlim):
    def body(c, carry):
        off = pl.multiple_of(c * cw, cw)
        key = keys_ref[:, pl.ds(off, cw)]
        idx = off + lax.broadcasted_iota(I32, key.shape, 1)
        sel = (key > tau) | ((key == tau) & (idx <= jlim))
        bias_ref[:, pl.ds(off, cw)] = pltpu.bitcast(jnp.where(sel, 0.0, NEG), I32)
        return carry

    lax.fori_loop(0, nch, body, 0)


def _flash_update(sl, d, v, m_scr, acc_scr):
    reps = d.shape[1] // LANES
    m_old = m_scr[sl]
    m_new = jnp.maximum(m_old, jnp.max(d, axis=1, keepdims=True))
    alpha = jnp.exp2(m_old - m_new)
    p = jnp.exp2(d - jnp.tile(m_new, (1, reps)))
    v_ones = jnp.concatenate([v, jnp.ones(v.shape, BF16)], axis=1)
    acc_scr[sl] = jnp.tile(alpha, (1, 2)) * acc_scr[sl] + jnp.dot(p.astype(BF16), v_ones,
                                                                   preferred_element_type=F32)
    m_scr[sl] = m_new


def _flash_init(m_scr, acc_scr):
    m_scr[...] = jnp.full(m_scr.shape, NEG, F32)
    acc_scr[...] = jnp.zeros(acc_scr.shape, F32)


def _flash_result(acc, guard=False):
    l = acc[:, HEAD_DIM:]
    return acc[:, :HEAD_DIM] / (jnp.maximum(l, 1e-30) if guard else l)


def _causal_steps(n_q, chunks_of):
    qb, cb = [], []
    for i in range(n_q):
        for c in range(chunks_of(i)):
            qb.append(i)
            cb.append(c)
    return jnp.asarray(qb, I32), jnp.asarray(cb, I32)


IDX_TN = 512


def _dsa_prompt_kernel(qb_ref, cb_ref, qi_ref, misc_ref, kit_ref, qa_ref, k_ref, v_ref, o_ref,
                       keys_scr, qih_scr, wib_scr, d_scr, m_scr, acc_scr,
                       *, tq, tk, topk, idx_bits):
    step = pl.program_id(0)
    i = qb_ref[step]
    c = cb_ref[step]
    t0 = i * tq
    nch = (t0 + tq + tk - 1) // tk
    rt = 32
    n_chunks = keys_scr.shape[1] // tk
    scan = max(w for w in (1, 2) if n_chunks % w == 0)

    @pl.when(c == 0)
    def _():
        for hh in range(IDX_HEADS):
            qih_scr[hh] = qi_ref[:, hh * IDX_DIM:(hh + 1) * IDX_DIM]
            wib_scr[hh] = jnp.broadcast_to(misc_ref[:, MISC_WI + hh:MISC_WI + hh + 1], (tq, LANES))
        qpos = t0 + lax.broadcasted_iota(I32, (tq, 1), 0)

        def sub_body(kc, carry):
            off = pl.multiple_of(kc * IDX_TN, IDX_TN)
            d_scr[...] = jnp.dot(qih_scr[...].reshape(IDX_HEADS * tq, IDX_DIM),
                                 kit_ref[:, pl.ds(off, IDX_TN)], preferred_element_type=F32)
            kpos = off + lax.broadcasted_iota(I32, (rt, IDX_TN), 1)
            for r in range(tq // rt):
                score = jnp.zeros((rt, IDX_TN), F32)
                for hh in range(IDX_HEADS):
                    w = wib_scr[hh, r * rt:(r + 1) * rt, :]
                    dd = d_scr[hh * tq + r * rt:hh * tq + (r + 1) * rt, :]
                    score = score + jnp.tile(w, (1, IDX_TN // LANES)) * jnp.maximum(dd, 0.0)
                keys_scr[r * rt:(r + 1) * rt, pl.ds(off, IDX_TN)] = jnp.where(
                    kpos <= qpos[r * rt:(r + 1) * rt], _sort_key(score), INT_MIN)
            return carry

        lax.fori_loop(0, nch * (tk // IDX_TN), sub_body, 0)

        n_scan = (nch + scan - 1) // scan

        def pad_body(kc, carry):
            keys_scr[:, pl.ds(pl.multiple_of(kc * tk, tk), tk)] = jnp.full((tq, tk), INT_MIN, I32)
            return carry

        lax.fori_loop(nch, n_scan * scan, pad_body, 0)
        tau, jl = _select_threshold(keys_scr, n_scan, scan * tk, topk, idx_bits)
        _keys_to_bias(keys_scr, keys_scr, n_scan, scan * tk, tau, jl)
        _flash_init(m_scr, acc_scr)

    bias = pltpu.bitcast(keys_scr[:, pl.ds(pl.multiple_of(c * tk, tk), tk)], F32)
    group = A_HEADS // A_KV_HEADS

    def scores(h):
        kv = h // group
        return _nt_dot(qa_ref[:, h * HEAD_DIM:(h + 1) * HEAD_DIM],
                       k_ref[:, kv * HEAD_DIM:(kv + 1) * HEAD_DIM]) + bias

    d_next = scores(0)
    for h in range(A_HEADS):
        d = d_next
        if h + 1 < A_HEADS:
            d_next = scores(h + 1)
        kv = h // group
        _flash_update(h, d, v_ref[:, kv * HEAD_DIM:(kv + 1) * HEAD_DIM], m_scr, acc_scr)

    @pl.when(c == nch - 1)
    def _():
        for h in range(A_HEADS):
            o_ref[:, h * HEAD_DIM:(h + 1) * HEAD_DIM] = _flash_result(acc_scr[h]).astype(o_ref.dtype)


def _dsa_prompt(slab, misc, kit, tq, tk):
    rows = slab.shape[0]
    topk = min(DSA_TOPK, rows // 4)
    qb, cb = _causal_steps(rows // tq, lambda i: (i * tq + tq + tk - 1) // tk)
    kern = functools.partial(_dsa_prompt_kernel, tq=tq, tk=tk, topk=topk,
                             idx_bits=max(1, (rows - 1).bit_length()))
    grid_spec = pltpu.PrefetchScalarGridSpec(
        num_scalar_prefetch=2,
        grid=(int(qb.shape[0]),),
        in_specs=[pl.BlockSpec((tq, I_Q_COLS), lambda s, qb, cb: (qb[s], COL_QI // I_Q_COLS)),
                  pl.BlockSpec((tq, LANES), lambda s, qb, cb: (qb[s], 0)),
                  pl.BlockSpec((IDX_DIM, rows), lambda s, qb, cb: (0, 0)),
                  pl.BlockSpec((tq, A_Q_COLS), lambda s, qb, cb: (qb[s], COL_QA // A_Q_COLS)),
                  pl.BlockSpec((tk, A_KV_COLS), lambda s, qb, cb: (cb[s], COL_KA // A_KV_COLS)),
                  pl.BlockSpec((tk, A_KV_COLS), lambda s, qb, cb: (cb[s], COL_VA // A_KV_COLS))],
        out_specs=pl.BlockSpec((tq, A_Q_COLS), lambda s, qb, cb: (qb[s], 0)),
        scratch_shapes=[pltpu.VMEM((tq, rows), I32),
                        pltpu.VMEM((IDX_HEADS, tq, IDX_DIM), BF16),
                        pltpu.VMEM((IDX_HEADS, tq, LANES), F32),
                        pltpu.VMEM((IDX_HEADS * tq, IDX_TN), F32),
                        pltpu.VMEM((A_HEADS, tq, LANES), F32),
                        pltpu.VMEM((A_HEADS, tq, 2 * HEAD_DIM), F32)])
    return pl.pallas_call(
        kern, grid_spec=grid_spec, name="dsa_prompt",
        out_shape=jax.ShapeDtypeStruct((rows, A_Q_COLS), BF16),
        compiler_params=pltpu.CompilerParams(
            dimension_semantics=("arbitrary",), vmem_limit_bytes=VMEM_LIMIT),
    )(qb, cb, slab, misc, kit, slab, slab, slab)


def _gate_topk(gate, n_valid, ksel):
    lane = lax.broadcasted_iota(I32, gate.shape, 1)
    lanef = lane.astype(F32)
    gate = jnp.where(lane < n_valid, gate, NEG)
    sel = jnp.zeros(gate.shape, F32)
    for _ in range(ksel):
        mx = jnp.max(gate, axis=1, keepdims=True)
        first = jnp.min(jnp.where(gate == mx, lanef, 1e9), axis=1, keepdims=True)
        hit = (lanef == first) & (mx > 0.5 * NEG)
        sel = jnp.where(hit, 1.0, sel)
        gate = jnp.where(hit, NEG, gate)
    return sel


def _block_bias_rows(sel):
    return ((sel - 1.0) * (-NEG)).astype(BF16)


def _block_expansion(n_keys, blk0, keys_on_rows):
    shape = (n_keys, LANES) if keys_on_rows else (LANES, n_keys)
    kdim, bdim = (0, 1) if keys_on_rows else (1, 0)
    kblk = lax.broadcasted_iota(I32, shape, kdim) // MOBA_BLOCK + blk0
    return jnp.where(kblk == lax.broadcasted_iota(I32, shape, bdim), 1.0, 0.0).astype(BF16)


def _block_means_kernel(k_ref, o_ref):
    x = k_ref[...]
    s8 = jnp.sum(x.reshape(x.shape[0] // SUBLANES, SUBLANES, HEAD_DIM), axis=0)
    o_ref[0] = (s8[:B_KV_HEADS] + s8[B_KV_HEADS:]) * (1.0 / MOBA_BLOCK)


def _block_means(kb_rows):
    blk_rows = MOBA_BLOCK * B_KV_HEADS
    nb = kb_rows.shape[0] // blk_rows
    out = pl.pallas_call(
        _block_means_kernel,
        grid=(nb,),
        in_specs=[pl.BlockSpec((blk_rows, HEAD_DIM), lambda n: (n, 0))],
        out_specs=pl.BlockSpec((1, B_KV_HEADS, HEAD_DIM), lambda n: (n, 0, 0)),
        out_shape=jax.ShapeDtypeStruct((nb, B_KV_HEADS, HEAD_DIM), F32),
        name="block_means",
    )(kb_rows)
    return out.reshape(nb, B_KV_COLS)


def _moba_prompt_kernel(qb_ref, cb_ref, q_ref, means_ref, k_ref, v_ref, o_ref,
                        qaug_scr, m_scr, acc_scr, *, tq, tk, ksel):
    step = pl.program_id(0)
    i = qb_ref[step]
    c = cb_ref[step]
    t0 = i * tq
    qblk = t0 // MOBA_BLOCK
    bpc = tk // MOBA_BLOCK
    nch = qblk // bpc + 1
    group = B_HEADS // B_KV_HEADS

    @pl.when(c == 0)
    def _():
        lane = lax.broadcasted_iota(I32, (tq, LANES), 1)
        for h in range(B_HEADS):
            kv = h // group
            q = q_ref[:, h * HEAD_DIM:(h + 1) * HEAD_DIM]
            gate = _nt_dot(q, means_ref[:, kv * HEAD_DIM:(kv + 1) * HEAD_DIM])
            sel = jnp.where(lane == qblk, 1.0, _gate_topk(gate, qblk, ksel))
            qaug_scr[h] = jnp.concatenate([q, _block_bias_rows(sel)], axis=1)
        _flash_init(m_scr, acc_scr)

    e_t = _block_expansion(tk, c * bpc, keys_on_rows=True)
    qlim = t0 + lax.broadcasted_iota(I32, (tq, 1), 0) + jnp.where(c == nch - 1, 0, 2 ** 30)
    kpos = c * tk + lax.broadcasted_iota(I32, (tq, tk), 1)
    causal = jnp.where(kpos <= qlim, 0.0, NEG)
    for kv in range(B_KV_HEADS):
        rhs = jnp.concatenate([k_ref[:, kv * HEAD_DIM:(kv + 1) * HEAD_DIM], e_t], axis=1)
        vv = v_ref[:, kv * HEAD_DIM:(kv + 1) * HEAD_DIM]
        for g in range(group):
            h = kv * group + g
            _flash_update(h, _nt_dot(qaug_scr[h], rhs) + causal, vv, m_scr, acc_scr)

    @pl.when(c == nch - 1)
    def _():
        for h in range(B_HEADS):
            o_ref[:, h * HEAD_DIM:(h + 1) * HEAD_DIM] = _flash_result(acc_scr[h]).astype(o_ref.dtype)


def _moba_prompt(slab, means_pad, tq, tk):
    rows = slab.shape[0]
    ksel = min(MOBA_TOPK, (rows - 1) // MOBA_BLOCK)
    assert tq == MOBA_BLOCK and tk % MOBA_BLOCK == 0 and means_pad.shape[0] == LANES
    qb, cb = _causal_steps(rows // tq, lambda i: (i * tq) // tk + 1)
    kern = functools.partial(_moba_prompt_kernel, tq=tq, tk=tk, ksel=ksel)
    grid_spec = pltpu.PrefetchScalarGridSpec(
        num_scalar_prefetch=2,
        grid=(int(qb.shape[0]),),
        in_specs=[pl.BlockSpec((tq, B_Q_COLS), lambda s, qb, cb: (qb[s], COL_QB // B_Q_COLS)),
                  pl.BlockSpec((LANES, B_KV_COLS), lambda s, qb, cb: (0, 0)),
                  pl.BlockSpec((tk, B_KV_COLS), lambda s, qb, cb: (cb[s], COL_KB // B_KV_COLS)),
                  pl.BlockSpec((tk, B_KV_COLS), lambda s, qb, cb: (cb[s], COL_VB // B_KV_COLS))],
        out_specs=pl.BlockSpec((tq, B_Q_COLS), lambda s, qb, cb: (qb[s], 0)),
        scratch_shapes=[pltpu.VMEM((B_HEADS, tq, 2 * HEAD_DIM), BF16),
                        pltpu.VMEM((B_HEADS, tq, LANES), F32),
                        pltpu.VMEM((B_HEADS, tq, 2 * HEAD_DIM), F32)])
    return pl.pallas_call(
        kern, grid_spec=grid_spec, name="moba_prompt",
        out_shape=jax.ShapeDtypeStruct((rows, B_Q_COLS), BF16),
        compiler_params=pltpu.CompilerParams(
            dimension_semantics=("arbitrary",), vmem_limit_bytes=VMEM_LIMIT),
    )(qb, cb, slab, means_pad, slab, slab)


TPAD = SUBLANES
SROWS = 8 * TPAD
KV_ROWS = SROWS // 4
PAGE_ROWS = PAGE_SIZE * 4


def _page_head(ref, kv):
    return ref[0, pl.ds(kv, PAGE_SIZE, stride=4), :]


def _paged_spec(shape, n_pages, per_step, first_step, n_steps, j):
    def index(b, s, pt):
        local = jnp.clip(s - first_step, 0, n_steps - 1)
        return (pt[b * n_pages + local * per_step + j], 0, 0)
    return pl.BlockSpec(shape, index)


def _dsa_sample_index_kernel(pt_ref, qi_ref, wi_ref, *rest, n_pages, pa, t_new):
    kidx_refs = rest[:pa]
    kin_ref, keys_ref, wib_scr = rest[pa:]
    keys_scr = keys_ref.at[0]
    s = pl.program_id(1)
    na = n_pages // pa
    past = n_pages * PAGE_SIZE
    ntot = keys_scr.shape[1]
    trow = lax.broadcasted_iota(I32, (TPAD, 1), 0)

    def index_keys(kd, off):
        width = kd.shape[1]
        x = jnp.tile(wib_scr[...], (1, width // LANES)) * jnp.maximum(
            jnp.dot(qi_ref[0], kd, preferred_element_type=F32), 0.0)
        score = x[:TPAD]
        for hh in range(1, IDX_HEADS):
            score = score + x[hh * TPAD:(hh + 1) * TPAD]
        kpos = off + lax.broadcasted_iota(I32, (TPAD, width), 1)
        visible = (kpos <= past + trow) & (trow < t_new)
        keys_scr[:, pl.ds(off, width)] = jnp.where(visible, _sort_key(score), INT_MIN)

    @pl.when(s == 0)
    def _():
        wib_scr[...] = jnp.broadcast_to(wi_ref[0], wib_scr.shape)
        keys_scr[:, past:] = jnp.full((TPAD, ntot - past), INT_MIN, I32)

    @pl.when(s < na)
    def _():
        kd = jnp.concatenate([r[0].astype(BF16) for r in kidx_refs], axis=1)
        index_keys(kd, pl.multiple_of(s * (pa * PAGE_SIZE), pa * PAGE_SIZE))

    @pl.when(s == na - 1)
    def _():
        index_keys(kin_ref[0], past)


def _topk_bias_kernel(keys_ref, bias_ref, *, cw, topk, idx_bits):
    nch = keys_ref.shape[1] // cw
    tau, jl = _select_threshold(keys_ref, nch, cw, topk, idx_bits)
    _keys_to_bias(keys_ref, bias_ref, nch, cw, tau, jl)


def _dsa_sample_attend_kernel(pt_ref, bias_ref, qa_ref, *rest, n_pages, pb):
    k_refs, v_refs = rest[:pb], rest[pb:2 * pb]
    kn_ref, vn_ref, o_ref, m_scr, acc_scr = rest[2 * pb:]
    keys_scr = bias_ref.at[0]
    s = pl.program_id(1)
    past = n_pages * PAGE_SIZE

    @pl.when(s == 0)
    def _():
        _flash_init(m_scr, acc_scr)

    def attend(key_of, val_of, off, width):
        bias = pltpu.bitcast(keys_scr[:, pl.ds(off, width)], F32)
        bias = jnp.concatenate([bias] * (KV_ROWS // TPAD), axis=0)
        for kv in range(A_KV_HEADS):
            rows = slice(kv * KV_ROWS, (kv + 1) * KV_ROWS)
            _flash_update(rows, _nt_dot(qa_ref[0, rows], key_of(kv)) + bias, val_of(kv), m_scr, acc_scr)

    def gather(refs, kv):
        return jnp.concatenate([_page_head(r, kv) for r in refs], axis=0).astype(BF16)

    attend(lambda kv: gather(k_refs, kv), lambda kv: gather(v_refs, kv),
           pl.multiple_of(s * (pb * PAGE_SIZE), pb * PAGE_SIZE), pb * PAGE_SIZE)

    @pl.when(s == n_pages // pb - 1)
    def _():
        attend(lambda kv: kn_ref[0, kv], lambda kv: vn_ref[0, kv], past, PAGE_SIZE)
        o_ref[0] = _flash_result(acc_scr[...], guard=True)


def _dsa_sample(pt_flat, n_pages, t_new, qi_s, wi_s, kidx_pages, ki_new_t, qa_s, k_pages, v_pages, ka_new, va_new,
                pa, pb):
    ns = qi_s.shape[0]
    past = n_pages * PAGE_SIZE
    topk = min(DSA_TOPK, (past + t_new) // 4)
    cw = 1024
    ntot = -(-(past + PAGE_SIZE) // cw) * cw
    na, nb = n_pages // pa, n_pages // pb
    seq3 = lambda s1, s2: pl.BlockSpec((1, s1, s2), lambda b, s, pt: (b, 0, 0))
    seq4 = pl.BlockSpec((1, A_KV_HEADS, PAGE_SIZE, HEAD_DIM), lambda b, s, pt: (b, 0, 0, 0))
    params = pltpu.CompilerParams(dimension_semantics=("arbitrary", "arbitrary"), vmem_limit_bytes=VMEM_LIMIT)

    keys = pl.pallas_call(
        functools.partial(_dsa_sample_index_kernel, n_pages=n_pages, pa=pa, t_new=t_new),
        grid_spec=pltpu.PrefetchScalarGridSpec(
            num_scalar_prefetch=1,
            grid=(ns, na),
            in_specs=[seq3(IDX_HEADS * TPAD, IDX_DIM), seq3(IDX_HEADS * TPAD, 1)]
                     + [_paged_spec((1, IDX_DIM, PAGE_SIZE), n_pages, pa, 0, na, j) for j in range(pa)]
                     + [seq3(IDX_DIM, PAGE_SIZE)],
            out_specs=seq3(TPAD, ntot),
            scratch_shapes=[pltpu.VMEM((IDX_HEADS * TPAD, LANES), F32)]),
        out_shape=jax.ShapeDtypeStruct((ns, TPAD, ntot), I32),
        name="dsa_sample_index", compiler_params=params,
    )(pt_flat, qi_s, wi_s, *([kidx_pages] * pa), ki_new_t)

    rows = ns * t_new
    rows_pad = -(-rows // SUBLANES) * SUBLANES
    keys2d = _pad_rows(keys[:, :t_new].reshape(rows, ntot), 0, rows_pad)
    bias2d = pl.pallas_call(
        functools.partial(_topk_bias_kernel, cw=cw, topk=topk, idx_bits=max(1, (ntot - 1).bit_length())),
        out_shape=jax.ShapeDtypeStruct((rows_pad, ntot), I32),
        name="dsa_sample_select", compiler_params=pltpu.CompilerParams(vmem_limit_bytes=VMEM_LIMIT),
    )(keys2d)
    bias = _pad_rows(bias2d[:rows].reshape(ns, t_new, ntot), 1, TPAD)

    page_kv = lambda j: _paged_spec((1, PAGE_ROWS, HEAD_DIM), n_pages, pb, 0, nb, j)
    return pl.pallas_call(
        functools.partial(_dsa_sample_attend_kernel, n_pages=n_pages, pb=pb),
        grid_spec=pltpu.PrefetchScalarGridSpec(
            num_scalar_prefetch=1,
            grid=(ns, nb),
            in_specs=[seq3(TPAD, ntot), seq3(SROWS, HEAD_DIM)]
                     + [page_kv(j) for j in range(pb)] + [page_kv(j) for j in range(pb)]
                     + [seq4, seq4],
            out_specs=seq3(SROWS, HEAD_DIM),
            scratch_shapes=[pltpu.VMEM((SROWS, LANES), F32),
                            pltpu.VMEM((SROWS, 2 * HEAD_DIM), F32)]),
        out_shape=jax.ShapeDtypeStruct((ns, SROWS, HEAD_DIM), F32),
        name="dsa_sample_attend", compiler_params=params,
    )(pt_flat, bias, qa_s, *([k_pages] * pb), *([v_pages] * pb), ka_new, va_new)


def _moba_sample_kernel(pt_ref, q_ref, *rest, n_pages, pb, t_new, ksel):
    k_refs, v_refs = rest[:pb], rest[pb:2 * pb]
    kn_ref, vn_ref, o_ref, bsum_scr, s_scr, seln_scr, m_scr, acc_scr = rest[2 * pb:]
    s = pl.program_id(1)
    nst = n_pages // pb
    past = n_pages * PAGE_SIZE
    nbp = past // MOBA_BLOCK
    bps = pb * PAGE_SIZE // MOBA_BLOCK
    width = pb * PAGE_SIZE
    tok = lax.broadcasted_iota(I32, (KV_ROWS, 1), 0) % TPAD
    pad_bias = jnp.where(tok < t_new, 0.0, NEG)
    kv_rows = lambda kv: slice(kv * KV_ROWS, (kv + 1) * KV_ROWS)

    @pl.when(s == 0)
    def _():
        bsum_scr[...] = jnp.zeros(bsum_scr.shape, F32)

    @pl.when(s < nst)
    def _():
        off = pl.multiple_of(s * width, width)
        for kv in range(B_KV_HEADS):
            kf = jnp.concatenate([_page_head(r, kv) for r in k_refs], axis=0)
            bsum_scr[kv, pl.ds(pl.multiple_of(s * bps, bps), bps), :] = jnp.sum(
                kf.reshape(bps, MOBA_BLOCK, HEAD_DIM), axis=1)
            s_scr[kv_rows(kv), pl.ds(off, width)] = _nt_dot(q_ref[0, kv_rows(kv)], kf.astype(BF16))

    @pl.when(s == nst - 1)
    def _():
        for kv in range(B_KV_HEADS):
            q = q_ref[0, kv_rows(kv)]
            s_scr[kv_rows(kv), pl.ds(past, PAGE_SIZE)] = _nt_dot(q, kn_ref[0, kv])
            means = (bsum_scr[kv] * (1.0 / MOBA_BLOCK)).astype(BF16)
            seln_scr[kv_rows(kv)] = _block_bias_rows(_gate_topk(_nt_dot(q, means), nbp, ksel))
        _flash_init(m_scr, acc_scr)

    @pl.when(s >= nst)
    def _():
        sb = s - nst
        off = pl.multiple_of(sb * width, width)
        expand = _block_expansion(width, sb * bps, keys_on_rows=False)
        for kv in range(B_KV_HEADS):
            bias = jnp.dot(seln_scr[kv_rows(kv)], expand, preferred_element_type=F32) + pad_bias
            vv = jnp.concatenate([_page_head(r, kv) for r in v_refs], axis=0).astype(BF16)
            _flash_update(kv_rows(kv), s_scr[kv_rows(kv), pl.ds(off, width)] + bias, vv, m_scr, acc_scr)

    @pl.when(s == 2 * nst - 1)
    def _():
        kpos = past + lax.broadcasted_iota(I32, (KV_ROWS, PAGE_SIZE), 1)
        bias = jnp.where(kpos <= past + tok, 0.0, NEG) + pad_bias
        for kv in range(B_KV_HEADS):
            _flash_update(kv_rows(kv), s_scr[kv_rows(kv), pl.ds(past, PAGE_SIZE)] + bias, vn_ref[0, kv],
                          m_scr, acc_scr)
        o_ref[0] = _flash_result(acc_scr[...], guard=True)


def _moba_sample(pt_flat, n_pages, t_new, qb_s, k_pages, v_pages, kb_new, vb_new, pb):
    ns = qb_s.shape[0]
    past = n_pages * PAGE_SIZE
    nbp = past // MOBA_BLOCK
    assert nbp <= LANES and (pb * PAGE_SIZE // MOBA_BLOCK) % SUBLANES == 0
    ksel = min(MOBA_TOPK, nbp)
    nst = n_pages // pb
    seq3 = lambda s1, s2: pl.BlockSpec((1, s1, s2), lambda b, s, pt: (b, 0, 0))
    seq4 = pl.BlockSpec((1, B_KV_HEADS, PAGE_SIZE, HEAD_DIM), lambda b, s, pt: (b, 0, 0, 0))
    kern = functools.partial(_moba_sample_kernel, n_pages=n_pages, pb=pb, t_new=t_new, ksel=ksel)
    grid_spec = pltpu.PrefetchScalarGridSpec(
        num_scalar_prefetch=1,
        grid=(ns, 2 * nst),
        in_specs=[seq3(SROWS, HEAD_DIM)]
                 + [_paged_spec((1, PAGE_ROWS, HEAD_DIM), n_pages, pb, 0, nst, j) for j in range(pb)]
                 + [_paged_spec((1, PAGE_ROWS, HEAD_DIM), n_pages, pb, nst, nst, j) for j in range(pb)]
                 + [seq4, seq4],
        out_specs=seq3(SROWS, HEAD_DIM),
        scratch_shapes=[pltpu.VMEM((B_KV_HEADS, LANES, HEAD_DIM), F32),
                        pltpu.VMEM((SROWS, past + PAGE_SIZE), F32),
                        pltpu.VMEM((SROWS, LANES), BF16),
                        pltpu.VMEM((SROWS, LANES), F32),
                        pltpu.VMEM((SROWS, 2 * HEAD_DIM), F32)])
    return pl.pallas_call(
        kern, grid_spec=grid_spec, name="moba_sample",
        out_shape=jax.ShapeDtypeStruct((ns, SROWS, HEAD_DIM), F32),
        compiler_params=pltpu.CompilerParams(
            dimension_semantics=("arbitrary", "arbitrary"), vmem_limit_bytes=VMEM_LIMIT),
    )(pt_flat, qb_s, *([k_pages] * pb), *([v_pages] * pb), kb_new, vb_new)


def _memory_kv_kernel(mem_ref, g_ref, w_ref, k_ref, v_ref):
    h = _rms(mem_ref[...], g_ref[...]).astype(BF16)
    kv = jnp.dot(h, w_ref[...], preferred_element_type=F32)
    k_ref[...] = kv[:, :MEM_WIDTH]
    v_ref[...] = kv[:, MEM_WIDTH:]


def _memory_kv(mem, g, w_bf):
    m = mem.shape[0]
    return pl.pallas_call(
        _memory_kv_kernel,
        out_shape=[jax.ShapeDtypeStruct((m, MEM_WIDTH), F32), jax.ShapeDtypeStruct((m, MEM_WIDTH), F32)],
        name="memory_kv",
        compiler_params=pltpu.CompilerParams(vmem_limit_bytes=VMEM_LIMIT),
    )(mem, g, w_bf)


def _outproj_cross_kernel(x_ref, oa_ref, ob_ref, woa_ref, wob_ref, g_ref, wq_ref, mk_ref, mv_ref, wo_ref,
                          y_ref, *, rows_per_seq):
    x1 = x_ref[...] + jnp.dot(oa_ref[...], woa_ref[...], preferred_element_type=F32) \
        + jnp.dot(ob_ref[...], wob_ref[...], preferred_element_type=F32)
    hc = _rms(x1, g_ref[...]).astype(BF16)
    q = jnp.dot(hc, wq_ref[...], preferred_element_type=F32).astype(BF16)
    tm = q.shape[0]
    nk = mk_ref.shape[0]
    scale = MEM_HEAD_DIM ** -0.5
    if rows_per_seq is not None:
        rseq = (pl.program_id(0) * tm + lax.broadcasted_iota(I32, (tm, 1), 0)) // rows_per_seq
        kseq = lax.broadcasted_iota(I32, (tm, nk), 1) // MEM_TOKENS
        mask = kseq == rseq
    outs = []
    for h in range(MEM_HEADS):
        sl = slice(h * MEM_HEAD_DIM, (h + 1) * MEM_HEAD_DIM)
        s = _nt_dot(q[:, sl], mk_ref[:, sl]) * scale
        if rows_per_seq is not None:
            s = jnp.where(mask, s, NEG)
        m = jnp.max(s, axis=1, keepdims=True)
        e = jnp.exp(s - m)
        p = e / jnp.sum(e, axis=1, keepdims=True)
        outs.append(jnp.dot(p.astype(BF16), mv_ref[:, sl], preferred_element_type=F32))
    o = jnp.concatenate(outs, axis=1).astype(BF16)
    y_ref[...] = x1 + jnp.dot(o, wo_ref[...], preferred_element_type=F32)


def _outproj_cross(x, oa, ob, w_out_bf, g_cross, w_mq_bf, mk_bf, mv_bf, w_mo_bf, tm, rows_per_seq):
    rows = x.shape[0]
    nk = mk_bf.shape[0]
    row_spec = lambda w: pl.BlockSpec((tm, w), lambda i: (i, 0))
    full = lambda a, b: pl.BlockSpec((a, b), lambda i: (0, 0))
    kern = functools.partial(_outproj_cross_kernel, rows_per_seq=rows_per_seq)
    return pl.pallas_call(
        kern,
        grid=(rows // tm,),
        in_specs=[row_spec(D_MODEL), row_spec(A_Q_COLS), row_spec(B_Q_COLS),
                  pl.BlockSpec((A_Q_COLS, D_MODEL), lambda i: (0, 0)),
                  pl.BlockSpec((B_Q_COLS, D_MODEL), lambda i: (1, 0)),
                  full(1, D_MODEL), full(D_MODEL, MEM_WIDTH), full(nk, MEM_WIDTH), full(nk, MEM_WIDTH),
                  full(MEM_WIDTH, D_MODEL)],
        out_specs=row_spec(D_MODEL),
        out_shape=jax.ShapeDtypeStruct((rows, D_MODEL), F32),
        name="outproj_cross",
        compiler_params=pltpu.CompilerParams(
            dimension_semantics=("parallel",), vmem_limit_bytes=VMEM_LIMIT),
    )(x, oa, ob, w_out_bf, w_out_bf, g_cross, w_mq_bf, mk_bf, mv_bf, w_mo_bf)


def _mlp_final_kernel(x_ref, g_ref, wu_ref, wd_ref, gf_ref, y_ref, h_scr, acc_scr):
    f = pl.program_id(1)

    @pl.when(f == 0)
    def _():
        h_scr[...] = _rms(x_ref[...], g_ref[...]).astype(BF16)
        acc_scr[...] = jnp.zeros(acc_scr.shape, F32)

    u = jnp.maximum(jnp.dot(h_scr[...], wu_ref[...], preferred_element_type=F32), 0.0)
    acc_scr[...] += jnp.dot((u * u).astype(BF16), wd_ref[...], preferred_element_type=F32)

    @pl.when(f == pl.num_programs(1) - 1)
    def _():
        y_ref[...] = _rms(x_ref[...] + acc_scr[...], gf_ref[...])


def _mlp_final(x, g_ffn, w_up_bf, w_down_bf, g_final, tm, tf):
    rows = x.shape[0]
    d_ff = w_up_bf.shape[1]
    return pl.pallas_call(
        _mlp_final_kernel,
        grid=(rows // tm, d_ff // tf),
        in_specs=[pl.BlockSpec((tm, D_MODEL), lambda i, f: (i, 0)),
                  pl.BlockSpec((1, D_MODEL), lambda i, f: (0, 0)),
                  pl.BlockSpec((D_MODEL, tf), lambda i, f: (0, f)),
                  pl.BlockSpec((tf, D_MODEL), lambda i, f: (f, 0)),
                  pl.BlockSpec((1, D_MODEL), lambda i, f: (0, 0))],
        out_specs=pl.BlockSpec((tm, D_MODEL), lambda i, f: (i, 0)),
        out_shape=jax.ShapeDtypeStruct((rows, D_MODEL), F32),
        scratch_shapes=[pltpu.VMEM((tm, D_MODEL), BF16), pltpu.VMEM((tm, D_MODEL), F32)],
        name="mlp_final",
        compiler_params=pltpu.CompilerParams(
            dimension_semantics=("parallel", "arbitrary"), vmem_limit_bytes=VMEM_LIMIT),
    )(x, g_ffn, w_up_bf, w_down_bf, g_final)


def _row_tile(rows, want):
    t = min(rows, want)
    while rows % t:
        t //= 2
    return t


def _pad_rows(a, axis, size):
    pad = [(0, 0)] * a.ndim
    pad[axis] = (0, size - a.shape[axis])
    return jnp.pad(a, pad)


def _heads_first(a, ns, t_new, heads, dim, t_pad):
    a = a.reshape(ns, t_new, heads, dim).transpose(0, 2, 1, 3)
    return _pad_rows(a, 2, t_pad)


def kernel(x_prompt, x_sample, cache_k_a, cache_v_a, cache_kidx, cache_k_b, cache_v_b, cache_mem_k,
           cache_mem_v, page_table, mem_prompt, g_mix, w_in, g_kidx, b_kidx, w_out, g_cross, w_mq, g_mem,
           w_mkv, w_mo, g_ffn, w_up, w_down, g_final):
    batch, s_len, _ = x_prompt.shape
    ns, t_new, _ = x_sample.shape
    depth = w_in.shape[0]
    n_pages = page_table.shape[1]
    past = n_pages * PAGE_SIZE
    n_phys = cache_k_a.shape[1]
    assert batch == 1 and depth == 1
    assert s_len % 1024 == 0 and s_len // MOBA_BLOCK <= LANES
    assert past % MOBA_BLOCK == 0 and t_new <= TPAD
    l = 0
    row = lambda v: v.reshape(1, -1)

    w_in_p = _permute_w_in(w_in[l])
    gk = row(jnp.concatenate([g_kidx[l], jnp.zeros((LANES - IDX_DIM,), F32)]))
    bk = row(jnp.concatenate([b_kidx[l], jnp.zeros((LANES - IDX_DIM,), F32)]))
    w_out_bf = w_out[l].astype(BF16)
    w_mq_bf = w_mq[l].astype(BF16)
    w_mo_bf = w_mo[l].astype(BF16)
    w_up_bf = w_up[l].astype(BF16)
    w_down_bf = w_down[l].astype(BF16)

    xp = x_prompt.reshape(s_len, D_MODEL)
    tabs_p = _rope_tables(jnp.arange(s_len))
    slab, ka, va, kb, vb, misc, misc_bf = _proj(xp, row(g_mix[l]), w_in_p, tabs_p, gk, bk, _row_tile(s_len, 512))
    kit = misc_bf[:, :IDX_DIM].T
    oa = _dsa_prompt(slab, misc, kit, tq=256, tk=1024)
    means_pad = _pad_rows(_block_means(kb).astype(BF16), 0, LANES)
    ob = _moba_prompt(slab, means_pad, tq=MOBA_BLOCK, tk=1024)
    mk, mv = _memory_kv(mem_prompt.reshape(MEM_TOKENS, D_MODEL), row(g_mem[l]), w_mkv[l].astype(BF16))
    xp2 = _outproj_cross(xp, oa, ob, w_out_bf, row(g_cross[l]), w_mq_bf, mk.astype(BF16), mv.astype(BF16),
                         w_mo_bf, _row_tile(s_len, 256), None)
    y_prompt = _mlp_final(xp2, row(g_ffn[l]), w_up_bf, w_down_bf, row(g_final), _row_tile(s_len, 512), 1024)

    rows_s = ns * t_new
    xs = x_sample.reshape(rows_s, D_MODEL)
    pos_s = jnp.tile(past + jnp.arange(t_new), ns)
    tabs_s = _rope_tables(pos_s)
    rows_pad = -(-rows_s // SUBLANES) * SUBLANES
    xs_pad = _pad_rows(xs, 0, rows_pad)
    tabs_s = tuple(_pad_rows(t, 0, rows_pad) for t in tabs_s)
    slab_s, ka_s, va_s, kb_s, vb_s, misc_s, _ = _proj(xs_pad, row(g_mix[l]), w_in_p, tabs_s, gk, bk,
                                                      _row_tile(rows_pad, 128))
    slab_s, misc_s = slab_s[:rows_s], misc_s[:rows_s]
    ka_s, va_s, kb_s, vb_s = [a[:rows_s * A_KV_HEADS] for a in (ka_s, va_s, kb_s, vb_s)]

    pt_flat = page_table.reshape(-1).astype(I32)
    index_pages, attend_pages = min(32, n_pages), min(16, n_pages)
    assert n_pages % index_pages == 0 and n_pages % attend_pages == 0
    page_view = lambda c: c[l].reshape(n_phys, PAGE_ROWS, HEAD_DIM)
    kidx_view = jnp.swapaxes(cache_kidx[l], 1, 2)
    cols = lambda c0, w: slab_s[:, c0:c0 + w]
    qi_s = _heads_first(cols(COL_QI, I_Q_COLS), ns, t_new, IDX_HEADS, IDX_DIM, TPAD)
    qi_s = qi_s.reshape(ns, IDX_HEADS * TPAD, IDX_DIM)
    wi_s = _heads_first(misc_s[:, MISC_WI:MISC_WI + IDX_HEADS], ns, t_new, IDX_HEADS, 1, TPAD)
    wi_s = wi_s.reshape(ns, IDX_HEADS * TPAD, 1)
    ki_new_t = _pad_rows(cols(COL_MISC, IDX_DIM).reshape(ns, t_new, IDX_DIM).transpose(0, 2, 1), 2, PAGE_SIZE)
    q_rows = lambda c0: _heads_first(cols(c0, A_Q_COLS), ns, t_new, A_HEADS, HEAD_DIM, TPAD).reshape(
        ns, SROWS, HEAD_DIM)
    new_kv = lambda c0: _heads_first(cols(c0, A_KV_COLS), ns, t_new, A_KV_HEADS, HEAD_DIM, PAGE_SIZE)
    oa_s = _dsa_sample(pt_flat, n_pages, t_new, qi_s, wi_s, kidx_view, ki_new_t, q_rows(COL_QA),
                       page_view(cache_k_a), page_view(cache_v_a), new_kv(COL_KA), new_kv(COL_VA),
                       index_pages, attend_pages)
    ob_s = _moba_sample(pt_flat, n_pages, t_new, q_rows(COL_QB), page_view(cache_k_b), page_view(cache_v_b),
                        new_kv(COL_KB), new_kv(COL_VB), attend_pages)
    tokens_first = lambda o: o.reshape(ns, A_HEADS, TPAD, HEAD_DIM)[:, :, :t_new].transpose(0, 2, 1, 3).reshape(
        rows_s, -1).astype(BF16)
    oa_s2 = _pad_rows(tokens_first(oa_s), 0, rows_pad)
    ob_s2 = _pad_rows(tokens_first(ob_s), 0, rows_pad)
    mk_s = cache_mem_k[l].reshape(ns * MEM_TOKENS, MEM_WIDTH).astype(BF16)
    mv_s = cache_mem_v[l].reshape(ns * MEM_TOKENS, MEM_WIDTH).astype(BF16)
    xs2 = _outproj_cross(xs_pad, oa_s2, ob_s2, w_out_bf, row(g_cross[l]), w_mq_bf, mk_s, mv_s, w_mo_bf,
                         rows_pad, t_new)
    y_sample = _mlp_final(xs2, row(g_ffn[l]), w_up_bf, w_down_bf, row(g_final), rows_pad, 512)[:rows_s]

    kv5 = lambda a, n, t: a.reshape(1, n, t, A_KV_HEADS, HEAD_DIM)
    return (y_prompt.reshape(batch, s_len, D_MODEL), y_sample.reshape(ns, t_new, D_MODEL),
            kv5(ka, batch, s_len), kv5(va, batch, s_len),
            misc[:, :IDX_DIM].reshape(1, batch, s_len, IDX_DIM),
            kv5(kb, batch, s_len), kv5(vb, batch, s_len),
            mk.reshape(1, batch, MEM_TOKENS, MEM_HEADS, MEM_HEAD_DIM),
            mv.reshape(1, batch, MEM_TOKENS, MEM_HEADS, MEM_HEAD_DIM),
            kv5(ka_s, ns, t_new), kv5(va_s, ns, t_new),
            misc_s[:, :IDX_DIM].reshape(1, ns, t_new, IDX_DIM),
            kv5(kb_s, ns, t_new), kv5(vb_s, ns, t_new))
```

```python
import functools

import jax
import jax.numpy as jnp
from jax import lax
from jax.experimental import pallas as pl
from jax.experimental.pallas import tpu as pltpu

F32 = jnp.float32
BF16 = jnp.bfloat16
I32 = jnp.int32

D_MODEL = 2048
HEAD_DIM = 128
A_HEADS = 8
A_KV_HEADS = 4
B_HEADS = 8
B_KV_HEADS = 4
IDX_HEADS = 16
IDX_DIM = 64
DSA_TOPK = 256
MOBA_BLOCK = 256
MOBA_TOPK = 3
MEM_TOKENS = 256
MEM_HEADS = 4
MEM_HEAD_DIM = 128
PAGE_SIZE = 128
ROPE_THETA = 500000.0
NORM_EPS = 1e-6

A_Q_COLS = A_HEADS * HEAD_DIM
A_KV_COLS = A_KV_HEADS * HEAD_DIM
I_Q_COLS = IDX_HEADS * IDX_DIM
B_Q_COLS = B_HEADS * HEAD_DIM
B_KV_COLS = B_KV_HEADS * HEAD_DIM
MEM_WIDTH = MEM_HEADS * MEM_HEAD_DIM

LANES = 128
SUBLANES = 8
VMEM_LIMIT = 56 * 1024 * 1024
INT_MIN = -(2 ** 31)
NEG = -1e30
LOG2E = 1.4426950408889634
SOFTMAX_EXP2_SCALE = HEAD_DIM ** -0.5 * LOG2E

PROJ_TN = 512
COL_QA, COL_KA, COL_VA, COL_QI, COL_QB, COL_KB, COL_VB, COL_MISC = 0, 1024, 1536, 2048, 3072, 4096, 4608, 5120
SLAB_COLS = COL_MISC + PROJ_TN
MISC_WI = IDX_DIM


def _nt_dot(a, b):
    return lax.dot_general(a, b, (((1,), (1,)), ((), ())), preferred_element_type=F32)


def _rms(x, g):
    return x * lax.rsqrt(jnp.mean(x * x, axis=-1, keepdims=True) + NORM_EPS) * g


def _rope(x, cos_t, sin_t, half, period):
    width = x.shape[1]
    reps = width // LANES
    c = jnp.concatenate([cos_t] * reps, axis=1) if reps > 1 else cos_t
    s = jnp.concatenate([sin_t] * reps, axis=1) if reps > 1 else sin_t
    lane = lax.broadcasted_iota(I32, x.shape, 1)
    first = (lane & (period - 1)) < half
    up = pltpu.roll(x, width - half, axis=1)
    dn = pltpu.roll(x, half, axis=1)
    return x * c + jnp.where(first, up, dn) * s


def _proj_kernel(x_ref, g_ref, w_ref, c128_ref, s128_ref, c64_ref, s64_ref, gk_ref, bk_ref,
                 slab_ref, ka_ref, va_ref, kb_ref, vb_ref, misc_ref, miscb_ref, h_scr):
    j = pl.program_id(1)

    @pl.when(j == 0)
    def _():
        h_scr[...] = _rms(x_ref[...], g_ref[...]).astype(BF16)

    acc = jnp.dot(h_scr[...], w_ref[...], preferred_element_type=F32)

    def rope128(v):
        return _rope(v, c128_ref[...], s128_ref[...], HEAD_DIM // 8, HEAD_DIM)

    def store_heads(ref, v):
        for kv in range(A_KV_HEADS):
            ref[pl.ds(kv, v.shape[0], stride=A_KV_HEADS), :] = v[:, kv * HEAD_DIM:(kv + 1) * HEAD_DIM]

    is_q128 = (j == COL_QA // PROJ_TN) | (j == COL_QA // PROJ_TN + 1) | \
              (j == COL_QB // PROJ_TN) | (j == COL_QB // PROJ_TN + 1)

    @pl.when(is_q128)
    def _():
        slab_ref[...] = (rope128(acc) * SOFTMAX_EXP2_SCALE).astype(BF16)

    @pl.when(j == COL_KA // PROJ_TN)
    def _():
        r = rope128(acc)
        store_heads(ka_ref, r)
        slab_ref[...] = r.astype(BF16)

    @pl.when(j == COL_KB // PROJ_TN)
    def _():
        r = rope128(acc)
        store_heads(kb_ref, r)
        slab_ref[...] = r.astype(BF16)

    @pl.when(j == COL_VA // PROJ_TN)
    def _():
        store_heads(va_ref, acc)
        slab_ref[...] = acc.astype(BF16)

    @pl.when(j == COL_VB // PROJ_TN)
    def _():
        store_heads(vb_ref, acc)
        slab_ref[...] = acc.astype(BF16)

    @pl.when((j == COL_QI // PROJ_TN) | (j == COL_QI // PROJ_TN + 1))
    def _():
        slab_ref[...] = _rope(acc, c64_ref[...], s64_ref[...], IDX_DIM // 8, IDX_DIM).astype(BF16)

    @pl.when(j == COL_MISC // PROJ_TN)
    def _():
        y = acc[:, :LANES]
        lane = lax.broadcasted_iota(I32, y.shape, 1)
        is_ki = lane < IDX_DIM
        mu = jnp.sum(jnp.where(is_ki, y, 0.0), axis=-1, keepdims=True) * (1.0 / IDX_DIM)
        xc = y - mu
        var = jnp.sum(jnp.where(is_ki, xc * xc, 0.0), axis=-1, keepdims=True) * (1.0 / IDX_DIM)
        ln = xc * lax.rsqrt(var + NORM_EPS) * gk_ref[...] + bk_ref[...]
        ki = _rope(ln, c64_ref[...], s64_ref[...], IDX_DIM // 8, IDX_DIM)
        wi = y * (IDX_HEADS ** -0.5 * IDX_DIM ** -0.5)
        out = jnp.where(is_ki, ki, wi)
        misc_ref[...] = out
        miscb_ref[...] = out.astype(BF16)
        slab_ref[...] = jnp.concatenate(
            [out.astype(BF16), jnp.zeros((out.shape[0], PROJ_TN - LANES), BF16)], axis=1)


def _proj(x, g, w_perm, tabs, gk, bk, tm):
    rows = x.shape[0]
    c128, s128, c64, s64 = tabs
    n_col = SLAB_COLS // PROJ_TN
    row_spec = lambda w: pl.BlockSpec((tm, w), lambda i, j: (i, 0))
    vec_spec = lambda w: pl.BlockSpec((1, w), lambda i, j: (0, 0))
    head_rows_spec = pl.BlockSpec((tm * A_KV_HEADS, HEAD_DIM), lambda i, j: (i, 0))
    head_rows = jax.ShapeDtypeStruct((rows * A_KV_HEADS, HEAD_DIM), F32)
    return pl.pallas_call(
        _proj_kernel,
        grid=(rows // tm, n_col),
        in_specs=[row_spec(D_MODEL), vec_spec(D_MODEL),
                  pl.BlockSpec((D_MODEL, PROJ_TN), lambda i, j: (0, j)),
                  row_spec(LANES), row_spec(LANES), row_spec(LANES), row_spec(LANES),
                  vec_spec(LANES), vec_spec(LANES)],
        out_specs=[pl.BlockSpec((tm, PROJ_TN), lambda i, j: (i, j)),
                   head_rows_spec, head_rows_spec, head_rows_spec, head_rows_spec,
                   row_spec(LANES), row_spec(LANES)],
        out_shape=[jax.ShapeDtypeStruct((rows, SLAB_COLS), BF16),
                   head_rows, head_rows, head_rows, head_rows,
                   jax.ShapeDtypeStruct((rows, LANES), F32),
                   jax.ShapeDtypeStruct((rows, LANES), BF16)],
        scratch_shapes=[pltpu.VMEM((tm, D_MODEL), BF16)],
        name="proj",
        compiler_params=pltpu.CompilerParams(
            dimension_semantics=("parallel", "arbitrary"), vmem_limit_bytes=VMEM_LIMIT),
    )(x, g, w_perm, c128, s128, c64, s64, gk, bk)


def _rope_tables(pos):
    posf = pos.astype(F32)[:, None]

    def tables(dh):
        rot = dh // 4
        half = rot // 2
        inv = jnp.power(jnp.float32(ROPE_THETA), -jnp.arange(half, dtype=F32) * (2.0 / rot))
        ang = posf * inv[None, :]
        cos, sin = jnp.cos(ang), jnp.sin(ang)
        ones = jnp.ones((pos.shape[0], dh - rot), F32)
        c = jnp.concatenate([cos, cos, ones], axis=1)
        s = jnp.concatenate([-sin, sin, 0.0 * ones], axis=1)
        reps = LANES // dh
        return jnp.tile(c, (1, reps)), jnp.tile(s, (1, reps))

    c128, s128 = tables(HEAD_DIM)
    c64, s64 = tables(IDX_DIM)
    return c128, s128, c64, s64


def _permute_w_in(w_in):
    sizes = (A_Q_COLS, A_KV_COLS, A_KV_COLS, I_Q_COLS, IDX_HEADS, IDX_DIM, B_Q_COLS, B_KV_COLS, B_KV_COLS)
    offs = [0]
    for s in sizes:
        offs.append(offs[-1] + s)
    qa, ka, va, qi, wi, ki, qb, kb, vb = [w_in[:, offs[n]:offs[n + 1]] for n in range(len(sizes))]
    pad = jnp.zeros((w_in.shape[0], PROJ_TN - IDX_DIM - IDX_HEADS), w_in.dtype)
    return jnp.concatenate([qa, ka, va, qi, qb, kb, vb, ki, wi, pad], axis=1).astype(BF16)


TOPK_GROUPS = 256


def _sort_key(score):
    score = jnp.where(score == 0.0, 0.0, score)
    bits = pltpu.bitcast(score, I32)
    return bits ^ ((bits >> 31) & 0x7FFFFFFF)


def _fold_lanes(op, x, width):
    parts = [x[:, u * width:(u + 1) * width] for u in range(x.shape[1] // width)]
    while len(parts) > 1:
        nxt = [op(parts[u], parts[u + 1]) for u in range(0, len(parts) - 1, 2)]
        if len(parts) % 2:
            nxt.append(parts[-1])
        parts = nxt
    return parts[0]


def _lane_allreduce(op, x):
    shift = LANES // 2
    while shift >= 1:
        x = op(x, pltpu.roll(x, shift, axis=1))
        shift //= 2
    return x


def _count(keys_ref, nch, cw, pred):
    rows = keys_ref.shape[0]

    def body(c, acc):
        off = pl.multiple_of(c * cw, cw)
        blk = keys_ref[:, pl.ds(off, cw)]
        idx = off + lax.broadcasted_iota(I32, blk.shape, 1)
        return acc + _fold_lanes(jnp.add, jnp.where(pred(blk, idx), 1.0, 0.0), LANES)

    acc = lax.fori_loop(0, nch, body, jnp.zeros((rows, LANES), F32))
    return jnp.sum(acc, axis=1, keepdims=True)


def _row_bounds(keys_ref, nch, cw):
    rows = keys_ref.shape[0]

    def body(c, gmax):
        off = pl.multiple_of(c * cw, cw)
        return jnp.maximum(gmax, _fold_lanes(jnp.maximum, keys_ref[:, pl.ds(off, cw)], TOPK_GROUPS))

    gmax = lax.fori_loop(0, nch, body, jnp.full((rows, TOPK_GROUPS), INT_MIN, I32))
    lo = _lane_allreduce(jnp.minimum, _fold_lanes(jnp.minimum, gmax, LANES))[:, :1]
    top = _lane_allreduce(jnp.maximum, _fold_lanes(jnp.maximum, gmax, LANES))[:, :1]
    return lo, top


def _select_threshold(keys_ref, nch, cw, topk, idx_bits):
    rows = keys_ref.shape[0]
    kf = float(topk)
    lo, top = _row_bounds(keys_ref, nch, cw)
    hi = top + 1

    def n_open(lo, hi):
        return jnp.max(jnp.where(hi - 1 > lo, 1.0, 0.0))

    def cond(st):
        return (st[2] > 0.5) & (st[3] < 40)

    def body(st):
        lo, hi, _, it = st
        mid = (lo & hi) + ((lo ^ hi) >> 1)
        cnt = _count(keys_ref, nch, cw, lambda k, i: k >= mid)
        ge = cnt >= kf
        lo = jnp.where(ge, mid, lo)
        hi = jnp.where(cnt == kf, mid + 1, jnp.where(ge, hi, mid))
        return lo, hi, n_open(lo, hi), it + 1

    tau, _, _, _ = lax.while_loop(cond, body, (lo, hi, n_open(lo, hi), jnp.int32(0)))
    return tau, _resolve_ties(keys_ref, nch, cw, tau, topk, idx_bits)


def _resolve_ties(keys_ref, nch, cw, tau, topk, idx_bits):
    rows = keys_ref.shape[0]
    kf = float(topk)
    n_ge = _count(keys_ref, nch, cw, lambda k, i: k >= tau)
    has_k = tau > INT_MIN
    tied = has_k & (n_ge > kf)
    any_tied = jnp.max(jnp.where(tied, 1.0, 0.0)) > 0.5

    def resolve(_):
        need = kf - _count(keys_ref, nch, cw, lambda k, i: k > tau)
        lim = jnp.zeros((rows, 1), I32)
        for b in range(idx_bits - 1, -1, -1):
            cand = lim + (1 << b)
            cnt = _count(keys_ref, nch, cw, lambda k, i: (k == tau) & (i < cand))
            lim = jnp.where(cnt < need, cand, lim)
        return lim

    lim = lax.cond(any_tied, resolve, lambda _: jnp.zeros((rows, 1), I32), 0)
    return jnp.where(tied, lim, jnp.where(has_k, jnp.int32(2 ** 31 - 1), jnp.int32(-1)))


def _search_trip(keys_ref, lo_scr, hi_scr, cnt_scr, st_ref, cw, topk):
    rows = keys_ref.shape[0]
    kf = float(topk)
    t = st_ref[0]
    lo, hi = lo_scr[...], hi_scr[...]
    mid = (lo & hi) + ((lo ^ hi) >> 1)
    blk = keys_ref[:, pl.ds(pl.multiple_of(t * cw, cw), cw)]
    acc = cnt_scr[...] + _fold_lanes(jnp.add, jnp.where(blk >= mid, 1.0, 0.0), LANES)
    cnt = jnp.sum(acc, axis=1, keepdims=True)
    last = (jnp.zeros((rows, 1), I32) + (t + 1)) == st_ref[1]
    ge = cnt >= kf
    lo_scr[...] = jnp.where(last & ge, mid, lo)
    hi_scr[...] = jnp.where(last, jnp.where(cnt == kf, mid + 1, jnp.where(ge, hi, mid)), hi)
    cnt_scr[...] = jnp.where(last, 0.0, acc)
    st_ref[0] = jnp.where(t + 1 == st_ref[1], 0, t + 1)


def _keys_to_bias(keys_ref, bias_ref, nch, cw, tau, jlim):
    def body(c, carry):
        off = pl.multiple_of(c * cw, cw)
        key = keys_ref[:, pl.ds(off, cw)]
        idx = off + lax.broadcasted_iota(I32, key.shape, 1)
        sel = (key > tau) | ((key == tau) & (idx <= jlim))
        bias_ref[:, pl.ds(off, cw)] = pltpu.bitcast(jnp.where(sel, 0.0, NEG), I32)
        return carry

    lax.fori_loop(0, nch, body, 0)


def _flash_update(sl, d, v, m_scr, acc_scr):
    reps = d.shape[1] // LANES
    m_old = m_scr[sl]
    m_new = jnp.maximum(m_old, jnp.max(d, axis=1, keepdims=True))
    alpha = jnp.exp2(m_old - m_new)
    p = jnp.exp2(d - jnp.tile(m_new, (1, reps)))
    v_ones = jnp.concatenate([v, jnp.ones(v.shape, BF16)], axis=1)
    acc_scr[sl] = jnp.tile(alpha, (1, 2)) * acc_scr[sl] + jnp.dot(p.astype(BF16), v_ones,
                                                                   preferred_element_type=F32)
    m_scr[sl] = m_new


def _flash_init(m_scr, acc_scr):
    m_scr[...] = jnp.full(m_scr.shape, NEG, F32)
    acc_scr[...] = jnp.zeros(acc_scr.shape, F32)


def _flash_result(acc, guard=False):
    l = acc[:, HEAD_DIM:]
    return acc[:, :HEAD_DIM] / (jnp.maximum(l, 1e-30) if guard else l)


def _causal_steps(n_q, chunks_of):
    qb, cb = [], []
    for i in range(n_q):
        for c in range(chunks_of(i)):
            qb.append(i)
            cb.append(c)
    return jnp.asarray(qb, I32), jnp.asarray(cb, I32)


IDX_TN = 512


def _dsa_prompt_kernel(qb_ref, cb_ref, qi_ref, misc_ref, qin_ref, miscn_ref, kit_ref, qa_ref, k_ref, v_ref, o_ref,
                       keys_scr, qih_scr, wib_scr, d_scr, m_scr, acc_scr, lo_scr, hi_scr, cnt_scr, st_ref,
                       *, tq, tk, topk, idx_bits, n_q, trips):
    step = pl.program_id(0)
    i = qb_ref[step]
    c = cb_ref[step]
    nch = (i * tq + tq + tk - 1) // tk
    rt = 32
    n_chunks = keys_scr.shape[2] // tk
    scan = max(w for w in (1, 2, 4) if n_chunks % w == 0)
    cw = scan * tk
    cur = keys_scr.at[i % 2]
    nxt = keys_scr.at[(i + 1) % 2]

    def index_block(q_ref, w_ref, blk, keys):
        t0 = blk * tq
        n_att = (t0 + tq + tk - 1) // tk
        for hh in range(IDX_HEADS):
            qih_scr[hh] = q_ref[:, hh * IDX_DIM:(hh + 1) * IDX_DIM]
            wib_scr[hh] = jnp.broadcast_to(w_ref[:, MISC_WI + hh:MISC_WI + hh + 1], (tq, LANES))
        qpos = t0 + lax.broadcasted_iota(I32, (tq, 1), 0)

        def sub_body(kc, carry):
            off = pl.multiple_of(kc * IDX_TN, IDX_TN)
            d_scr[...] = jnp.dot(qih_scr[...].reshape(IDX_HEADS * tq, IDX_DIM),
                                 kit_ref[:, pl.ds(off, IDX_TN)], preferred_element_type=F32)
            kpos = off + lax.broadcasted_iota(I32, (rt, IDX_TN), 1)
            for r in range(tq // rt):
                score = jnp.zeros((rt, IDX_TN), F32)
                for hh in range(IDX_HEADS):
                    w = wib_scr[hh, r * rt:(r + 1) * rt, :]
                    dd = d_scr[hh * tq + r * rt:hh * tq + (r + 1) * rt, :]
                    score = score + jnp.tile(w, (1, IDX_TN // LANES)) * jnp.maximum(dd, 0.0)
                keys[r * rt:(r + 1) * rt, pl.ds(off, IDX_TN)] = jnp.where(
                    kpos <= qpos[r * rt:(r + 1) * rt], _sort_key(score), INT_MIN)
            return carry

        lax.fori_loop(0, n_att * (tk // IDX_TN), sub_body, 0)
        n_scan = (n_att + scan - 1) // scan

        def pad_body(kc, carry):
            keys[:, pl.ds(pl.multiple_of(kc * tk, tk), tk)] = jnp.full((tq, tk), INT_MIN, I32)
            return carry

        lax.fori_loop(n_att, n_scan * scan, pad_body, 0)
        return n_scan

    @pl.when(step == 0)
    def _():
        n_scan = index_block(qi_ref, misc_ref, i, cur)
        tau, jl = _select_threshold(cur, n_scan, cw, topk, idx_bits)
        _keys_to_bias(cur, cur, n_scan, cw, tau, jl)

    @pl.when(c == 0)
    def _():
        _flash_init(m_scr, acc_scr)
        st_ref[0] = 0
        st_ref[1] = 1
        st_ref[2] = 1
        lo_scr[...] = jnp.zeros(lo_scr.shape, I32)
        hi_scr[...] = jnp.zeros(hi_scr.shape, I32)
        cnt_scr[...] = jnp.zeros(cnt_scr.shape, F32)

        @pl.when(i + 1 < n_q)
        def _():
            n_scan = index_block(qin_ref, miscn_ref, i + 1, nxt)
            lo, top = _row_bounds(nxt, n_scan, cw)
            lo_scr[...] = lo
            hi_scr[...] = top + 1
            st_ref[1] = n_scan
            st_ref[2] = n_scan

    bias = pltpu.bitcast(cur[:, pl.ds(pl.multiple_of(c * tk, tk), tk)], F32)
    group = A_HEADS // A_KV_HEADS

    def scores(h):
        kv = h // group
        return _nt_dot(qa_ref[:, h * HEAD_DIM:(h + 1) * HEAD_DIM],
                       k_ref[:, kv * HEAD_DIM:(kv + 1) * HEAD_DIM]) + bias

    def search_trip():
        _search_trip(nxt, lo_scr, hi_scr, cnt_scr, st_ref, cw, topk)

    d_next = scores(0)
    done = 0
    for h in range(A_HEADS):
        d = d_next
        if h + 1 < A_HEADS:
            d_next = scores(h + 1)
        kv = h // group
        _flash_update(h, d, v_ref[:, kv * HEAD_DIM:(kv + 1) * HEAD_DIM], m_scr, acc_scr)
        while done * A_HEADS < trips * (h + 1):
            search_trip()
            done += 1

    @pl.when(c == nch - 1)
    def _():
        for h in range(A_HEADS):
            o_ref[:, h * HEAD_DIM:(h + 1) * HEAD_DIM] = _flash_result(acc_scr[h]).astype(o_ref.dtype)

        @pl.when(i + 1 < n_q)
        def _():
            n_scan = st_ref[2]

            def n_open():
                return jnp.max(jnp.where(hi_scr[...] - 1 > lo_scr[...], 1.0, 0.0))

            def body(st):
                search_trip()
                return n_open(), st[1] + 1

            lax.while_loop(lambda st: (st[0] > 0.5) & (st[1] < 40 * n_scan), body, (n_open(), jnp.int32(0)))
            tau = lo_scr[...]
            jl = _resolve_ties(nxt, n_scan, cw, tau, topk, idx_bits)
            _keys_to_bias(nxt, nxt, n_scan, cw, tau, jl)


def _dsa_prompt(slab, misc, kit, tq, tk, trips):
    rows = slab.shape[0]
    n_q = rows // tq
    topk = min(DSA_TOPK, rows // 4)
    qb, cb = _causal_steps(n_q, lambda i: (i * tq + tq + tk - 1) // tk)
    kern = functools.partial(_dsa_prompt_kernel, tq=tq, tk=tk, topk=topk,
                             idx_bits=max(1, (rows - 1).bit_length()), n_q=n_q, trips=trips)
    nxt_blk = lambda s, qb: jnp.minimum(qb[s] + 1, n_q - 1)
    grid_spec = pltpu.PrefetchScalarGridSpec(
        num_scalar_prefetch=2,
        grid=(int(qb.shape[0]),),
        in_specs=[pl.BlockSpec((tq, I_Q_COLS), lambda s, qb, cb: (qb[s], COL_QI // I_Q_COLS)),
                  pl.BlockSpec((tq, LANES), lambda s, qb, cb: (qb[s], 0)),
                  pl.BlockSpec((tq, I_Q_COLS), lambda s, qb, cb: (nxt_blk(s, qb), COL_QI // I_Q_COLS)),
                  pl.BlockSpec((tq, LANES), lambda s, qb, cb: (nxt_blk(s, qb), 0)),
                  pl.BlockSpec((IDX_DIM, rows), lambda s, qb, cb: (0, 0)),
                  pl.BlockSpec((tq, A_Q_COLS), lambda s, qb, cb: (qb[s], COL_QA // A_Q_COLS)),
                  pl.BlockSpec((tk, A_KV_COLS), lambda s, qb, cb: (cb[s], COL_KA // A_KV_COLS)),
                  pl.BlockSpec((tk, A_KV_COLS), lambda s, qb, cb: (cb[s], COL_VA // A_KV_COLS))],
        out_specs=pl.BlockSpec((tq, A_Q_COLS), lambda s, qb, cb: (qb[s], 0)),
        scratch_shapes=[pltpu.VMEM((2, tq, rows), I32),
                        pltpu.VMEM((IDX_HEADS, tq, IDX_DIM), BF16),
                        pltpu.VMEM((IDX_HEADS, tq, LANES), F32),
                        pltpu.VMEM((IDX_HEADS * tq, IDX_TN), F32),
                        pltpu.VMEM((A_HEADS, tq, LANES), F32),
                        pltpu.VMEM((A_HEADS, tq, 2 * HEAD_DIM), F32),
                        pltpu.VMEM((tq, 1), I32), pltpu.VMEM((tq, 1), I32),
                        pltpu.VMEM((tq, LANES), F32),
                        pltpu.SMEM((4,), I32)])
    return pl.pallas_call(
        kern, grid_spec=grid_spec, name="dsa_prompt",
        out_shape=jax.ShapeDtypeStruct((rows, A_Q_COLS), BF16),
        compiler_params=pltpu.CompilerParams(
            dimension_semantics=("arbitrary",), vmem_limit_bytes=VMEM_LIMIT),
    )(qb, cb, slab, misc, slab, misc, kit, slab, slab, slab)


def _gate_topk(gate, n_valid, ksel):
    lane = lax.broadcasted_iota(I32, gate.shape, 1)
    lanef = lane.astype(F32)
    gate = jnp.where(lane < n_valid, gate, NEG)
    sel = jnp.zeros(gate.shape, F32)
    for _ in range(ksel):
        mx = jnp.max(gate, axis=1, keepdims=True)
        first = jnp.min(jnp.where(gate == mx, lanef, 1e9), axis=1, keepdims=True)
        hit = (lanef == first) & (mx > 0.5 * NEG)
        sel = jnp.where(hit, 1.0, sel)
        gate = jnp.where(hit, NEG, gate)
    return sel


def _block_bias_rows(sel):
    return ((sel - 1.0) * (-NEG)).astype(BF16)


def _block_expansion(n_keys, blk0, keys_on_rows):
    shape = (n_keys, LANES) if keys_on_rows else (LANES, n_keys)
    kdim, bdim = (0, 1) if keys_on_rows else (1, 0)
    kblk = lax.broadcasted_iota(I32, shape, kdim) // MOBA_BLOCK + blk0
    return jnp.where(kblk == lax.broadcasted_iota(I32, shape, bdim), 1.0, 0.0).astype(BF16)


def _block_means_kernel(k_ref, o_ref):
    x = k_ref[...]
    s8 = jnp.sum(x.reshape(x.shape[0] // SUBLANES, SUBLANES, HEAD_DIM), axis=0)
    o_ref[0] = (s8[:B_KV_HEADS] + s8[B_KV_HEADS:]) * (1.0 / MOBA_BLOCK)


def _block_means(kb_rows):
    blk_rows = MOBA_BLOCK * B_KV_HEADS
    nb = kb_rows.shape[0] // blk_rows
    out = pl.pallas_call(
        _block_means_kernel,
        grid=(nb,),
        in_specs=[pl.BlockSpec((blk_rows, HEAD_DIM), lambda n: (n, 0))],
        out_specs=pl.BlockSpec((1, B_KV_HEADS, HEAD_DIM), lambda n: (n, 0, 0)),
        out_shape=jax.ShapeDtypeStruct((nb, B_KV_HEADS, HEAD_DIM), F32),
        name="block_means",
    )(kb_rows)
    return out.reshape(nb, B_KV_COLS)


def _moba_prompt_kernel(qb_ref, cb_ref, q_ref, means_ref, k_ref, v_ref, o_ref,
                        qaug_scr, m_scr, acc_scr, *, tq, tk, ksel):
    step = pl.program_id(0)
    i = qb_ref[step]
    c = cb_ref[step]
    t0 = i * tq
    qblk = t0 // MOBA_BLOCK
    bpc = tk // MOBA_BLOCK
    nch = qblk // bpc + 1
    group = B_HEADS // B_KV_HEADS

    @pl.when(c == 0)
    def _():
        lane = lax.broadcasted_iota(I32, (tq, LANES), 1)
        for h in range(B_HEADS):
            kv = h // group
            q = q_ref[:, h * HEAD_DIM:(h + 1) * HEAD_DIM]
            gate = _nt_dot(q, means_ref[:, kv * HEAD_DIM:(kv + 1) * HEAD_DIM])
            sel = jnp.where(lane == qblk, 1.0, _gate_topk(gate, qblk, ksel))
            qaug_scr[h] = jnp.concatenate([q, _block_bias_rows(sel)], axis=1)
        _flash_init(m_scr, acc_scr)

    e_t = _block_expansion(tk, c * bpc, keys_on_rows=True)
    qlim = t0 + lax.broadcasted_iota(I32, (tq, 1), 0) + jnp.where(c == nch - 1, 0, 2 ** 30)
    kpos = c * tk + lax.broadcasted_iota(I32, (tq, tk), 1)
    causal = jnp.where(kpos <= qlim, 0.0, NEG)
    for kv in range(B_KV_HEADS):
        rhs = jnp.concatenate([k_ref[:, kv * HEAD_DIM:(kv + 1) * HEAD_DIM], e_t], axis=1)
        vv = v_ref[:, kv * HEAD_DIM:(kv + 1) * HEAD_DIM]
        for g in range(group):
            h = kv * group + g
            _flash_update(h, _nt_dot(qaug_scr[h], rhs) + causal, vv, m_scr, acc_scr)

    @pl.when(c == nch - 1)
    def _():
        for h in range(B_HEADS):
            o_ref[:, h * HEAD_DIM:(h + 1) * HEAD_DIM] = _flash_result(acc_scr[h]).astype(o_ref.dtype)


def _moba_prompt(slab, means_pad, tq, tk):
    rows = slab.shape[0]
    ksel = min(MOBA_TOPK, (rows - 1) // MOBA_BLOCK)
    assert tq == MOBA_BLOCK and tk % MOBA_BLOCK == 0 and means_pad.shape[0] == LANES
    qb, cb = _causal_steps(rows // tq, lambda i: (i * tq) // tk + 1)
    kern = functools.partial(_moba_prompt_kernel, tq=tq, tk=tk, ksel=ksel)
    grid_spec = pltpu.PrefetchScalarGridSpec(
        num_scalar_prefetch=2,
        grid=(int(qb.shape[0]),),
        in_specs=[pl.BlockSpec((tq, B_Q_COLS), lambda s, qb, cb: (qb[s], COL_QB // B_Q_COLS)),
                  pl.BlockSpec((LANES, B_KV_COLS), lambda s, qb, cb: (0, 0)),
                  pl.BlockSpec((tk, B_KV_COLS), lambda s, qb, cb: (cb[s], COL_KB // B_KV_COLS)),
                  pl.BlockSpec((tk, B_KV_COLS), lambda s, qb, cb: (cb[s], COL_VB // B_KV_COLS))],
        out_specs=pl.BlockSpec((tq, B_Q_COLS), lambda s, qb, cb: (qb[s], 0)),
        scratch_shapes=[pltpu.VMEM((B_HEADS, tq, 2 * HEAD_DIM), BF16),
                        pltpu.VMEM((B_HEADS, tq, LANES), F32),
                        pltpu.VMEM((B_HEADS, tq, 2 * HEAD_DIM), F32)])
    return pl.pallas_call(
        kern, grid_spec=grid_spec, name="moba_prompt",
        out_shape=jax.ShapeDtypeStruct((rows, B_Q_COLS), BF16),
        compiler_params=pltpu.CompilerParams(
            dimension_semantics=("arbitrary",), vmem_limit_bytes=VMEM_LIMIT),
    )(qb, cb, slab, means_pad, slab, slab)


TPAD = SUBLANES
SROWS = 8 * TPAD
KV_ROWS = SROWS // 4
PAGE_ROWS = PAGE_SIZE * 4


def _page_head(ref, kv):
    return ref[0, pl.ds(kv, PAGE_SIZE, stride=4), :]


def _paged_spec(shape, n_pages, per_step, first_step, n_steps, j):
    def index(b, s, pt):
        local = jnp.clip(s - first_step, 0, n_steps - 1)
        return (pt[b * n_pages + local * per_step + j], 0, 0)
    return pl.BlockSpec(shape, index)


def _dsa_sample_index_kernel(pt_ref, qi_ref, wi_ref, *rest, n_pages, pa, t_new):
    kidx_refs = rest[:pa]
    kin_ref, keys_ref, wib_scr = rest[pa:]
    keys_scr = keys_ref.at[0]
    s = pl.program_id(1)
    na = n_pages // pa
    past = n_pages * PAGE_SIZE
    ntot = keys_scr.shape[1]
    trow = lax.broadcasted_iota(I32, (TPAD, 1), 0)

    def index_keys(kd, off):
        width = kd.shape[1]
        x = jnp.tile(wib_scr[...], (1, width // LANES)) * jnp.maximum(
            jnp.dot(qi_ref[0], kd, preferred_element_type=F32), 0.0)
        score = x[:TPAD]
        for hh in range(1, IDX_HEADS):
            score = score + x[hh * TPAD:(hh + 1) * TPAD]
        kpos = off + lax.broadcasted_iota(I32, (TPAD, width), 1)
        visible = (kpos <= past + trow) & (trow < t_new)
        keys_scr[:, pl.ds(off, width)] = jnp.where(visible, _sort_key(score), INT_MIN)

    @pl.when(s == 0)
    def _():
        wib_scr[...] = jnp.broadcast_to(wi_ref[0], wib_scr.shape)
        keys_scr[:, past:] = jnp.full((TPAD, ntot - past), INT_MIN, I32)

    @pl.when(s < na)
    def _():
        kd = jnp.concatenate([r[0].astype(BF16) for r in kidx_refs], axis=1)
        index_keys(kd, pl.multiple_of(s * (pa * PAGE_SIZE), pa * PAGE_SIZE))

    @pl.when(s == na - 1)
    def _():
        index_keys(kin_ref[0], past)


def _topk_bias_kernel(keys_ref, bias_ref, *, cw, topk, idx_bits):
    nch = keys_ref.shape[1] // cw
    tau, jl = _select_threshold(keys_ref, nch, cw, topk, idx_bits)
    _keys_to_bias(keys_ref, bias_ref, nch, cw, tau, jl)


def _dsa_sample_attend_kernel(pt_ref, bias_ref, qa_ref, *rest, n_pages, pb):
    k_refs, v_refs = rest[:pb], rest[pb:2 * pb]
    kn_ref, vn_ref, o_ref, m_scr, acc_scr = rest[2 * pb:]
    keys_scr = bias_ref.at[0]
    s = pl.program_id(1)
    past = n_pages * PAGE_SIZE

    @pl.when(s == 0)
    def _():
        _flash_init(m_scr, acc_scr)

    def attend(key_of, val_of, off, width):
        bias = pltpu.bitcast(keys_scr[:, pl.ds(off, width)], F32)
        bias = jnp.concatenate([bias] * (KV_ROWS // TPAD), axis=0)
        for kv in range(A_KV_HEADS):
            rows = slice(kv * KV_ROWS, (kv + 1) * KV_ROWS)
            _flash_update(rows, _nt_dot(qa_ref[0, rows], key_of(kv)) + bias, val_of(kv), m_scr, acc_scr)

    def gather(refs, kv):
        return jnp.concatenate([_page_head(r, kv) for r in refs], axis=0).astype(BF16)

    attend(lambda kv: gather(k_refs, kv), lambda kv: gather(v_refs, kv),
           pl.multiple_of(s * (pb * PAGE_SIZE), pb * PAGE_SIZE), pb * PAGE_SIZE)

    @pl.when(s == n_pages // pb - 1)
    def _():
        attend(lambda kv: kn_ref[0, kv], lambda kv: vn_ref[0, kv], past, PAGE_SIZE)
        o_ref[0] = _flash_result(acc_scr[...], guard=True)


def _dsa_sample(pt_flat, n_pages, t_new, qi_s, wi_s, kidx_pages, ki_new_t, qa_s, k_pages, v_pages, ka_new, va_new,
                pa, pb):
    ns = qi_s.shape[0]
    past = n_pages * PAGE_SIZE
    topk = min(DSA_TOPK, (past + t_new) // 4)
    cw = 1024
    ntot = -(-(past + PAGE_SIZE) // cw) * cw
    na, nb = n_pages // pa, n_pages // pb
    seq3 = lambda s1, s2: pl.BlockSpec((1, s1, s2), lambda b, s, pt: (b, 0, 0))
    seq4 = pl.BlockSpec((1, A_KV_HEADS, PAGE_SIZE, HEAD_DIM), lambda b, s, pt: (b, 0, 0, 0))
    params = pltpu.CompilerParams(dimension_semantics=("arbitrary", "arbitrary"), vmem_limit_bytes=VMEM_LIMIT)

    keys = pl.pallas_call(
        functools.partial(_dsa_sample_index_kernel, n_pages=n_pages, pa=pa, t_new=t_new),
        grid_spec=pltpu.PrefetchScalarGridSpec(
            num_scalar_prefetch=1,
            grid=(ns, na),
            in_specs=[seq3(IDX_HEADS * TPAD, IDX_DIM), seq3(IDX_HEADS * TPAD, 1)]
                     + [_paged_spec((1, IDX_DIM, PAGE_SIZE), n_pages, pa, 0, na, j) for j in range(pa)]
                     + [seq3(IDX_DIM, PAGE_SIZE)],
            out_specs=seq3(TPAD, ntot),
            scratch_shapes=[pltpu.VMEM((IDX_HEADS * TPAD, LANES), F32)]),
        out_shape=jax.ShapeDtypeStruct((ns, TPAD, ntot), I32),
        name="dsa_sample_index", compiler_params=params,
    )(pt_flat, qi_s, wi_s, *([kidx_pages] * pa), ki_new_t)

    rows = ns * t_new
    rows_pad = -(-rows // SUBLANES) * SUBLANES
    keys2d = _pad_rows(keys[:, :t_new].reshape(rows, ntot), 0, rows_pad)
    bias2d = pl.pallas_call(
        functools.partial(_topk_bias_kernel, cw=cw, topk=topk, idx_bits=max(1, (ntot - 1).bit_length())),
        out_shape=jax.ShapeDtypeStruct((rows_pad, ntot), I32),
        name="dsa_sample_select", compiler_params=pltpu.CompilerParams(vmem_limit_bytes=VMEM_LIMIT),
    )(keys2d)
    bias = _pad_rows(bias2d[:rows].reshape(ns, t_new, ntot), 1, TPAD)

    page_kv = lambda j: _paged_spec((1, PAGE_ROWS, HEAD_DIM), n_pages, pb, 0, nb, j)
    return pl.pallas_call(
        functools.partial(_dsa_sample_attend_kernel, n_pages=n_pages, pb=pb),
        grid_spec=pltpu.PrefetchScalarGridSpec(
            num_scalar_prefetch=1,
            grid=(ns, nb),
            in_specs=[seq3(TPAD, ntot), seq3(SROWS, HEAD_DIM)]
                     + [page_kv(j) for j in range(pb)] + [page_kv(j) for j in range(pb)]
                     + [seq4, seq4],
            out_specs=seq3(SROWS, HEAD_DIM),
            scratch_shapes=[pltpu.VMEM((SROWS, LANES), F32),
                            pltpu.VMEM((SROWS, 2 * HEAD_DIM), F32)]),
        out_shape=jax.ShapeDtypeStruct((ns, SROWS, HEAD_DIM), F32),
        name="dsa_sample_attend", compiler_params=params,
    )(pt_flat, bias, qa_s, *([k_pages] * pb), *([v_pages] * pb), ka_new, va_new)


def _moba_sample_kernel(pt_ref, q_ref, *rest, n_pages, pb, t_new, ksel):
    k_refs, v_refs = rest[:pb], rest[pb:2 * pb]
    kn_ref, vn_ref, o_ref, bsum_scr, s_scr, seln_scr, m_scr, acc_scr = rest[2 * pb:]
    s = pl.program_id(1)
    nst = n_pages // pb
    past = n_pages * PAGE_SIZE
    nbp = past // MOBA_BLOCK
    bps = pb * PAGE_SIZE // MOBA_BLOCK
    width = pb * PAGE_SIZE
    tok = lax.broadcasted_iota(I32, (KV_ROWS, 1), 0) % TPAD
    pad_bias = jnp.where(tok < t_new, 0.0, NEG)
    kv_rows = lambda kv: slice(kv * KV_ROWS, (kv + 1) * KV_ROWS)

    @pl.when(s == 0)
    def _():
        bsum_scr[...] = jnp.zeros(bsum_scr.shape, F32)

    @pl.when(s < nst)
    def _():
        off = pl.multiple_of(s * width, width)
        for kv in range(B_KV_HEADS):
            kf = jnp.concatenate([_page_head(r, kv) for r in k_refs], axis=0)
            bsum_scr[kv, pl.ds(pl.multiple_of(s * bps, bps), bps), :] = jnp.sum(
                kf.reshape(bps, MOBA_BLOCK, HEAD_DIM), axis=1)
            s_scr[kv_rows(kv), pl.ds(off, width)] = _nt_dot(q_ref[0, kv_rows(kv)], kf.astype(BF16))

    @pl.when(s == nst - 1)
    def _():
        for kv in range(B_KV_HEADS):
            q = q_ref[0, kv_rows(kv)]
            s_scr[kv_rows(kv), pl.ds(past, PAGE_SIZE)] = _nt_dot(q, kn_ref[0, kv])
            means = (bsum_scr[kv] * (1.0 / MOBA_BLOCK)).astype(BF16)
            seln_scr[kv_rows(kv)] = _block_bias_rows(_gate_topk(_nt_dot(q, means), nbp, ksel))
        _flash_init(m_scr, acc_scr)

    @pl.when(s >= nst)
    def _():
        sb = s - nst
        off = pl.multiple_of(sb * width, width)
        expand = _block_expansion(width, sb * bps, keys_on_rows=False)
        for kv in range(B_KV_HEADS):
            bias = jnp.dot(seln_scr[kv_rows(kv)], expand, preferred_element_type=F32) + pad_bias
            vv = jnp.concatenate([_page_head(r, kv) for r in v_refs], axis=0).astype(BF16)
            _flash_update(kv_rows(kv), s_scr[kv_rows(kv), pl.ds(off, width)] + bias, vv, m_scr, acc_scr)

    @pl.when(s == 2 * nst - 1)
    def _():
        kpos = past + lax.broadcasted_iota(I32, (KV_ROWS, PAGE_SIZE), 1)
        bias = jnp.where(kpos <= past + tok, 0.0, NEG) + pad_bias
        for kv in range(B_KV_HEADS):
            _flash_update(kv_rows(kv), s_scr[kv_rows(kv), pl.ds(past, PAGE_SIZE)] + bias, vn_ref[0, kv],
                          m_scr, acc_scr)
        o_ref[0] = _flash_result(acc_scr[...], guard=True)


def _moba_sample(pt_flat, n_pages, t_new, qb_s, k_pages, v_pages, kb_new, vb_new, pb):
    ns = qb_s.shape[0]
    past = n_pages * PAGE_SIZE
    nbp = past // MOBA_BLOCK
    assert nbp <= LANES and (pb * PAGE_SIZE // MOBA_BLOCK) % SUBLANES == 0
    ksel = min(MOBA_TOPK, nbp)
    nst = n_pages // pb
    seq3 = lambda s1, s2: pl.BlockSpec((1, s1, s2), lambda b, s, pt: (b, 0, 0))
    seq4 = pl.BlockSpec((1, B_KV_HEADS, PAGE_SIZE, HEAD_DIM), lambda b, s, pt: (b, 0, 0, 0))
    kern = functools.partial(_moba_sample_kernel, n_pages=n_pages, pb=pb, t_new=t_new, ksel=ksel)
    grid_spec = pltpu.PrefetchScalarGridSpec(
        num_scalar_prefetch=1,
        grid=(ns, 2 * nst),
        in_specs=[seq3(SROWS, HEAD_DIM)]
                 + [_paged_spec((1, PAGE_ROWS, HEAD_DIM), n_pages, pb, 0, nst, j) for j in range(pb)]
                 + [_paged_spec((1, PAGE_ROWS, HEAD_DIM), n_pages, pb, nst, nst, j) for j in range(pb)]
                 + [seq4, seq4],
        out_specs=seq3(SROWS, HEAD_DIM),
        scratch_shapes=[pltpu.VMEM((B_KV_HEADS, LANES, HEAD_DIM), F32),
                        pltpu.VMEM((SROWS, past + PAGE_SIZE), F32),
                        pltpu.VMEM((SROWS, LANES), BF16),
                        pltpu.VMEM((SROWS, LANES), F32),
                        pltpu.VMEM((SROWS, 2 * HEAD_DIM), F32)])
    return pl.pallas_call(
        kern, grid_spec=grid_spec, name="moba_sample",
        out_shape=jax.ShapeDtypeStruct((ns, SROWS, HEAD_DIM), F32),
        compiler_params=pltpu.CompilerParams(
            dimension_semantics=("arbitrary", "arbitrary"), vmem_limit_bytes=VMEM_LIMIT),
    )(pt_flat, qb_s, *([k_pages] * pb), *([v_pages] * pb), kb_new, vb_new)


def _memory_kv_kernel(mem_ref, g_ref, w_ref, k_ref, v_ref):
    h = _rms(mem_ref[...], g_ref[...]).astype(BF16)
    kv = jnp.dot(h, w_ref[...], preferred_element_type=F32)
    k_ref[...] = kv[:, :MEM_WIDTH]
    v_ref[...] = kv[:, MEM_WIDTH:]


def _memory_kv(mem, g, w_bf):
    m = mem.shape[0]
    return pl.pallas_call(
        _memory_kv_kernel,
        out_shape=[jax.ShapeDtypeStruct((m, MEM_WIDTH), F32), jax.ShapeDtypeStruct((m, MEM_WIDTH), F32)],
        name="memory_kv",
        compiler_params=pltpu.CompilerParams(vmem_limit_bytes=VMEM_LIMIT),
    )(mem, g, w_bf)


def _outproj_cross_kernel(x_ref, oa_ref, ob_ref, woa_ref, wob_ref, g_ref, wq_ref, mk_ref, mv_ref, wo_ref,
                          y_ref, *, rows_per_seq):
    x1 = x_ref[...] + jnp.dot(oa_ref[...], woa_ref[...], preferred_element_type=F32) \
        + jnp.dot(ob_ref[...], wob_ref[...], preferred_element_type=F32)
    hc = _rms(x1, g_ref[...]).astype(BF16)
    q = jnp.dot(hc, wq_ref[...], preferred_element_type=F32).astype(BF16)
    tm = q.shape[0]
    nk = mk_ref.shape[0]
    scale = MEM_HEAD_DIM ** -0.5
    if rows_per_seq is not None:
        rseq = (pl.program_id(0) * tm + lax.broadcasted_iota(I32, (tm, 1), 0)) // rows_per_seq
        kseq = lax.broadcasted_iota(I32, (tm, nk), 1) // MEM_TOKENS
        mask = kseq == rseq
    outs = []
    for h in range(MEM_HEADS):
        sl = slice(h * MEM_HEAD_DIM, (h + 1) * MEM_HEAD_DIM)
        s = _nt_dot(q[:, sl], mk_ref[:, sl]) * scale
        if rows_per_seq is not None:
            s = jnp.where(mask, s, NEG)
        m = jnp.max(s, axis=1, keepdims=True)
        e = jnp.exp(s - m)
        p = e / jnp.sum(e, axis=1, keepdims=True)
        outs.append(jnp.dot(p.astype(BF16), mv_ref[:, sl], preferred_element_type=F32))
    o = jnp.concatenate(outs, axis=1).astype(BF16)
    y_ref[...] = x1 + jnp.dot(o, wo_ref[...], preferred_element_type=F32)


def _outproj_cross(x, oa, ob, w_out_bf, g_cross, w_mq_bf, mk_bf, mv_bf, w_mo_bf, tm, rows_per_seq):
    rows = x.shape[0]
    nk = mk_bf.shape[0]
    row_spec = lambda w: pl.BlockSpec((tm, w), lambda i: (i, 0))
    full = lambda a, b: pl.BlockSpec((a, b), lambda i: (0, 0))
    kern = functools.partial(_outproj_cross_kernel, rows_per_seq=rows_per_seq)
    return pl.pallas_call(
        kern,
        grid=(rows // tm,),
        in_specs=[row_spec(D_MODEL), row_spec(A_Q_COLS), row_spec(B_Q_COLS),
                  pl.BlockSpec((A_Q_COLS, D_MODEL), lambda i: (0, 0)),
                  pl.BlockSpec((B_Q_COLS, D_MODEL), lambda i: (1, 0)),
                  full(1, D_MODEL), full(D_MODEL, MEM_WIDTH), full(nk, MEM_WIDTH), full(nk, MEM_WIDTH),
                  full(MEM_WIDTH, D_MODEL)],
        out_specs=row_spec(D_MODEL),
        out_shape=jax.ShapeDtypeStruct((rows, D_MODEL), F32),
        name="outproj_cross",
        compiler_params=pltpu.CompilerParams(
            dimension_semantics=("parallel",), vmem_limit_bytes=VMEM_LIMIT),
    )(x, oa, ob, w_out_bf, w_out_bf, g_cross, w_mq_bf, mk_bf, mv_bf, w_mo_bf)


def _mlp_final_kernel(x_ref, g_ref, wu_ref, wd_ref, gf_ref, y_ref, h_scr, acc_scr):
    f = pl.program_id(1)

    @pl.when(f == 0)
    def _():
        h_scr[...] = _rms(x_ref[...], g_ref[...]).astype(BF16)
        acc_scr[...] = jnp.zeros(acc_scr.shape, F32)

    u = jnp.maximum(jnp.dot(h_scr[...], wu_ref[...], preferred_element_type=F32), 0.0)
    acc_scr[...] += jnp.dot((u * u).astype(BF16), wd_ref[...], preferred_element_type=F32)

    @pl.when(f == pl.num_programs(1) - 1)
    def _():
        y_ref[...] = _rms(x_ref[...] + acc_scr[...], gf_ref[...])


def _mlp_final(x, g_ffn, w_up_bf, w_down_bf, g_final, tm, tf):
    rows = x.shape[0]
    d_ff = w_up_bf.shape[1]
    return pl.pallas_call(
        _mlp_final_kernel,
        grid=(rows // tm, d_ff // tf),
        in_specs=[pl.BlockSpec((tm, D_MODEL), lambda i, f: (i, 0)),
                  pl.BlockSpec((1, D_MODEL), lambda i, f: (0, 0)),
                  pl.BlockSpec((D_MODEL, tf), lambda i, f: (0, f)),
                  pl.BlockSpec((tf, D_MODEL), lambda i, f: (f, 0)),
                  pl.BlockSpec((1, D_MODEL), lambda i, f: (0, 0))],
        out_specs=pl.BlockSpec((tm, D_MODEL), lambda i, f: (i, 0)),
        out_shape=jax.ShapeDtypeStruct((rows, D_MODEL), F32),
        scratch_shapes=[pltpu.VMEM((tm, D_MODEL), BF16), pltpu.VMEM((tm, D_MODEL), F32)],
        name="mlp_final",
        compiler_params=pltpu.CompilerParams(
            dimension_semantics=("parallel", "arbitrary"), vmem_limit_bytes=VMEM_LIMIT),
    )(x, g_ffn, w_up_bf, w_down_bf, g_final)


def _row_tile(rows, want):
    t = min(rows, want)
    while rows % t:
        t //= 2
    return t


def _pad_rows(a, axis, size):
    pad = [(0, 0)] * a.ndim
    pad[axis] = (0, size - a.shape[axis])
    return jnp.pad(a, pad)


def _heads_first(a, ns, t_new, heads, dim, t_pad):
    a = a.reshape(ns, t_new, heads, dim).transpose(0, 2, 1, 3)
    return _pad_rows(a, 2, t_pad)


def kernel(x_prompt, x_sample, cache_k_a, cache_v_a, cache_kidx, cache_k_b, cache_v_b, cache_mem_k,
           cache_mem_v, page_table, mem_prompt, g_mix, w_in, g_kidx, b_kidx, w_out, g_cross, w_mq, g_mem,
           w_mkv, w_mo, g_ffn, w_up, w_down, g_final):
    batch, s_len, _ = x_prompt.shape
    ns, t_new, _ = x_sample.shape
    depth = w_in.shape[0]
    n_pages = page_table.shape[1]
    past = n_pages * PAGE_SIZE
    n_phys = cache_k_a.shape[1]
    assert batch == 1 and depth == 1
    assert s_len % 1024 == 0 and s_len // MOBA_BLOCK <= LANES
    assert past % MOBA_BLOCK == 0 and t_new <= TPAD
    l = 0
    row = lambda v: v.reshape(1, -1)

    w_in_p = _permute_w_in(w_in[l])
    gk = row(jnp.concatenate([g_kidx[l], jnp.zeros((LANES - IDX_DIM,), F32)]))
    bk = row(jnp.concatenate([b_kidx[l], jnp.zeros((LANES - IDX_DIM,), F32)]))
    w_out_bf = w_out[l].astype(BF16)
    w_mq_bf = w_mq[l].astype(BF16)
    w_mo_bf = w_mo[l].astype(BF16)
    w_up_bf = w_up[l].astype(BF16)
    w_down_bf = w_down[l].astype(BF16)

    xp = x_prompt.reshape(s_len, D_MODEL)
    tabs_p = _rope_tables(jnp.arange(s_len))
    slab, ka, va, kb, vb, misc, misc_bf = _proj(xp, row(g_mix[l]), w_in_p, tabs_p, gk, bk, _row_tile(s_len, 512))
    kit = misc_bf[:, :IDX_DIM].T
    oa = _dsa_prompt(slab, misc, kit, tq=128, tk=1024, trips=7)
    means_pad = _pad_rows(_block_means(kb).astype(BF16), 0, LANES)
    ob = _moba_prompt(slab, means_pad, tq=MOBA_BLOCK, tk=1024)
    mk, mv = _memory_kv(mem_prompt.reshape(MEM_TOKENS, D_MODEL), row(g_mem[l]), w_mkv[l].astype(BF16))
    xp2 = _outproj_cross(xp, oa, ob, w_out_bf, row(g_cross[l]), w_mq_bf, mk.astype(BF16), mv.astype(BF16),
                         w_mo_bf, _row_tile(s_len, 256), None)
    y_prompt = _mlp_final(xp2, row(g_ffn[l]), w_up_bf, w_down_bf, row(g_final), _row_tile(s_len, 512), 1024)

    rows_s = ns * t_new
    xs = x_sample.reshape(rows_s, D_MODEL)
    pos_s = jnp.tile(past + jnp.arange(t_new), ns)
    tabs_s = _rope_tables(pos_s)
    rows_pad = -(-rows_s // SUBLANES) * SUBLANES
    xs_pad = _pad_rows(xs, 0, rows_pad)
    tabs_s = tuple(_pad_rows(t, 0, rows_pad) for t in tabs_s)
    slab_s, ka_s, va_s, kb_s, vb_s, misc_s, _ = _proj(xs_pad, row(g_mix[l]), w_in_p, tabs_s, gk, bk,
                                                      _row_tile(rows_pad, 128))
    slab_s, misc_s = slab_s[:rows_s], misc_s[:rows_s]
    ka_s, va_s, kb_s, vb_s = [a[:rows_s * A_KV_HEADS] for a in (ka_s, va_s, kb_s, vb_s)]

    pt_flat = page_table.reshape(-1).astype(I32)
    index_pages, attend_pages = min(32, n_pages), min(16, n_pages)
    assert n_pages % index_pages == 0 and n_pages % attend_pages == 0
    page_view = lambda c: c[l].reshape(n_phys, PAGE_ROWS, HEAD_DIM)
    kidx_view = jnp.swapaxes(cache_kidx[l], 1, 2)
    cols = lambda c0, w: slab_s[:, c0:c0 + w]
    qi_s = _heads_first(cols(COL_QI, I_Q_COLS), ns, t_new, IDX_HEADS, IDX_DIM, TPAD)
    qi_s = qi_s.reshape(ns, IDX_HEADS * TPAD, IDX_DIM)
    wi_s = _heads_first(misc_s[:, MISC_WI:MISC_WI + IDX_HEADS], ns, t_new, IDX_HEADS, 1, TPAD)
    wi_s = wi_s.reshape(ns, IDX_HEADS * TPAD, 1)
    ki_new_t = _pad_rows(cols(COL_MISC, IDX_DIM).reshape(ns, t_new, IDX_DIM).transpose(0, 2, 1), 2, PAGE_SIZE)
    q_rows = lambda c0: _heads_first(cols(c0, A_Q_COLS), ns, t_new, A_HEADS, HEAD_DIM, TPAD).reshape(
        ns, SROWS, HEAD_DIM)
    new_kv = lambda c0: _heads_first(cols(c0, A_KV_COLS), ns, t_new, A_KV_HEADS, HEAD_DIM, PAGE_SIZE)
    oa_s = _dsa_sample(pt_flat, n_pages, t_new, qi_s, wi_s, kidx_view, ki_new_t, q_rows(COL_QA),
                       page_view(cache_k_a), page_view(cache_v_a), new_kv(COL_KA), new_kv(COL_VA),
                       index_pages, attend_pages)
    ob_s = _moba_sample(pt_flat, n_pages, t_new, q_rows(COL_QB), page_view(cache_k_b), page_view(cache_v_b),
                        new_kv(COL_KB), new_kv(COL_VB), attend_pages)
    tokens_first = lambda o: o.reshape(ns, A_HEADS, TPAD, HEAD_DIM)[:, :, :t_new].transpose(0, 2, 1, 3).reshape(
        rows_s, -1).astype(BF16)
    oa_s2 = _pad_rows(tokens_first(oa_s), 0, rows_pad)
    ob_s2 = _pad_rows(tokens_first(ob_s), 0, rows_pad)
    mk_s = cache_mem_k[l].reshape(ns * MEM_TOKENS, MEM_WIDTH).astype(BF16)
    mv_s = cache_mem_v[l].reshape(ns * MEM_TOKENS, MEM_WIDTH).astype(BF16)
    xs2 = _outproj_cross(xs_pad, oa_s2, ob_s2, w_out_bf, row(g_cross[l]), w_mq_bf, mk_s, mv_s, w_mo_bf,
                         rows_pad, t_new)
    y_sample = _mlp_final(xs2, row(g_ffn[l]), w_up_bf, w_down_bf, row(g_final), rows_pad, 512)[:rows_s]

    kv5 = lambda a, n, t: a.reshape(1, n, t, A_KV_HEADS, HEAD_DIM)
    return (y_prompt.reshape(batch, s_len, D_MODEL), y_sample.reshape(ns, t_new, D_MODEL),
            kv5(ka, batch, s_len), kv5(va, batch, s_len),
            misc[:, :IDX_DIM].reshape(1, batch, s_len, IDX_DIM),
            kv5(kb, batch, s_len), kv5(vb, batch, s_len),
            mk.reshape(1, batch, MEM_TOKENS, MEM_HEADS, MEM_HEAD_DIM),
            mv.reshape(1, batch, MEM_TOKENS, MEM_HEADS, MEM_HEAD_DIM),
            kv5(ka_s, ns, t_new), kv5(va_s, ns, t_new),
            misc_s[:, :IDX_DIM].reshape(1, ns, t_new, IDX_DIM),
            kv5(kb_s, ns, t_new), kv5(vb_s, ns, t_new))
```

```python
import functools

import jax
import jax.numpy as jnp
from jax import lax
from jax.experimental import pallas as pl
from jax.experimental.pallas import tpu as pltpu

F32 = jnp.float32
BF16 = jnp.bfloat16
I32 = jnp.int32

D_MODEL = 2048
HEAD_DIM = 128
A_HEADS = 8
A_KV_HEADS = 4
B_HEADS = 8
B_KV_HEADS = 4
IDX_HEADS = 16
IDX_DIM = 64
DSA_TOPK = 256
MOBA_BLOCK = 256
MOBA_TOPK = 3
MEM_TOKENS = 256
MEM_HEADS = 4
MEM_HEAD_DIM = 128
PAGE_SIZE = 128
ROPE_THETA = 500000.0
NORM_EPS = 1e-6

A_Q_COLS = A_HEADS * HEAD_DIM
A_KV_COLS = A_KV_HEADS * HEAD_DIM
I_Q_COLS = IDX_HEADS * IDX_DIM
B_Q_COLS = B_HEADS * HEAD_DIM
B_KV_COLS = B_KV_HEADS * HEAD_DIM
MEM_WIDTH = MEM_HEADS * MEM_HEAD_DIM

LANES = 128
SUBLANES = 8
VMEM_LIMIT = 56 * 1024 * 1024
INT_MIN = -(2 ** 31)
NEG = -1e30
LOG2E = 1.4426950408889634
SOFTMAX_EXP2_SCALE = HEAD_DIM ** -0.5 * LOG2E

PROJ_TN = 512
COL_QA, COL_KA, COL_VA, COL_QI, COL_QB, COL_KB, COL_VB, COL_MISC = 0, 1024, 1536, 2048, 3072, 4096, 4608, 5120
SLAB_COLS = COL_MISC + PROJ_TN
MISC_WI = IDX_DIM


def _nt_dot(a, b):
    return lax.dot_general(a, b, (((1,), (1,)), ((), ())), preferred_element_type=F32)


def _rms(x, g):
    return x * lax.rsqrt(jnp.mean(x * x, axis=-1, keepdims=True) + NORM_EPS) * g


def _rope(x, cos_t, sin_t, half, period):
    width = x.shape[1]
    reps = width // LANES
    c = jnp.concatenate([cos_t] * reps, axis=1) if reps > 1 else cos_t
    s = jnp.concatenate([sin_t] * reps, axis=1) if reps > 1 else sin_t
    lane = lax.broadcasted_iota(I32, x.shape, 1)
    first = (lane & (period - 1)) < half
    up = pltpu.roll(x, width - half, axis=1)
    dn = pltpu.roll(x, half, axis=1)
    return x * c + jnp.where(first, up, dn) * s


def _proj_kernel(x_ref, g_ref, w_ref, c128_ref, s128_ref, c64_ref, s64_ref, gk_ref, bk_ref,
                 slab_ref, ka_ref, va_ref, kb_ref, vb_ref, misc_ref, miscb_ref, h_scr, pend_scr):
    j = pl.program_id(1)
    jj = j - 1

    def step(epilogue):
        acc = jnp.dot(h_scr[...], w_ref[...], preferred_element_type=F32)
        if epilogue is not None:
            epilogue(pend_scr[...])
        pend_scr[...] = acc

    @pl.when(j == 0)
    def _():
        h_scr[...] = _rms(x_ref[...], g_ref[...]).astype(BF16)
        step(None)

    def rope128(v):
        return _rope(v, c128_ref[...], s128_ref[...], HEAD_DIM // 8, HEAD_DIM)

    def store_heads(ref, v):
        for kv in range(A_KV_HEADS):
            ref[pl.ds(kv, v.shape[0], stride=A_KV_HEADS), :] = v[:, kv * HEAD_DIM:(kv + 1) * HEAD_DIM]

    def finish_q128(acc):
        slab_ref[...] = (rope128(acc) * SOFTMAX_EXP2_SCALE).astype(BF16)

    def finish_k(ref):
        def fn(acc):
            r = rope128(acc)
            store_heads(ref, r)
            slab_ref[...] = r.astype(BF16)
        return fn

    def finish_v(ref):
        def fn(acc):
            store_heads(ref, acc)
            slab_ref[...] = acc.astype(BF16)
        return fn

    def finish_qi(acc):
        slab_ref[...] = _rope(acc, c64_ref[...], s64_ref[...], IDX_DIM // 8, IDX_DIM).astype(BF16)

    is_q128 = (jj == COL_QA // PROJ_TN) | (jj == COL_QA // PROJ_TN + 1) | \
              (jj == COL_QB // PROJ_TN) | (jj == COL_QB // PROJ_TN + 1)
    pl.when(is_q128)(lambda: step(finish_q128))
    pl.when(jj == COL_KA // PROJ_TN)(lambda: step(finish_k(ka_ref)))
    pl.when(jj == COL_KB // PROJ_TN)(lambda: step(finish_k(kb_ref)))
    pl.when(jj == COL_VA // PROJ_TN)(lambda: step(finish_v(va_ref)))
    pl.when(jj == COL_VB // PROJ_TN)(lambda: step(finish_v(vb_ref)))
    pl.when((jj == COL_QI // PROJ_TN) | (jj == COL_QI // PROJ_TN + 1))(lambda: step(finish_qi))

    @pl.when(jj == COL_MISC // PROJ_TN)
    def _():
        acc = pend_scr[...]
        y = acc[:, :LANES]
        lane = lax.broadcasted_iota(I32, y.shape, 1)
        is_ki = lane < IDX_DIM
        mu = jnp.sum(jnp.where(is_ki, y, 0.0), axis=-1, keepdims=True) * (1.0 / IDX_DIM)
        xc = y - mu
        var = jnp.sum(jnp.where(is_ki, xc * xc, 0.0), axis=-1, keepdims=True) * (1.0 / IDX_DIM)
        ln = xc * lax.rsqrt(var + NORM_EPS) * gk_ref[...] + bk_ref[...]
        ki = _rope(ln, c64_ref[...], s64_ref[...], IDX_DIM // 8, IDX_DIM)
        wi = y * (IDX_HEADS ** -0.5 * IDX_DIM ** -0.5)
        out = jnp.where(is_ki, ki, wi)
        misc_ref[...] = out
        miscb_ref[...] = out.astype(BF16)
        slab_ref[...] = jnp.concatenate(
            [out.astype(BF16), jnp.zeros((out.shape[0], PROJ_TN - LANES), BF16)], axis=1)


def _proj(x, g, w_perm, tabs, gk, bk, tm):
    rows = x.shape[0]
    c128, s128, c64, s64 = tabs
    n_col = SLAB_COLS // PROJ_TN
    row_spec = lambda w: pl.BlockSpec((tm, w), lambda i, j: (i, 0))
    vec_spec = lambda w: pl.BlockSpec((1, w), lambda i, j: (0, 0))
    head_rows_spec = pl.BlockSpec((tm * A_KV_HEADS, HEAD_DIM), lambda i, j: (i, 0))
    head_rows = jax.ShapeDtypeStruct((rows * A_KV_HEADS, HEAD_DIM), F32)
    return pl.pallas_call(
        _proj_kernel,
        grid=(rows // tm, n_col + 1),
        in_specs=[row_spec(D_MODEL), vec_spec(D_MODEL),
                  pl.BlockSpec((D_MODEL, PROJ_TN), lambda i, j: (0, jnp.minimum(j, n_col - 1))),
                  row_spec(LANES), row_spec(LANES), row_spec(LANES), row_spec(LANES),
                  vec_spec(LANES), vec_spec(LANES)],
        out_specs=[pl.BlockSpec((tm, PROJ_TN), lambda i, j: (i, jnp.maximum(j - 1, 0))),
                   head_rows_spec, head_rows_spec, head_rows_spec, head_rows_spec,
                   row_spec(LANES), row_spec(LANES)],
        out_shape=[jax.ShapeDtypeStruct((rows, SLAB_COLS), BF16),
                   head_rows, head_rows, head_rows, head_rows,
                   jax.ShapeDtypeStruct((rows, LANES), F32),
                   jax.ShapeDtypeStruct((rows, LANES), BF16)],
        scratch_shapes=[pltpu.VMEM((tm, D_MODEL), BF16), pltpu.VMEM((tm, PROJ_TN), F32)],
        name="proj",
        compiler_params=pltpu.CompilerParams(
            dimension_semantics=("parallel", "arbitrary"), vmem_limit_bytes=VMEM_LIMIT),
    )(x, g, w_perm, c128, s128, c64, s64, gk, bk)


def _rope_tables(pos):
    posf = pos.astype(F32)[:, None]

    def tables(dh):
        rot = dh // 4
        half = rot // 2
        inv = jnp.power(jnp.float32(ROPE_THETA), -jnp.arange(half, dtype=F32) * (2.0 / rot))
        ang = posf * inv[None, :]
        cos, sin = jnp.cos(ang), jnp.sin(ang)
        ones = jnp.ones((pos.shape[0], dh - rot), F32)
        c = jnp.concatenate([cos, cos, ones], axis=1)
        s = jnp.concatenate([-sin, sin, 0.0 * ones], axis=1)
        reps = LANES // dh
        return jnp.tile(c, (1, reps)), jnp.tile(s, (1, reps))

    c128, s128 = tables(HEAD_DIM)
    c64, s64 = tables(IDX_DIM)
    return c128, s128, c64, s64


def _permute_w_in(w_in):
    sizes = (A_Q_COLS, A_KV_COLS, A_KV_COLS, I_Q_COLS, IDX_HEADS, IDX_DIM, B_Q_COLS, B_KV_COLS, B_KV_COLS)
    offs = [0]
    for s in sizes:
        offs.append(offs[-1] + s)
    qa, ka, va, qi, wi, ki, qb, kb, vb = [w_in[:, offs[n]:offs[n + 1]] for n in range(len(sizes))]
    pad = jnp.zeros((w_in.shape[0], PROJ_TN - IDX_DIM - IDX_HEADS), w_in.dtype)
    return jnp.concatenate([qa, ka, va, qi, qb, kb, vb, ki, wi, pad], axis=1).astype(BF16)


TOPK_GROUPS = 256


def _sort_key(score):
    score = jnp.where(score == 0.0, 0.0, score)
    bits = pltpu.bitcast(score, I32)
    return bits ^ ((bits >> 31) & 0x7FFFFFFF)


def _fold_lanes(op, x, width):
    parts = [x[:, u * width:(u + 1) * width] for u in range(x.shape[1] // width)]
    while len(parts) > 1:
        nxt = [op(parts[u], parts[u + 1]) for u in range(0, len(parts) - 1, 2)]
        if len(parts) % 2:
            nxt.append(parts[-1])
        parts = nxt
    return parts[0]


def _lane_allreduce(op, x):
    shift = LANES // 2
    while shift >= 1:
        x = op(x, pltpu.roll(x, shift, axis=1))
        shift //= 2
    return x


def _count(keys_ref, nch, cw, pred):
    rows = keys_ref.shape[0]

    def body(c, acc):
        off = pl.multiple_of(c * cw, cw)
        blk = keys_ref[:, pl.ds(off, cw)]
        idx = off + lax.broadcasted_iota(I32, blk.shape, 1)
        return acc + _fold_lanes(jnp.add, jnp.where(pred(blk, idx), 1.0, 0.0), LANES)

    acc = lax.fori_loop(0, nch, body, jnp.zeros((rows, LANES), F32))
    return jnp.sum(acc, axis=1, keepdims=True)


def _row_bounds(keys_ref, nch, cw):
    rows = keys_ref.shape[0]

    def body(c, gmax):
        off = pl.multiple_of(c * cw, cw)
        return jnp.maximum(gmax, _fold_lanes(jnp.maximum, keys_ref[:, pl.ds(off, cw)], TOPK_GROUPS))

    gmax = lax.fori_loop(0, nch, body, jnp.full((rows, TOPK_GROUPS), INT_MIN, I32))
    lo = _lane_allreduce(jnp.minimum, _fold_lanes(jnp.minimum, gmax, LANES))[:, :1]
    top = _lane_allreduce(jnp.maximum, _fold_lanes(jnp.maximum, gmax, LANES))[:, :1]
    return lo, top


def _select_threshold(keys_ref, nch, cw, topk, idx_bits):
    rows = keys_ref.shape[0]
    kf = float(topk)
    lo, top = _row_bounds(keys_ref, nch, cw)
    hi = top + 1

    def n_open(lo, hi):
        return jnp.max(jnp.where(hi - 1 > lo, 1.0, 0.0))

    def cond(st):
        return (st[2] > 0.5) & (st[3] < 40)

    def body(st):
        lo, hi, _, it = st
        mid = (lo & hi) + ((lo ^ hi) >> 1)
        cnt = _count(keys_ref, nch, cw, lambda k, i: k >= mid)
        ge = cnt >= kf
        lo = jnp.where(ge, mid, lo)
        hi = jnp.where(cnt == kf, mid + 1, jnp.where(ge, hi, mid))
        return lo, hi, n_open(lo, hi), it + 1

    tau, _, _, _ = lax.while_loop(cond, body, (lo, hi, n_open(lo, hi), jnp.int32(0)))
    n_ge = _count(keys_ref, nch, cw, lambda k, i: k >= tau)
    has_k = tau > INT_MIN
    tied = has_k & (n_ge > kf)
    any_tied = jnp.max(jnp.where(tied, 1.0, 0.0)) > 0.5

    def resolve(_):
        need = kf - _count(keys_ref, nch, cw, lambda k, i: k > tau)
        lim = jnp.zeros((rows, 1), I32)
        for b in range(idx_bits - 1, -1, -1):
            cand = lim + (1 << b)
            cnt = _count(keys_ref, nch, cw, lambda k, i: (k == tau) & (i < cand))
            lim = jnp.where(cnt < need, cand, lim)
        return lim

    lim = lax.cond(any_tied, resolve, lambda _: jnp.zeros((rows, 1), I32), 0)
    jlim = jnp.where(tied, lim, jnp.where(has_k, jnp.int32(2 ** 31 - 1), jnp.int32(-1)))
    return tau, jlim


def _keys_to_bias(keys_ref, bias_ref, nch, cw, tau, jlim):
    def body(c, carry):
        off = pl.multiple_of(c * cw, cw)
        key = keys_ref[:, pl.ds(off, cw)]
        idx = off + lax.broadcasted_iota(I32, key.shape, 1)
        sel = (key > tau) | ((key == tau) & (idx <= jlim))
        bias_ref[:, pl.ds(off, cw)] = pltpu.bitcast(jnp.where(sel, 0.0, NEG), I32)
        return carry

    lax.fori_loop(0, nch, body, 0)


def _flash_update(sl, d, v, m_scr, acc_scr):
    reps = d.shape[1] // LANES
    m_old = m_scr[sl]
    m_new = jnp.maximum(m_old, jnp.max(d, axis=1, keepdims=True))
    alpha = jnp.exp2(m_old - m_new)
    p = jnp.exp2(d - jnp.tile(m_new, (1, reps)))
    v_ones = jnp.concatenate([v, jnp.ones(v.shape, BF16)], axis=1)
    acc_scr[sl] = jnp.tile(alpha, (1, 2)) * acc_scr[sl] + jnp.dot(p.astype(BF16), v_ones,
                                                                   preferred_element_type=F32)
    m_scr[sl] = m_new


def _flash_init(m_scr, acc_scr):
    m_scr[...] = jnp.full(m_scr.shape, NEG, F32)
    acc_scr[...] = jnp.zeros(acc_scr.shape, F32)


def _flash_result(acc, guard=False):
    l = acc[:, HEAD_DIM:]
    return acc[:, :HEAD_DIM] / (jnp.maximum(l, 1e-30) if guard else l)


def _causal_steps(n_q, chunks_of):
    qb, cb = [], []
    for i in range(n_q):
        for c in range(chunks_of(i)):
            qb.append(i)
            cb.append(c)
    return jnp.asarray(qb, I32), jnp.asarray(cb, I32)


IDX_TN = 512


def _dsa_prompt_kernel(qb_ref, cb_ref, qi_ref, misc_ref, kit_ref, qa_ref, k_ref, v_ref, o_ref,
                       keys_scr, qih_scr, wib_scr, d_scr, m_scr, acc_scr,
                       *, tq, tk, topk, idx_bits):
    step = pl.program_id(0)
    i = qb_ref[step]
    c = cb_ref[step]
    t0 = i * tq
    nch = (t0 + tq + tk - 1) // tk
    rt = 32
    n_chunks = keys_scr.shape[1] // tk
    scan = max(w for w in (1, 2) if n_chunks % w == 0)

    @pl.when(c == 0)
    def _():
        for hh in range(IDX_HEADS):
            qih_scr[hh] = qi_ref[:, hh * IDX_DIM:(hh + 1) * IDX_DIM]
            wib_scr[hh] = jnp.broadcast_to(misc_ref[:, MISC_WI + hh:MISC_WI + hh + 1], (tq, LANES))
        qpos = t0 + lax.broadcasted_iota(I32, (tq, 1), 0)

        def sub_body(kc, carry):
            off = pl.multiple_of(kc * IDX_TN, IDX_TN)
            d_scr[...] = jnp.dot(qih_scr[...].reshape(IDX_HEADS * tq, IDX_DIM),
                                 kit_ref[:, pl.ds(off, IDX_TN)], preferred_element_type=F32)
            kpos = off + lax.broadcasted_iota(I32, (rt, IDX_TN), 1)
            for r in range(tq // rt):
                score = jnp.zeros((rt, IDX_TN), F32)
                for hh in range(IDX_HEADS):
                    w = wib_scr[hh, r * rt:(r + 1) * rt, :]
                    dd = d_scr[hh * tq + r * rt:hh * tq + (r + 1) * rt, :]
                    score = score + jnp.tile(w, (1, IDX_TN // LANES)) * jnp.maximum(dd, 0.0)
                keys_scr[r * rt:(r + 1) * rt, pl.ds(off, IDX_TN)] = jnp.where(
                    kpos <= qpos[r * rt:(r + 1) * rt], _sort_key(score), INT_MIN)
            return carry

        lax.fori_loop(0, nch * (tk // IDX_TN), sub_body, 0)

        n_scan = (nch + scan - 1) // scan

        def pad_body(kc, carry):
            keys_scr[:, pl.ds(pl.multiple_of(kc * tk, tk), tk)] = jnp.full((tq, tk), INT_MIN, I32)
            return carry

        lax.fori_loop(nch, n_scan * scan, pad_body, 0)
        tau, jl = _select_threshold(keys_scr, n_scan, scan * tk, topk, idx_bits)
        _keys_to_bias(keys_scr, keys_scr, n_scan, scan * tk, tau, jl)
        _flash_init(m_scr, acc_scr)

    bias = pltpu.bitcast(keys_scr[:, pl.ds(pl.multiple_of(c * tk, tk), tk)], F32)
    group = A_HEADS // A_KV_HEADS

    def scores(h):
        kv = h // group
        return _nt_dot(qa_ref[:, h * HEAD_DIM:(h + 1) * HEAD_DIM],
                       k_ref[:, kv * HEAD_DIM:(kv + 1) * HEAD_DIM]) + bias

    d_next = scores(0)
    for h in range(A_HEADS):
        d = d_next
        if h + 1 < A_HEADS:
            d_next = scores(h + 1)
        kv = h // group
        _flash_update(h, d, v_ref[:, kv * HEAD_DIM:(kv + 1) * HEAD_DIM], m_scr, acc_scr)

    @pl.when(c == nch - 1)
    def _():
        for h in range(A_HEADS):
            o_ref[:, h * HEAD_DIM:(h + 1) * HEAD_DIM] = _flash_result(acc_scr[h]).astype(o_ref.dtype)


def _dsa_prompt(slab, misc, kit, tq, tk):
    rows = slab.shape[0]
    topk = min(DSA_TOPK, rows // 4)
    qb, cb = _causal_steps(rows // tq, lambda i: (i * tq + tq + tk - 1) // tk)
    kern = functools.partial(_dsa_prompt_kernel, tq=tq, tk=tk, topk=topk,
                             idx_bits=max(1, (rows - 1).bit_length()))
    grid_spec = pltpu.PrefetchScalarGridSpec(
        num_scalar_prefetch=2,
        grid=(int(qb.shape[0]),),
        in_specs=[pl.BlockSpec((tq, I_Q_COLS), lambda s, qb, cb: (qb[s], COL_QI // I_Q_COLS)),
                  pl.BlockSpec((tq, LANES), lambda s, qb, cb: (qb[s], 0)),
                  pl.BlockSpec((IDX_DIM, rows), lambda s, qb, cb: (0, 0)),
                  pl.BlockSpec((tq, A_Q_COLS), lambda s, qb, cb: (qb[s], COL_QA // A_Q_COLS)),
                  pl.BlockSpec((tk, A_KV_COLS), lambda s, qb, cb: (cb[s], COL_KA // A_KV_COLS)),
                  pl.BlockSpec((tk, A_KV_COLS), lambda s, qb, cb: (cb[s], COL_VA // A_KV_COLS))],
        out_specs=pl.BlockSpec((tq, A_Q_COLS), lambda s, qb, cb: (qb[s], 0)),
        scratch_shapes=[pltpu.VMEM((tq, rows), I32),
                        pltpu.VMEM((IDX_HEADS, tq, IDX_DIM), BF16),
                        pltpu.VMEM((IDX_HEADS, tq, LANES), F32),
                        pltpu.VMEM((IDX_HEADS * tq, IDX_TN), F32),
                        pltpu.VMEM((A_HEADS, tq, LANES), F32),
                        pltpu.VMEM((A_HEADS, tq, 2 * HEAD_DIM), F32)])
    return pl.pallas_call(
        kern, grid_spec=grid_spec, name="dsa_prompt",
        out_shape=jax.ShapeDtypeStruct((rows, A_Q_COLS), BF16),
        compiler_params=pltpu.CompilerParams(
            dimension_semantics=("arbitrary",), vmem_limit_bytes=VMEM_LIMIT),
    )(qb, cb, slab, misc, kit, slab, slab, slab)


def _gate_topk(gate, n_valid, ksel):
    lane = lax.broadcasted_iota(I32, gate.shape, 1)
    lanef = lane.astype(F32)
    gate = jnp.where(lane < n_valid, gate, NEG)
    sel = jnp.zeros(gate.shape, F32)
    for _ in range(ksel):
        mx = jnp.max(gate, axis=1, keepdims=True)
        first = jnp.min(jnp.where(gate == mx, lanef, 1e9), axis=1, keepdims=True)
        hit = (lanef == first) & (mx > 0.5 * NEG)
        sel = jnp.where(hit, 1.0, sel)
        gate = jnp.where(hit, NEG, gate)
    return sel


def _block_bias_rows(sel):
    return ((sel - 1.0) * (-NEG)).astype(BF16)


def _block_expansion(n_keys, blk0, keys_on_rows):
    shape = (n_keys, LANES) if keys_on_rows else (LANES, n_keys)
    kdim, bdim = (0, 1) if keys_on_rows else (1, 0)
    kblk = lax.broadcasted_iota(I32, shape, kdim) // MOBA_BLOCK + blk0
    return jnp.where(kblk == lax.broadcasted_iota(I32, shape, bdim), 1.0, 0.0).astype(BF16)


def _block_means_kernel(k_ref, o_ref):
    x = k_ref[...]
    s8 = jnp.sum(x.reshape(x.shape[0] // SUBLANES, SUBLANES, HEAD_DIM), axis=0)
    o_ref[0] = (s8[:B_KV_HEADS] + s8[B_KV_HEADS:]) * (1.0 / MOBA_BLOCK)


def _block_means(kb_rows):
    blk_rows = MOBA_BLOCK * B_KV_HEADS
    nb = kb_rows.shape[0] // blk_rows
    out = pl.pallas_call(
        _block_means_kernel,
        grid=(nb,),
        in_specs=[pl.BlockSpec((blk_rows, HEAD_DIM), lambda n: (n, 0))],
        out_specs=pl.BlockSpec((1, B_KV_HEADS, HEAD_DIM), lambda n: (n, 0, 0)),
        out_shape=jax.ShapeDtypeStruct((nb, B_KV_HEADS, HEAD_DIM), F32),
        name="block_means",
    )(kb_rows)
    return out.reshape(nb, B_KV_COLS)


def _moba_prompt_kernel(qb_ref, cb_ref, q_ref, means_ref, k_ref, v_ref, o_ref,
                        qaug_scr, m_scr, acc_scr, *, tq, tk, ksel):
    step = pl.program_id(0)
    i = qb_ref[step]
    c = cb_ref[step]
    t0 = i * tq
    qblk = t0 // MOBA_BLOCK
    bpc = tk // MOBA_BLOCK
    nch = qblk // bpc + 1
    group = B_HEADS // B_KV_HEADS

    @pl.when(c == 0)
    def _():
        lane = lax.broadcasted_iota(I32, (tq, LANES), 1)
        for h in range(B_HEADS):
            kv = h // group
            q = q_ref[:, h * HEAD_DIM:(h + 1) * HEAD_DIM]
            gate = _nt_dot(q, means_ref[:, kv * HEAD_DIM:(kv + 1) * HEAD_DIM])
            sel = jnp.where(lane == qblk, 1.0, _gate_topk(gate, qblk, ksel))
            qaug_scr[h] = jnp.concatenate([q, _block_bias_rows(sel)], axis=1)
        _flash_init(m_scr, acc_scr)

    e_t = _block_expansion(tk, c * bpc, keys_on_rows=True)
    qlim = t0 + lax.broadcasted_iota(I32, (tq, 1), 0) + jnp.where(c == nch - 1, 0, 2 ** 30)
    kpos = c * tk + lax.broadcasted_iota(I32, (tq, tk), 1)
    causal = jnp.where(kpos <= qlim, 0.0, NEG)
    for kv in range(B_KV_HEADS):
        rhs = jnp.concatenate([k_ref[:, kv * HEAD_DIM:(kv + 1) * HEAD_DIM], e_t], axis=1)
        vv = v_ref[:, kv * HEAD_DIM:(kv + 1) * HEAD_DIM]
        for g in range(group):
            h = kv * group + g
            _flash_update(h, _nt_dot(qaug_scr[h], rhs) + causal, vv, m_scr, acc_scr)

    @pl.when(c == nch - 1)
    def _():
        for h in range(B_HEADS):
            o_ref[:, h * HEAD_DIM:(h + 1) * HEAD_DIM] = _flash_result(acc_scr[h]).astype(o_ref.dtype)


def _moba_prompt(slab, means_pad, tq, tk):
    rows = slab.shape[0]
    ksel = min(MOBA_TOPK, (rows - 1) // MOBA_BLOCK)
    assert tq == MOBA_BLOCK and tk % MOBA_BLOCK == 0 and means_pad.shape[0] == LANES
    qb, cb = _causal_steps(rows // tq, lambda i: (i * tq) // tk + 1)
    kern = functools.partial(_moba_prompt_kernel, tq=tq, tk=tk, ksel=ksel)
    grid_spec = pltpu.PrefetchScalarGridSpec(
        num_scalar_prefetch=2,
        grid=(int(qb.shape[0]),),
        in_specs=[pl.BlockSpec((tq, B_Q_COLS), lambda s, qb, cb: (qb[s], COL_QB // B_Q_COLS)),
                  pl.BlockSpec((LANES, B_KV_COLS), lambda s, qb, cb: (0, 0)),
                  pl.BlockSpec((tk, B_KV_COLS), lambda s, qb, cb: (cb[s], COL_KB // B_KV_COLS)),
                  pl.BlockSpec((tk, B_KV_COLS), lambda s, qb, cb: (cb[s], COL_VB // B_KV_COLS))],
        out_specs=pl.BlockSpec((tq, B_Q_COLS), lambda s, qb, cb: (qb[s], 0)),
        scratch_shapes=[pltpu.VMEM((B_HEADS, tq, 2 * HEAD_DIM), BF16),
                        pltpu.VMEM((B_HEADS, tq, LANES), F32),
                        pltpu.VMEM((B_HEADS, tq, 2 * HEAD_DIM), F32)])
    return pl.pallas_call(
        kern, grid_spec=grid_spec, name="moba_prompt",
        out_shape=jax.ShapeDtypeStruct((rows, B_Q_COLS), BF16),
        compiler_params=pltpu.CompilerParams(
            dimension_semantics=("arbitrary",), vmem_limit_bytes=VMEM_LIMIT),
    )(qb, cb, slab, means_pad, slab, slab)


TPAD = SUBLANES
SROWS = 8 * TPAD
KV_ROWS = SROWS // 4
PAGE_ROWS = PAGE_SIZE * 4


def _page_head(ref, kv):
    return ref[0, pl.ds(kv, PAGE_SIZE, stride=4), :]


def _paged_spec(shape, n_pages, per_step, first_step, n_steps, j):
    def index(b, s, pt):
        local = jnp.clip(s - first_step, 0, n_steps - 1)
        return (pt[b * n_pages + local * per_step + j], 0, 0)
    return pl.BlockSpec(shape, index)


def _dsa_sample_index_kernel(pt_ref, qi_ref, wi_ref, *rest, n_pages, pa, t_new):
    kidx_refs = rest[:pa]
    kin_ref, keys_ref, wib_scr = rest[pa:]
    keys_scr = keys_ref.at[0]
    s = pl.program_id(1)
    na = n_pages // pa
    past = n_pages * PAGE_SIZE
    ntot = keys_scr.shape[1]
    trow = lax.broadcasted_iota(I32, (TPAD, 1), 0)

    def index_keys(kd, off):
        width = kd.shape[1]
        x = jnp.tile(wib_scr[...], (1, width // LANES)) * jnp.maximum(
            jnp.dot(qi_ref[0], kd, preferred_element_type=F32), 0.0)
        score = x[:TPAD]
        for hh in range(1, IDX_HEADS):
            score = score + x[hh * TPAD:(hh + 1) * TPAD]
        kpos = off + lax.broadcasted_iota(I32, (TPAD, width), 1)
        visible = (kpos <= past + trow) & (trow < t_new)
        keys_scr[:, pl.ds(off, width)] = jnp.where(visible, _sort_key(score), INT_MIN)

    @pl.when(s == 0)
    def _():
        wib_scr[...] = jnp.broadcast_to(wi_ref[0], wib_scr.shape)
        keys_scr[:, past:] = jnp.full((TPAD, ntot - past), INT_MIN, I32)

    @pl.when(s < na)
    def _():
        kd = jnp.concatenate([r[0].astype(BF16) for r in kidx_refs], axis=1)
        index_keys(kd, pl.multiple_of(s * (pa * PAGE_SIZE), pa * PAGE_SIZE))

    @pl.when(s == na - 1)
    def _():
        index_keys(kin_ref[0], past)


def _topk_bias_kernel(keys_ref, bias_ref, *, cw, topk, idx_bits):
    nch = keys_ref.shape[1] // cw
    tau, jl = _select_threshold(keys_ref, nch, cw, topk, idx_bits)
    _keys_to_bias(keys_ref, bias_ref, nch, cw, tau, jl)


def _dsa_sample_attend_kernel(pt_ref, bias_ref, qa_ref, *rest, n_pages, pb):
    k_refs, v_refs = rest[:pb], rest[pb:2 * pb]
    kn_ref, vn_ref, o_ref, m_scr, acc_scr = rest[2 * pb:]
    keys_scr = bias_ref.at[0]
    s = pl.program_id(1)
    past = n_pages * PAGE_SIZE

    @pl.when(s == 0)
    def _():
        _flash_init(m_scr, acc_scr)

    def attend(key_of, val_of, off, width):
        bias = pltpu.bitcast(keys_scr[:, pl.ds(off, width)], F32)
        bias = jnp.concatenate([bias] * (KV_ROWS // TPAD), axis=0)
        for kv in range(A_KV_HEADS):
            rows = slice(kv * KV_ROWS, (kv + 1) * KV_ROWS)
            _flash_update(rows, _nt_dot(qa_ref[0, rows], key_of(kv)) + bias, val_of(kv), m_scr, acc_scr)

    def gather(refs, kv):
        return jnp.concatenate([_page_head(r, kv) for r in refs], axis=0).astype(BF16)

    attend(lambda kv: gather(k_refs, kv), lambda kv: gather(v_refs, kv),
           pl.multiple_of(s * (pb * PAGE_SIZE), pb * PAGE_SIZE), pb * PAGE_SIZE)

    @pl.when(s == n_pages // pb - 1)
    def _():
        attend(lambda kv: kn_ref[0, kv], lambda kv: vn_ref[0, kv], past, PAGE_SIZE)
        o_ref[0] = _flash_result(acc_scr[...], guard=True)


def _dsa_sample(pt_flat, n_pages, t_new, qi_s, wi_s, kidx_pages, ki_new_t, qa_s, k_pages, v_pages, ka_new, va_new,
                pa, pb):
    ns = qi_s.shape[0]
    past = n_pages * PAGE_SIZE
    topk = min(DSA_TOPK, (past + t_new) // 4)
    cw = 1024
    ntot = -(-(past + PAGE_SIZE) // cw) * cw
    na, nb = n_pages // pa, n_pages // pb
    seq3 = lambda s1, s2: pl.BlockSpec((1, s1, s2), lambda b, s, pt: (b, 0, 0))
    seq4 = pl.BlockSpec((1, A_KV_HEADS, PAGE_SIZE, HEAD_DIM), lambda b, s, pt: (b, 0, 0, 0))
    params = pltpu.CompilerParams(dimension_semantics=("arbitrary", "arbitrary"), vmem_limit_bytes=VMEM_LIMIT)

    keys = pl.pallas_call(
        functools.partial(_dsa_sample_index_kernel, n_pages=n_pages, pa=pa, t_new=t_new),
        grid_spec=pltpu.PrefetchScalarGridSpec(
            num_scalar_prefetch=1,
            grid=(ns, na),
            in_specs=[seq3(IDX_HEADS * TPAD, IDX_DIM), seq3(IDX_HEADS * TPAD, 1)]
                     + [_paged_spec((1, IDX_DIM, PAGE_SIZE), n_pages, pa, 0, na, j) for j in range(pa)]
                     + [seq3(IDX_DIM, PAGE_SIZE)],
            out_specs=seq3(TPAD, ntot),
            scratch_shapes=[pltpu.VMEM((IDX_HEADS * TPAD, LANES), F32)]),
        out_shape=jax.ShapeDtypeStruct((ns, TPAD, ntot), I32),
        name="dsa_sample_index", compiler_params=params,
    )(pt_flat, qi_s, wi_s, *([kidx_pages] * pa), ki_new_t)

    rows = ns * t_new
    rows_pad = -(-rows // SUBLANES) * SUBLANES
    keys2d = _pad_rows(keys[:, :t_new].reshape(rows, ntot), 0, rows_pad)
    bias2d = pl.pallas_call(
        functools.partial(_topk_bias_kernel, cw=cw, topk=topk, idx_bits=max(1, (ntot - 1).bit_length())),
        out_shape=jax.ShapeDtypeStruct((rows_pad, ntot), I32),
        name="dsa_sample_select", compiler_params=pltpu.CompilerParams(vmem_limit_bytes=VMEM_LIMIT),
    )(keys2d)
    bias = _pad_rows(bias2d[:rows].reshape(ns, t_new, ntot), 1, TPAD)

    page_kv = lambda j: _paged_spec((1, PAGE_ROWS, HEAD_DIM), n_pages, pb, 0, nb, j)
    return pl.pallas_call(
        functools.partial(_dsa_sample_attend_kernel, n_pages=n_pages, pb=pb),
        grid_spec=pltpu.PrefetchScalarGridSpec(
            num_scalar_prefetch=1,
            grid=(ns, nb),
            in_specs=[seq3(TPAD, ntot), seq3(SROWS, HEAD_DIM)]
                     + [page_kv(j) for j in range(pb)] + [page_kv(j) for j in range(pb)]
                     + [seq4, seq4],
            out_specs=seq3(SROWS, HEAD_DIM),
            scratch_shapes=[pltpu.VMEM((SROWS, LANES), F32),
                            pltpu.VMEM((SROWS, 2 * HEAD_DIM), F32)]),
        out_shape=jax.ShapeDtypeStruct((ns, SROWS, HEAD_DIM), F32),
        name="dsa_sample_attend", compiler_params=params,
    )(pt_flat, bias, qa_s, *([k_pages] * pb), *([v_pages] * pb), ka_new, va_new)


def _moba_sample_kernel(pt_ref, q_ref, *rest, n_pages, pb, t_new, ksel):
    k_refs, v_refs = rest[:pb], rest[pb:2 * pb]
    kn_ref, vn_ref, o_ref, bsum_scr, s_scr, seln_scr, m_scr, acc_scr = rest[2 * pb:]
    s = pl.program_id(1)
    nst = n_pages // pb
    past = n_pages * PAGE_SIZE
    nbp = past // MOBA_BLOCK
    bps = pb * PAGE_SIZE // MOBA_BLOCK
    width = pb * PAGE_SIZE
    tok = lax.broadcasted_iota(I32, (KV_ROWS, 1), 0) % TPAD
    pad_bias = jnp.where(tok < t_new, 0.0, NEG)
    kv_rows = lambda kv: slice(kv * KV_ROWS, (kv + 1) * KV_ROWS)

    @pl.when(s == 0)
    def _():
        bsum_scr[...] = jnp.zeros(bsum_scr.shape, F32)

    @pl.when(s < nst)
    def _():
        off = pl.multiple_of(s * width, width)
        for kv in range(B_KV_HEADS):
            kf = jnp.concatenate([_page_head(r, kv) for r in k_refs], axis=0)
            bsum_scr[kv, pl.ds(pl.multiple_of(s * bps, bps), bps), :] = jnp.sum(
                kf.reshape(bps, MOBA_BLOCK, HEAD_DIM), axis=1)
            s_scr[kv_rows(kv), pl.ds(off, width)] = _nt_dot(q_ref[0, kv_rows(kv)], kf.astype(BF16))

    @pl.when(s == nst - 1)
    def _():
        for kv in range(B_KV_HEADS):
            q = q_ref[0, kv_rows(kv)]
            s_scr[kv_rows(kv), pl.ds(past, PAGE_SIZE)] = _nt_dot(q, kn_ref[0, kv])
            means = (bsum_scr[kv] * (1.0 / MOBA_BLOCK)).astype(BF16)
            seln_scr[kv_rows(kv)] = _block_bias_rows(_gate_topk(_nt_dot(q, means), nbp, ksel))
        _flash_init(m_scr, acc_scr)

    @pl.when(s >= nst)
    def _():
        sb = s - nst
        off = pl.multiple_of(sb * width, width)
        expand = _block_expansion(width, sb * bps, keys_on_rows=False)
        for kv in range(B_KV_HEADS):
            bias = jnp.dot(seln_scr[kv_rows(kv)], expand, preferred_element_type=F32) + pad_bias
            vv = jnp.concatenate([_page_head(r, kv) for r in v_refs], axis=0).astype(BF16)
            _flash_update(kv_rows(kv), s_scr[kv_rows(kv), pl.ds(off, width)] + bias, vv, m_scr, acc_scr)

    @pl.when(s == 2 * nst - 1)
    def _():
        kpos = past + lax.broadcasted_iota(I32, (KV_ROWS, PAGE_SIZE), 1)
        bias = jnp.where(kpos <= past + tok, 0.0, NEG) + pad_bias
        for kv in range(B_KV_HEADS):
            _flash_update(kv_rows(kv), s_scr[kv_rows(kv), pl.ds(past, PAGE_SIZE)] + bias, vn_ref[0, kv],
                          m_scr, acc_scr)
        o_ref[0] = _flash_result(acc_scr[...], guard=True)


def _moba_sample(pt_flat, n_pages, t_new, qb_s, k_pages, v_pages, kb_new, vb_new, pb):
    ns = qb_s.shape[0]
    past = n_pages * PAGE_SIZE
    nbp = past // MOBA_BLOCK
    assert nbp <= LANES and (pb * PAGE_SIZE // MOBA_BLOCK) % SUBLANES == 0
    ksel = min(MOBA_TOPK, nbp)
    nst = n_pages // pb
    seq3 = lambda s1, s2: pl.BlockSpec((1, s1, s2), lambda b, s, pt: (b, 0, 0))
    seq4 = pl.BlockSpec((1, B_KV_HEADS, PAGE_SIZE, HEAD_DIM), lambda b, s, pt: (b, 0, 0, 0))
    kern = functools.partial(_moba_sample_kernel, n_pages=n_pages, pb=pb, t_new=t_new, ksel=ksel)
    grid_spec = pltpu.PrefetchScalarGridSpec(
        num_scalar_prefetch=1,
        grid=(ns, 2 * nst),
        in_specs=[seq3(SROWS, HEAD_DIM)]
                 + [_paged_spec((1, PAGE_ROWS, HEAD_DIM), n_pages, pb, 0, nst, j) for j in range(pb)]
                 + [_paged_spec((1, PAGE_ROWS, HEAD_DIM), n_pages, pb, nst, nst, j) for j in range(pb)]
                 + [seq4, seq4],
        out_specs=seq3(SROWS, HEAD_DIM),
        scratch_shapes=[pltpu.VMEM((B_KV_HEADS, LANES, HEAD_DIM), F32),
                        pltpu.VMEM((SROWS, past + PAGE_SIZE), F32),
                        pltpu.VMEM((SROWS, LANES), BF16),
                        pltpu.VMEM((SROWS, LANES), F32),
                        pltpu.VMEM((SROWS, 2 * HEAD_DIM), F32)])
    return pl.pallas_call(
        kern, grid_spec=grid_spec, name="moba_sample",
        out_shape=jax.ShapeDtypeStruct((ns, SROWS, HEAD_DIM), F32),
        compiler_params=pltpu.CompilerParams(
            dimension_semantics=("arbitrary", "arbitrary"), vmem_limit_bytes=VMEM_LIMIT),
    )(pt_flat, qb_s, *([k_pages] * pb), *([v_pages] * pb), kb_new, vb_new)


def _memory_kv_kernel(mem_ref, g_ref, w_ref, k_ref, v_ref):
    h = _rms(mem_ref[...], g_ref[...]).astype(BF16)
    kv = jnp.dot(h, w_ref[...], preferred_element_type=F32)
    k_ref[...] = kv[:, :MEM_WIDTH]
    v_ref[...] = kv[:, MEM_WIDTH:]


def _memory_kv(mem, g, w_bf):
    m = mem.shape[0]
    return pl.pallas_call(
        _memory_kv_kernel,
        out_shape=[jax.ShapeDtypeStruct((m, MEM_WIDTH), F32), jax.ShapeDtypeStruct((m, MEM_WIDTH), F32)],
        name="memory_kv",
        compiler_params=pltpu.CompilerParams(vmem_limit_bytes=VMEM_LIMIT),
    )(mem, g, w_bf)


def _outproj_cross_kernel(x_ref, oa_ref, ob_ref, woa_ref, wob_ref, g_ref, wq_ref, mk_ref, mv_ref, wo_ref,
                          y_ref, *, rows_per_seq):
    x1 = x_ref[...] + jnp.dot(oa_ref[...], woa_ref[...], preferred_element_type=F32) \
        + jnp.dot(ob_ref[...], wob_ref[...], preferred_element_type=F32)
    hc = _rms(x1, g_ref[...]).astype(BF16)
    q = jnp.dot(hc, wq_ref[...], preferred_element_type=F32).astype(BF16)
    tm = q.shape[0]
    nk = mk_ref.shape[0]
    scale = MEM_HEAD_DIM ** -0.5
    if rows_per_seq is not None:
        rseq = (pl.program_id(0) * tm + lax.broadcasted_iota(I32, (tm, 1), 0)) // rows_per_seq
        kseq = lax.broadcasted_iota(I32, (tm, nk), 1) // MEM_TOKENS
        mask = kseq == rseq
    outs = []
    for h in range(MEM_HEADS):
        sl = slice(h * MEM_HEAD_DIM, (h + 1) * MEM_HEAD_DIM)
        s = _nt_dot(q[:, sl], mk_ref[:, sl]) * scale
        if rows_per_seq is not None:
            s = jnp.where(mask, s, NEG)
        m = jnp.max(s, axis=1, keepdims=True)
        e = jnp.exp(s - m)
        p = e / jnp.sum(e, axis=1, keepdims=True)
        outs.append(jnp.dot(p.astype(BF16), mv_ref[:, sl], preferred_element_type=F32))
    o = jnp.concatenate(outs, axis=1).astype(BF16)
    y_ref[...] = x1 + jnp.dot(o, wo_ref[...], preferred_element_type=F32)


def _outproj_cross(x, oa, ob, w_out_bf, g_cross, w_mq_bf, mk_bf, mv_bf, w_mo_bf, tm, rows_per_seq):
    rows = x.shape[0]
    nk = mk_bf.shape[0]
    row_spec = lambda w: pl.BlockSpec((tm, w), lambda i: (i, 0))
    full = lambda a, b: pl.BlockSpec((a, b), lambda i: (0, 0))
    kern = functools.partial(_outproj_cross_kernel, rows_per_seq=rows_per_seq)
    return pl.pallas_call(
        kern,
        grid=(rows // tm,),
        in_specs=[row_spec(D_MODEL), row_spec(A_Q_COLS), row_spec(B_Q_COLS),
                  pl.BlockSpec((A_Q_COLS, D_MODEL), lambda i: (0, 0)),
                  pl.BlockSpec((B_Q_COLS, D_MODEL), lambda i: (1, 0)),
                  full(1, D_MODEL), full(D_MODEL, MEM_WIDTH), full(nk, MEM_WIDTH), full(nk, MEM_WIDTH),
                  full(MEM_WIDTH, D_MODEL)],
        out_specs=row_spec(D_MODEL),
        out_shape=jax.ShapeDtypeStruct((rows, D_MODEL), F32),
        name="outproj_cross",
        compiler_params=pltpu.CompilerParams(
            dimension_semantics=("parallel",), vmem_limit_bytes=VMEM_LIMIT),
    )(x, oa, ob, w_out_bf, w_out_bf, g_cross, w_mq_bf, mk_bf, mv_bf, w_mo_bf)


def _mlp_final_kernel(x_ref, g_ref, wu_ref, wd_ref, gf_ref, y_ref, h_scr, acc_scr):
    f = pl.program_id(1)

    @pl.when(f == 0)
    def _():
        h_scr[...] = _rms(x_ref[...], g_ref[...]).astype(BF16)
        acc_scr[...] = jnp.zeros(acc_scr.shape, F32)

    u = jnp.maximum(jnp.dot(h_scr[...], wu_ref[...], preferred_element_type=F32), 0.0)
    acc_scr[...] += jnp.dot((u * u).astype(BF16), wd_ref[...], preferred_element_type=F32)

    @pl.when(f == pl.num_programs(1) - 1)
    def _():
        y_ref[...] = _rms(x_ref[...] + acc_scr[...], gf_ref[...])


def _mlp_final(x, g_ffn, w_up_bf, w_down_bf, g_final, tm, tf):
    rows = x.shape[0]
    d_ff = w_up_bf.shape[1]
    return pl.pallas_call(
        _mlp_final_kernel,
        grid=(rows // tm, d_ff // tf),
        in_specs=[pl.BlockSpec((tm, D_MODEL), lambda i, f: (i, 0)),
                  pl.BlockSpec((1, D_MODEL), lambda i, f: (0, 0)),
                  pl.BlockSpec((D_MODEL, tf), lambda i, f: (0, f)),
                  pl.BlockSpec((tf, D_MODEL), lambda i, f: (f, 0)),
                  pl.BlockSpec((1, D_MODEL), lambda i, f: (0, 0))],
        out_specs=pl.BlockSpec((tm, D_MODEL), lambda i, f: (i, 0)),
        out_shape=jax.ShapeDtypeStruct((rows, D_MODEL), F32),
        scratch_shapes=[pltpu.VMEM((tm, D_MODEL), BF16), pltpu.VMEM((tm, D_MODEL), F32)],
        name="mlp_final",
        compiler_params=pltpu.CompilerParams(
            dimension_semantics=("parallel", "arbitrary"), vmem_limit_bytes=VMEM_LIMIT),
    )(x, g_ffn, w_up_bf, w_down_bf, g_final)


def _row_tile(rows, want):
    t = min(rows, want)
    while rows % t:
        t //= 2
    return t


def _pad_rows(a, axis, size):
    pad = [(0, 0)] * a.ndim
    pad[axis] = (0, size - a.shape[axis])
    return jnp.pad(a, pad)


def _heads_first(a, ns, t_new, heads, dim, t_pad):
    a = a.reshape(ns, t_new, heads, dim).transpose(0, 2, 1, 3)
    return _pad_rows(a, 2, t_pad)


def kernel(x_prompt, x_sample, cache_k_a, cache_v_a, cache_kidx, cache_k_b, cache_v_b, cache_mem_k,
           cache_mem_v, page_table, mem_prompt, g_mix, w_in, g_kidx, b_kidx, w_out, g_cross, w_mq, g_mem,
           w_mkv, w_mo, g_ffn, w_up, w_down, g_final):
    batch, s_len, _ = x_prompt.shape
    ns, t_new, _ = x_sample.shape
    depth = w_in.shape[0]
    n_pages = page_table.shape[1]
    past = n_pages * PAGE_SIZE
    n_phys = cache_k_a.shape[1]
    assert batch == 1 and depth == 1
    assert s_len % 1024 == 0 and s_len // MOBA_BLOCK <= LANES
    assert past % MOBA_BLOCK == 0 and t_new <= TPAD
    l = 0
    row = lambda v: v.reshape(1, -1)

    w_in_p = _permute_w_in(w_in[l])
    gk = row(jnp.concatenate([g_kidx[l], jnp.zeros((LANES - IDX_DIM,), F32)]))
    bk = row(jnp.concatenate([b_kidx[l], jnp.zeros((LANES - IDX_DIM,), F32)]))
    w_out_bf = w_out[l].astype(BF16)
    w_mq_bf = w_mq[l].astype(BF16)
    w_mo_bf = w_mo[l].astype(BF16)
    w_up_bf = w_up[l].astype(BF16)
    w_down_bf = w_down[l].astype(BF16)

    xp = x_prompt.reshape(s_len, D_MODEL)
    tabs_p = _rope_tables(jnp.arange(s_len))
    slab, ka, va, kb, vb, misc, misc_bf = _proj(xp, row(g_mix[l]), w_in_p, tabs_p, gk, bk, _row_tile(s_len, 512))
    kit = misc_bf[:, :IDX_DIM].T
    oa = _dsa_prompt(slab, misc, kit, tq=256, tk=1024)
    means_pad = _pad_rows(_block_means(kb).astype(BF16), 0, LANES)
    ob = _moba_prompt(slab, means_pad, tq=MOBA_BLOCK, tk=1024)
    mk, mv = _memory_kv(mem_prompt.reshape(MEM_TOKENS, D_MODEL), row(g_mem[l]), w_mkv[l].astype(BF16))
    xp2 = _outproj_cross(xp, oa, ob, w_out_bf, row(g_cross[l]), w_mq_bf, mk.astype(BF16), mv.astype(BF16),
                         w_mo_bf, _row_tile(s_len, 256), None)
    y_prompt = _mlp_final(xp2, row(g_ffn[l]), w_up_bf, w_down_bf, row(g_final), _row_tile(s_len, 512), 1024)

    rows_s = ns * t_new
    xs = x_sample.reshape(rows_s, D_MODEL)
    pos_s = jnp.tile(past + jnp.arange(t_new), ns)
    tabs_s = _rope_tables(pos_s)
    rows_pad = -(-rows_s // SUBLANES) * SUBLANES
    xs_pad = _pad_rows(xs, 0, rows_pad)
    tabs_s = tuple(_pad_rows(t, 0, rows_pad) for t in tabs_s)
    slab_s, ka_s, va_s, kb_s, vb_s, misc_s, _ = _proj(xs_pad, row(g_mix[l]), w_in_p, tabs_s, gk, bk,
                                                      _row_tile(rows_pad, 128))
    slab_s, misc_s = slab_s[:rows_s], misc_s[:rows_s]
    ka_s, va_s, kb_s, vb_s = [a[:rows_s * A_KV_HEADS] for a in (ka_s, va_s, kb_s, vb_s)]

    pt_flat = page_table.reshape(-1).astype(I32)
    index_pages, attend_pages = min(32, n_pages), min(16, n_pages)
    assert n_pages % index_pages == 0 and n_pages % attend_pages == 0
    page_view = lambda c: c[l].reshape(n_phys, PAGE_ROWS, HEAD_DIM)
    kidx_view = jnp.swapaxes(cache_kidx[l], 1, 2)
    cols = lambda c0, w: slab_s[:, c0:c0 + w]
    qi_s = _heads_first(cols(COL_QI, I_Q_COLS), ns, t_new, IDX_HEADS, IDX_DIM, TPAD)
    qi_s = qi_s.reshape(ns, IDX_HEADS * TPAD, IDX_DIM)
    wi_s = _heads_first(misc_s[:, MISC_WI:MISC_WI + IDX_HEADS], ns, t_new, IDX_HEADS, 1, TPAD)
    wi_s = wi_s.reshape(ns, IDX_HEADS * TPAD, 1)
    ki_new_t = _pad_rows(cols(COL_MISC, IDX_DIM).reshape(ns, t_new, IDX_DIM).transpose(0, 2, 1), 2, PAGE_SIZE)
    q_rows = lambda c0: _heads_first(cols(c0, A_Q_COLS), ns, t_new, A_HEADS, HEAD_DIM, TPAD).reshape(
        ns, SROWS, HEAD_DIM)
    new_kv = lambda c0: _heads_first(cols(c0, A_KV_COLS), ns, t_new, A_KV_HEADS, HEAD_DIM, PAGE_SIZE)
    oa_s = _dsa_sample(pt_flat, n_pages, t_new, qi_s, wi_s, kidx_view, ki_new_t, q_rows(COL_QA),
                       page_view(cache_k_a), page_view(cache_v_a), new_kv(COL_KA), new_kv(COL_VA),
                       index_pages, attend_pages)
    ob_s = _moba_sample(pt_flat, n_pages, t_new, q_rows(COL_QB), page_view(cache_k_b), page_view(cache_v_b),
                        new_kv(COL_KB), new_kv(COL_VB), attend_pages)
    tokens_first = lambda o: o.reshape(ns, A_HEADS, TPAD, HEAD_DIM)[:, :, :t_new].transpose(0, 2, 1, 3).reshape(
        rows_s, -1).astype(BF16)
    oa_s2 = _pad_rows(tokens_first(oa_s), 0, rows_pad)
    ob_s2 = _pad_rows(tokens_first(ob_s), 0, rows_pad)
    mk_s = cache_mem_k[l].reshape(ns * MEM_TOKENS, MEM_WIDTH).astype(BF16)
    mv_s = cache_mem_v[l].reshape(ns * MEM_TOKENS, MEM_WIDTH).astype(BF16)
    xs2 = _outproj_cross(xs_pad, oa_s2, ob_s2, w_out_bf, row(g_cross[l]), w_mq_bf, mk_s, mv_s, w_mo_bf,
                         rows_pad, t_new)
    y_sample = _mlp_final(xs2, row(g_ffn[l]), w_up_bf, w_down_bf, row(g_final), rows_pad, 512)[:rows_s]

    kv5 = lambda a, n, t: a.reshape(1, n, t, A_KV_HEADS, HEAD_DIM)
    return (y_prompt.reshape(batch, s_len, D_MODEL), y_sample.reshape(ns, t_new, D_MODEL),
            kv5(ka, batch, s_len), kv5(va, batch, s_len),
            misc[:, :IDX_DIM].reshape(1, batch, s_len, IDX_DIM),
            kv5(kb, batch, s_len), kv5(vb, batch, s_len),
            mk.reshape(1, batch, MEM_TOKENS, MEM_HEADS, MEM_HEAD_DIM),
            mv.reshape(1, batch, MEM_TOKENS, MEM_HEADS, MEM_HEAD_DIM),
            kv5(ka_s, ns, t_new), kv5(va_s, ns, t_new),
            misc_s[:, :IDX_DIM].reshape(1, ns, t_new, IDX_DIM),
            kv5(kb_s, ns, t_new), kv5(vb_s, ns, t_new))
```

```python
import functools

import jax
import jax.numpy as jnp
from jax import lax
from jax.experimental import pallas as pl
from jax.experimental.pallas import tpu as pltpu

F32 = jnp.float32
BF16 = jnp.bfloat16
I32 = jnp.int32

D_MODEL = 2048
HEAD_DIM = 128
A_HEADS = 8
A_KV_HEADS = 4
B_HEADS = 8
B_KV_HEADS = 4
IDX_HEADS = 16
IDX_DIM = 64
DSA_TOPK = 256
MOBA_BLOCK = 256
MOBA_TOPK = 3
MEM_TOKENS = 256
MEM_HEADS = 4
MEM_HEAD_DIM = 128
PAGE_SIZE = 128
ROPE_THETA = 500000.0
NORM_EPS = 1e-6

A_Q_COLS = A_HEADS * HEAD_DIM
A_KV_COLS = A_KV_HEADS * HEAD_DIM
I_Q_COLS = IDX_HEADS * IDX_DIM
B_Q_COLS = B_HEADS * HEAD_DIM
B_KV_COLS = B_KV_HEADS * HEAD_DIM
MEM_WIDTH = MEM_HEADS * MEM_HEAD_DIM

LANES = 128
SUBLANES = 8
VMEM_LIMIT = 56 * 1024 * 1024
INT_MIN = -(2 ** 31)
NEG = -1e30
LOG2E = 1.4426950408889634
SOFTMAX_EXP2_SCALE = HEAD_DIM ** -0.5 * LOG2E

PROJ_TN = 512
COL_QA, COL_KA, COL_VA, COL_QI, COL_QB, COL_KB, COL_VB, COL_MISC = 0, 1024, 1536, 2048, 3072, 4096, 4608, 5120
SLAB_COLS = COL_MISC + PROJ_TN
MISC_WI = IDX_DIM


def _nt_dot(a, b):
    return lax.dot_general(a, b, (((1,), (1,)), ((), ())), preferred_element_type=F32)


def _rms(x, g):
    return x * lax.rsqrt(jnp.mean(x * x, axis=-1, keepdims=True) + NORM_EPS) * g


def _rope(x, cos_t, sin_t, half, period):
    width = x.shape[1]
    reps = width // LANES
    c = jnp.concatenate([cos_t] * reps, axis=1) if reps > 1 else cos_t
    s = jnp.concatenate([sin_t] * reps, axis=1) if reps > 1 else sin_t
    lane = lax.broadcasted_iota(I32, x.shape, 1)
    first = (lane & (period - 1)) < half
    up = pltpu.roll(x, width - half, axis=1)
    dn = pltpu.roll(x, half, axis=1)
    return x * c + jnp.where(first, up, dn) * s


def _proj_kernel(x_ref, g_ref, w_ref, c128_ref, s128_ref, c64_ref, s64_ref, gk_ref, bk_ref,
                 slab_ref, ka_ref, va_ref, kb_ref, vb_ref, misc_ref, miscb_ref, h_scr, pend_scr):
    j = pl.program_id(1)
    jj = j - 1

    def step(epilogue):
        acc = jnp.dot(h_scr[...], w_ref[...], preferred_element_type=F32)
        if epilogue is not None:
            epilogue(pend_scr[...])
        pend_scr[...] = acc

    @pl.when(j == 0)
    def _():
        h_scr[...] = _rms(x_ref[...], g_ref[...]).astype(BF16)
        step(None)

    def rope128(v):
        return _rope(v, c128_ref[...], s128_ref[...], HEAD_DIM // 8, HEAD_DIM)

    def store_heads(ref, v):
        for kv in range(A_KV_HEADS):
            ref[pl.ds(kv, v.shape[0], stride=A_KV_HEADS), :] = v[:, kv * HEAD_DIM:(kv + 1) * HEAD_DIM]

    def finish_q128(acc):
        slab_ref[...] = (rope128(acc) * SOFTMAX_EXP2_SCALE).astype(BF16)

    def finish_k(ref):
        def fn(acc):
            r = rope128(acc)
            store_heads(ref, r)
            slab_ref[...] = r.astype(BF16)
        return fn

    def finish_v(ref):
        def fn(acc):
            store_heads(ref, acc)
            slab_ref[...] = acc.astype(BF16)
        return fn

    def finish_qi(acc):
        slab_ref[...] = _rope(acc, c64_ref[...], s64_ref[...], IDX_DIM // 8, IDX_DIM).astype(BF16)

    is_q128 = (jj == COL_QA // PROJ_TN) | (jj == COL_QA // PROJ_TN + 1) | \
              (jj == COL_QB // PROJ_TN) | (jj == COL_QB // PROJ_TN + 1)
    pl.when(is_q128)(lambda: step(finish_q128))
    pl.when(jj == COL_KA // PROJ_TN)(lambda: step(finish_k(ka_ref)))
    pl.when(jj == COL_KB // PROJ_TN)(lambda: step(finish_k(kb_ref)))
    pl.when(jj == COL_VA // PROJ_TN)(lambda: step(finish_v(va_ref)))
    pl.when(jj == COL_VB // PROJ_TN)(lambda: step(finish_v(vb_ref)))
    pl.when((jj == COL_QI // PROJ_TN) | (jj == COL_QI // PROJ_TN + 1))(lambda: step(finish_qi))

    @pl.when(jj == COL_MISC // PROJ_TN)
    def _():
        acc = pend_scr[...]
        y = acc[:, :LANES]
        lane = lax.broadcasted_iota(I32, y.shape, 1)
        is_ki = lane < IDX_DIM
        mu = jnp.sum(jnp.where(is_ki, y, 0.0), axis=-1, keepdims=True) * (1.0 / IDX_DIM)
        xc = y - mu
        var = jnp.sum(jnp.where(is_ki, xc * xc, 0.0), axis=-1, keepdims=True) * (1.0 / IDX_DIM)
        ln = xc * lax.rsqrt(var + NORM_EPS) * gk_ref[...] + bk_ref[...]
        ki = _rope(ln, c64_ref[...], s64_ref[...], IDX_DIM // 8, IDX_DIM)
        wi = y * (IDX_HEADS ** -0.5 * IDX_DIM ** -0.5)
        out = jnp.where(is_ki, ki, wi)
        misc_ref[...] = out
        miscb_ref[...] = out.astype(BF16)
        slab_ref[...] = jnp.concatenate(
            [out.astype(BF16), jnp.zeros((out.shape[0], PROJ_TN - LANES), BF16)], axis=1)


def _proj(x, g, w_perm, tabs, gk, bk, tm):
    rows = x.shape[0]
    c128, s128, c64, s64 = tabs
    n_col = SLAB_COLS // PROJ_TN
    row_spec = lambda w: pl.BlockSpec((tm, w), lambda i, j: (i, 0))
    vec_spec = lambda w: pl.BlockSpec((1, w), lambda i, j: (0, 0))
    head_rows_spec = pl.BlockSpec((tm * A_KV_HEADS, HEAD_DIM), lambda i, j: (i, 0))
    head_rows = jax.ShapeDtypeStruct((rows * A_KV_HEADS, HEAD_DIM), F32)
    return pl.pallas_call(
        _proj_kernel,
        grid=(rows // tm, n_col + 1),
        in_specs=[row_spec(D_MODEL), vec_spec(D_MODEL),
                  pl.BlockSpec((D_MODEL, PROJ_TN), lambda i, j: (0, jnp.minimum(j, n_col - 1))),
                  row_spec(LANES), row_spec(LANES), row_spec(LANES), row_spec(LANES),
                  vec_spec(LANES), vec_spec(LANES)],
        out_specs=[pl.BlockSpec((tm, PROJ_TN), lambda i, j: (i, jnp.maximum(j - 1, 0))),
                   head_rows_spec, head_rows_spec, head_rows_spec, head_rows_spec,
                   row_spec(LANES), row_spec(LANES)],
        out_shape=[jax.ShapeDtypeStruct((rows, SLAB_COLS), BF16),
                   head_rows, head_rows, head_rows, head_rows,
                   jax.ShapeDtypeStruct((rows, LANES), F32),
                   jax.ShapeDtypeStruct((rows, LANES), BF16)],
        scratch_shapes=[pltpu.VMEM((tm, D_MODEL), BF16), pltpu.VMEM((tm, PROJ_TN), F32)],
        name="proj",
        compiler_params=pltpu.CompilerParams(
            dimension_semantics=("parallel", "arbitrary"), vmem_limit_bytes=VMEM_LIMIT),
    )(x, g, w_perm, c128, s128, c64, s64, gk, bk)


def _rope_tables(pos):
    posf = pos.astype(F32)[:, None]

    def tables(dh):
        rot = dh // 4
        half = rot // 2
        inv = jnp.power(jnp.float32(ROPE_THETA), -jnp.arange(half, dtype=F32) * (2.0 / rot))
        ang = posf * inv[None, :]
        cos, sin = jnp.cos(ang), jnp.sin(ang)
        ones = jnp.ones((pos.shape[0], dh - rot), F32)
        c = jnp.concatenate([cos, cos, ones], axis=1)
        s = jnp.concatenate([-sin, sin, 0.0 * ones], axis=1)
        reps = LANES // dh
        return jnp.tile(c, (1, reps)), jnp.tile(s, (1, reps))

    c128, s128 = tables(HEAD_DIM)
    c64, s64 = tables(IDX_DIM)
    return c128, s128, c64, s64


def _permute_w_in(w_in):
    sizes = (A_Q_COLS, A_KV_COLS, A_KV_COLS, I_Q_COLS, IDX_HEADS, IDX_DIM, B_Q_COLS, B_KV_COLS, B_KV_COLS)
    offs = [0]
    for s in sizes:
        offs.append(offs[-1] + s)
    qa, ka, va, qi, wi, ki, qb, kb, vb = [w_in[:, offs[n]:offs[n + 1]] for n in range(len(sizes))]
    pad = jnp.zeros((w_in.shape[0], PROJ_TN - IDX_DIM - IDX_HEADS), w_in.dtype)
    return jnp.concatenate([qa, ka, va, qi, qb, kb, vb, ki, wi, pad], axis=1).astype(BF16)


TOPK_GROUPS = 256


def _sort_key(score):
    score = jnp.where(score == 0.0, 0.0, score)
    bits = pltpu.bitcast(score, I32)
    return bits ^ ((bits >> 31) & 0x7FFFFFFF)


def _fold_lanes(op, x, width):
    parts = [x[:, u * width:(u + 1) * width] for u in range(x.shape[1] // width)]
    while len(parts) > 1:
        nxt = [op(parts[u], parts[u + 1]) for u in range(0, len(parts) - 1, 2)]
        if len(parts) % 2:
            nxt.append(parts[-1])
        parts = nxt
    return parts[0]


def _lane_allreduce(op, x):
    shift = LANES // 2
    while shift >= 1:
        x = op(x, pltpu.roll(x, shift, axis=1))
        shift //= 2
    return x


def _count(keys_ref, nch, cw, pred):
    rows = keys_ref.shape[0]

    def body(c, acc):
        off = pl.multiple_of(c * cw, cw)
        blk = keys_ref[:, pl.ds(off, cw)]
        idx = off + lax.broadcasted_iota(I32, blk.shape, 1)
        return acc + _fold_lanes(jnp.add, jnp.where(pred(blk, idx), 1.0, 0.0), LANES)

    acc = lax.fori_loop(0, nch, body, jnp.zeros((rows, LANES), F32))
    return jnp.sum(acc, axis=1, keepdims=True)


def _row_bounds(keys_ref, nch, cw):
    rows = keys_ref.shape[0]

    def body(c, gmax):
        off = pl.multiple_of(c * cw, cw)
        return jnp.maximum(gmax, _fold_lanes(jnp.maximum, keys_ref[:, pl.ds(off, cw)], TOPK_GROUPS))

    gmax = lax.fori_loop(0, nch, body, jnp.full((rows, TOPK_GROUPS), INT_MIN, I32))
    lo = _lane_allreduce(jnp.minimum, _fold_lanes(jnp.minimum, gmax, LANES))[:, :1]
    top = _lane_allreduce(jnp.maximum, _fold_lanes(jnp.maximum, gmax, LANES))[:, :1]
    return lo, top


def _select_threshold(keys_ref, nch, cw, topk, idx_bits):
    rows = keys_ref.shape[0]
    kf = float(topk)
    lo, top = _row_bounds(keys_ref, nch, cw)
    hi = top + 1

    def n_open(lo, hi):
        return jnp.max(jnp.where(hi - 1 > lo, 1.0, 0.0))

    def cond(st):
        return (st[2] > 0.5) & (st[3] < 40)

    def body(st):
        lo, hi, _, it = st
        mid = (lo & hi) + ((lo ^ hi) >> 1)
        cnt = _count(keys_ref, nch, cw, lambda k, i: k >= mid)
        ge = cnt >= kf
        lo = jnp.where(ge, mid, lo)
        hi = jnp.where(cnt == kf, mid + 1, jnp.where(ge, hi, mid))
        return lo, hi, n_open(lo, hi), it + 1

    tau, _, _, _ = lax.while_loop(cond, body, (lo, hi, n_open(lo, hi), jnp.int32(0)))
    n_ge = _count(keys_ref, nch, cw, lambda k, i: k >= tau)
    has_k = tau > INT_MIN
    tied = has_k & (n_ge > kf)
    any_tied = jnp.max(jnp.where(tied, 1.0, 0.0)) > 0.5

    def resolve(_):
        need = kf - _count(keys_ref, nch, cw, lambda k, i: k > tau)
        lim = jnp.zeros((rows, 1), I32)
        for b in range(idx_bits - 1, -1, -1):
            cand = lim + (1 << b)
            cnt = _count(keys_ref, nch, cw, lambda k, i: (k == tau) & (i < cand))
            lim = jnp.where(cnt < need, cand, lim)
        return lim

    lim = lax.cond(any_tied, resolve, lambda _: jnp.zeros((rows, 1), I32), 0)
    jlim = jnp.where(tied, lim, jnp.where(has_k, jnp.int32(2 ** 31 - 1), jnp.int32(-1)))
    return tau, jlim


def _keys_to_bias(keys_ref, bias_ref, nch, cw, tau, jlim):
    def body(c, carry):
        off = pl.multiple_of(c * cw, cw)
        key = keys_ref[:, pl.ds(off, cw)]
        idx = off + lax.broadcasted_iota(I32, key.shape, 1)
        sel = (key > tau) | ((key == tau) & (idx <= jlim))
        bias_ref[:, pl.ds(off, cw)] = pltpu.bitcast(jnp.where(sel, 0.0, NEG), I32)
        return carry

    lax.fori_loop(0, nch, body, 0)


def _flash_update(sl, d, v, m_scr, acc_scr):
    reps = d.shape[1] // LANES
    m_old = m_scr[sl]
    m_new = jnp.maximum(m_old, jnp.max(d, axis=1, keepdims=True))
    alpha = jnp.exp2(m_old - m_new)
    p = jnp.exp2(d - jnp.tile(m_new, (1, reps)))
    v_ones = jnp.concatenate([v, jnp.ones(v.shape, BF16)], axis=1)
    acc_scr[sl] = jnp.tile(alpha, (1, 2)) * acc_scr[sl] + jnp.dot(p.astype(BF16), v_ones,
                                                                   preferred_element_type=F32)
    m_scr[sl] = m_new


def _flash_init(m_scr, acc_scr):
    m_scr[...] = jnp.full(m_scr.shape, NEG, F32)
    acc_scr[...] = jnp.zeros(acc_scr.shape, F32)


def _flash_result(acc, guard=False):
    l = acc[:, HEAD_DIM:]
    return acc[:, :HEAD_DIM] / (jnp.maximum(l, 1e-30) if guard else l)


def _causal_steps(n_q, chunks_of):
    qb, cb = [], []
    for i in range(n_q):
        for c in range(chunks_of(i)):
            qb.append(i)
            cb.append(c)
    return jnp.asarray(qb, I32), jnp.asarray(cb, I32)


IDX_TN = 1024


def _dsa_prompt_kernel(qb_ref, cb_ref, qi_ref, misc_ref, kit_ref, qa_ref, k_ref, v_ref, o_ref,
                       keys_scr, qih_scr, wib_scr, d_scr, m_scr, acc_scr,
                       *, tq, tk, topk, idx_bits):
    step = pl.program_id(0)
    i = qb_ref[step]
    c = cb_ref[step]
    t0 = i * tq
    nch = (t0 + tq + tk - 1) // tk
    rt = 16
    n_chunks = keys_scr.shape[1] // tk
    scan = max(w for w in (1, 2) if n_chunks % w == 0)

    @pl.when(c == 0)
    def _():
        for hh in range(IDX_HEADS):
            qih_scr[hh] = qi_ref[:, hh * IDX_DIM:(hh + 1) * IDX_DIM]
            wib_scr[hh] = jnp.broadcast_to(misc_ref[:, MISC_WI + hh:MISC_WI + hh + 1], (tq, LANES))
        qpos = t0 + lax.broadcasted_iota(I32, (tq, 1), 0)

        def sub_body(kc, carry):
            off = pl.multiple_of(kc * IDX_TN, IDX_TN)
            d_scr[...] = jnp.dot(qih_scr[...].reshape(IDX_HEADS * tq, IDX_DIM),
                                 kit_ref[:, pl.ds(off, IDX_TN)], preferred_element_type=F32)
            kpos = off + lax.broadcasted_iota(I32, (rt, IDX_TN), 1)
            for r in range(tq // rt):
                score = jnp.zeros((rt, IDX_TN), F32)
                for hh in range(IDX_HEADS):
                    w = wib_scr[hh, r * rt:(r + 1) * rt, :]
                    dd = d_scr[hh * tq + r * rt:hh * tq + (r + 1) * rt, :]
                    score = score + jnp.tile(w, (1, IDX_TN // LANES)) * jnp.maximum(dd, 0.0)
                keys_scr[r * rt:(r + 1) * rt, pl.ds(off, IDX_TN)] = jnp.where(
                    kpos <= qpos[r * rt:(r + 1) * rt], _sort_key(score), INT_MIN)
            return carry

        lax.fori_loop(0, nch * (tk // IDX_TN), sub_body, 0)

        n_scan = (nch + scan - 1) // scan

        def pad_body(kc, carry):
            keys_scr[:, pl.ds(pl.multiple_of(kc * tk, tk), tk)] = jnp.full((tq, tk), INT_MIN, I32)
            return carry

        lax.fori_loop(nch, n_scan * scan, pad_body, 0)
        tau, jl = _select_threshold(keys_scr, n_scan, scan * tk, topk, idx_bits)
        _keys_to_bias(keys_scr, keys_scr, n_scan, scan * tk, tau, jl)
        _flash_init(m_scr, acc_scr)

    bias = pltpu.bitcast(keys_scr[:, pl.ds(pl.multiple_of(c * tk, tk), tk)], F32)
    group = A_HEADS // A_KV_HEADS

    def scores(h):
        kv = h // group
        return _nt_dot(qa_ref[:, h * HEAD_DIM:(h + 1) * HEAD_DIM],
                       k_ref[:, kv * HEAD_DIM:(kv + 1) * HEAD_DIM]) + bias

    d_next = scores(0)
    for h in range(A_HEADS):
        d = d_next
        if h + 1 < A_HEADS:
            d_next = scores(h + 1)
        kv = h // group
        _flash_update(h, d, v_ref[:, kv * HEAD_DIM:(kv + 1) * HEAD_DIM], m_scr, acc_scr)

    @pl.when(c == nch - 1)
    def _():
        for h in range(A_HEADS):
            o_ref[:, h * HEAD_DIM:(h + 1) * HEAD_DIM] = _flash_result(acc_scr[h]).astype(o_ref.dtype)


def _dsa_prompt(slab, misc, kit, tq, tk):
    rows = slab.shape[0]
    topk = min(DSA_TOPK, rows // 4)
    qb, cb = _causal_steps(rows // tq, lambda i: (i * tq + tq + tk - 1) // tk)
    kern = functools.partial(_dsa_prompt_kernel, tq=tq, tk=tk, topk=topk,
                             idx_bits=max(1, (rows - 1).bit_length()))
    grid_spec = pltpu.PrefetchScalarGridSpec(
        num_scalar_prefetch=2,
        grid=(int(qb.shape[0]),),
        in_specs=[pl.BlockSpec((tq, I_Q_COLS), lambda s, qb, cb: (qb[s], COL_QI // I_Q_COLS)),
                  pl.BlockSpec((tq, LANES), lambda s, qb, cb: (qb[s], 0)),
                  pl.BlockSpec((IDX_DIM, rows), lambda s, qb, cb: (0, 0)),
                  pl.BlockSpec((tq, A_Q_COLS), lambda s, qb, cb: (qb[s], COL_QA // A_Q_COLS)),
                  pl.BlockSpec((tk, A_KV_COLS), lambda s, qb, cb: (cb[s], COL_KA // A_KV_COLS)),
                  pl.BlockSpec((tk, A_KV_COLS), lambda s, qb, cb: (cb[s], COL_VA // A_KV_COLS))],
        out_specs=pl.BlockSpec((tq, A_Q_COLS), lambda s, qb, cb: (qb[s], 0)),
        scratch_shapes=[pltpu.VMEM((tq, rows), I32),
                        pltpu.VMEM((IDX_HEADS, tq, IDX_DIM), BF16),
                        pltpu.VMEM((IDX_HEADS, tq, LANES), F32),
                        pltpu.VMEM((IDX_HEADS * tq, IDX_TN), F32),
                        pltpu.VMEM((A_HEADS, tq, LANES), F32),
                        pltpu.VMEM((A_HEADS, tq, 2 * HEAD_DIM), F32)])
    return pl.pallas_call(
        kern, grid_spec=grid_spec, name="dsa_prompt",
        out_shape=jax.ShapeDtypeStruct((rows, A_Q_COLS), BF16),
        compiler_params=pltpu.CompilerParams(
            dimension_semantics=("arbitrary",), vmem_limit_bytes=VMEM_LIMIT),
    )(qb, cb, slab, misc, kit, slab, slab, slab)


def _gate_topk(gate, n_valid, ksel):
    lane = lax.broadcasted_iota(I32, gate.shape, 1)
    lanef = lane.astype(F32)
    gate = jnp.where(lane < n_valid, gate, NEG)
    sel = jnp.zeros(gate.shape, F32)
    for _ in range(ksel):
        mx = jnp.max(gate, axis=1, keepdims=True)
        first = jnp.min(jnp.where(gate == mx, lanef, 1e9), axis=1, keepdims=True)
        hit = (lanef == first) & (mx > 0.5 * NEG)
        sel = jnp.where(hit, 1.0, sel)
        gate = jnp.where(hit, NEG, gate)
    return sel


def _block_bias_rows(sel):
    return ((sel - 1.0) * (-NEG)).astype(BF16)


def _block_expansion(n_keys, blk0, keys_on_rows):
    shape = (n_keys, LANES) if keys_on_rows else (LANES, n_keys)
    kdim, bdim = (0, 1) if keys_on_rows else (1, 0)
    kblk = lax.broadcasted_iota(I32, shape, kdim) // MOBA_BLOCK + blk0
    return jnp.where(kblk == lax.broadcasted_iota(I32, shape, bdim), 1.0, 0.0).astype(BF16)


def _block_means_kernel(k_ref, o_ref):
    x = k_ref[...]
    s8 = jnp.sum(x.reshape(x.shape[0] // SUBLANES, SUBLANES, HEAD_DIM), axis=0)
    o_ref[0] = (s8[:B_KV_HEADS] + s8[B_KV_HEADS:]) * (1.0 / MOBA_BLOCK)


def _block_means(kb_rows):
    blk_rows = MOBA_BLOCK * B_KV_HEADS
    nb = kb_rows.shape[0] // blk_rows
    out = pl.pallas_call(
        _block_means_kernel,
        grid=(nb,),
        in_specs=[pl.BlockSpec((blk_rows, HEAD_DIM), lambda n: (n, 0))],
        out_specs=pl.BlockSpec((1, B_KV_HEADS, HEAD_DIM), lambda n: (n, 0, 0)),
        out_shape=jax.ShapeDtypeStruct((nb, B_KV_HEADS, HEAD_DIM), F32),
        name="block_means",
    )(kb_rows)
    return out.reshape(nb, B_KV_COLS)


def _moba_prompt_kernel(qb_ref, cb_ref, q_ref, means_ref, k_ref, v_ref, o_ref,
                        qaug_scr, m_scr, acc_scr, *, tq, tk, ksel):
    step = pl.program_id(0)
    i = qb_ref[step]
    c = cb_ref[step]
    t0 = i * tq
    qblk = t0 // MOBA_BLOCK
    bpc = tk // MOBA_BLOCK
    nch = qblk // bpc + 1
    group = B_HEADS // B_KV_HEADS

    @pl.when(c == 0)
    def _():
        lane = lax.broadcasted_iota(I32, (tq, LANES), 1)
        for h in range(B_HEADS):
            kv = h // group
            q = q_ref[:, h * HEAD_DIM:(h + 1) * HEAD_DIM]
            gate = _nt_dot(q, means_ref[:, kv * HEAD_DIM:(kv + 1) * HEAD_DIM])
            sel = jnp.where(lane == qblk, 1.0, _gate_topk(gate, qblk, ksel))
            qaug_scr[h] = jnp.concatenate([q, _block_bias_rows(sel)], axis=1)
        _flash_init(m_scr, acc_scr)

    e_t = _block_expansion(tk, c * bpc, keys_on_rows=True)
    qlim = t0 + lax.broadcasted_iota(I32, (tq, 1), 0) + jnp.where(c == nch - 1, 0, 2 ** 30)
    kpos = c * tk + lax.broadcasted_iota(I32, (tq, tk), 1)
    causal = jnp.where(kpos <= qlim, 0.0, NEG)
    for kv in range(B_KV_HEADS):
        rhs = jnp.concatenate([k_ref[:, kv * HEAD_DIM:(kv + 1) * HEAD_DIM], e_t], axis=1)
        vv = v_ref[:, kv * HEAD_DIM:(kv + 1) * HEAD_DIM]
        for g in range(group):
            h = kv * group + g
            _flash_update(h, _nt_dot(qaug_scr[h], rhs) + causal, vv, m_scr, acc_scr)

    @pl.when(c == nch - 1)
    def _():
        for h in range(B_HEADS):
            o_ref[:, h * HEAD_DIM:(h + 1) * HEAD_DIM] = _flash_result(acc_scr[h]).astype(o_ref.dtype)


def _moba_prompt(slab, means_pad, tq, tk):
    rows = slab.shape[0]
    ksel = min(MOBA_TOPK, (rows - 1) // MOBA_BLOCK)
    assert tq == MOBA_BLOCK and tk % MOBA_BLOCK == 0 and means_pad.shape[0] == LANES
    qb, cb = _causal_steps(rows // tq, lambda i: (i * tq) // tk + 1)
    kern = functools.partial(_moba_prompt_kernel, tq=tq, tk=tk, ksel=ksel)
    grid_spec = pltpu.PrefetchScalarGridSpec(
        num_scalar_prefetch=2,
        grid=(int(qb.shape[0]),),
        in_specs=[pl.BlockSpec((tq, B_Q_COLS), lambda s, qb, cb: (qb[s], COL_QB // B_Q_COLS)),
                  pl.BlockSpec((LANES, B_KV_COLS), lambda s, qb, cb: (0, 0)),
                  pl.BlockSpec((tk, B_KV_COLS), lambda s, qb, cb: (cb[s], COL_KB // B_KV_COLS)),
                  pl.BlockSpec((tk, B_KV_COLS), lambda s, qb, cb: (cb[s], COL_VB // B_KV_COLS))],
        out_specs=pl.BlockSpec((tq, B_Q_COLS), lambda s, qb, cb: (qb[s], 0)),
        scratch_shapes=[pltpu.VMEM((B_HEADS, tq, 2 * HEAD_DIM), BF16),
                        pltpu.VMEM((B_HEADS, tq, LANES), F32),
                        pltpu.VMEM((B_HEADS, tq, 2 * HEAD_DIM), F32)])
    return pl.pallas_call(
        kern, grid_spec=grid_spec, name="moba_prompt",
        out_shape=jax.ShapeDtypeStruct((rows, B_Q_COLS), BF16),
        compiler_params=pltpu.CompilerParams(
            dimension_semantics=("arbitrary",), vmem_limit_bytes=VMEM_LIMIT),
    )(qb, cb, slab, means_pad, slab, slab)


TPAD = SUBLANES
SROWS = 8 * TPAD
KV_ROWS = SROWS // 4
PAGE_ROWS = PAGE_SIZE * 4


def _page_head(ref, kv):
    return ref[0, pl.ds(kv, PAGE_SIZE, stride=4), :]


def _paged_spec(shape, n_pages, per_step, first_step, n_steps, j):
    def index(b, s, pt):
        local = jnp.clip(s - first_step, 0, n_steps - 1)
        return (pt[b * n_pages + local * per_step + j], 0, 0)
    return pl.BlockSpec(shape, index)


def _dsa_sample_index_kernel(pt_ref, qi_ref, wi_ref, *rest, n_pages, pa, t_new):
    kidx_refs = rest[:pa]
    kin_ref, keys_ref, wib_scr = rest[pa:]
    keys_scr = keys_ref.at[0]
    s = pl.program_id(1)
    na = n_pages // pa
    past = n_pages * PAGE_SIZE
    ntot = keys_scr.shape[1]
    trow = lax.broadcasted_iota(I32, (TPAD, 1), 0)

    def index_keys(kd, off):
        width = kd.shape[1]
        x = jnp.tile(wib_scr[...], (1, width // LANES)) * jnp.maximum(
            jnp.dot(qi_ref[0], kd, preferred_element_type=F32), 0.0)
        score = x[:TPAD]
        for hh in range(1, IDX_HEADS):
            score = score + x[hh * TPAD:(hh + 1) * TPAD]
        kpos = off + lax.broadcasted_iota(I32, (TPAD, width), 1)
        visible = (kpos <= past + trow) & (trow < t_new)
        keys_scr[:, pl.ds(off, width)] = jnp.where(visible, _sort_key(score), INT_MIN)

    @pl.when(s == 0)
    def _():
        wib_scr[...] = jnp.broadcast_to(wi_ref[0], wib_scr.shape)
        keys_scr[:, past:] = jnp.full((TPAD, ntot - past), INT_MIN, I32)

    @pl.when(s < na)
    def _():
        kd = jnp.concatenate([r[0].astype(BF16) for r in kidx_refs], axis=1)
        index_keys(kd, pl.multiple_of(s * (pa * PAGE_SIZE), pa * PAGE_SIZE))

    @pl.when(s == na - 1)
    def _():
        index_keys(kin_ref[0], past)


def _topk_bias_kernel(keys_ref, bias_ref, *, cw, topk, idx_bits):
    nch = keys_ref.shape[1] // cw
    tau, jl = _select_threshold(keys_ref, nch, cw, topk, idx_bits)
    _keys_to_bias(keys_ref, bias_ref, nch, cw, tau, jl)


def _dsa_sample_attend_kernel(pt_ref, bias_ref, qa_ref, *rest, n_pages, pb):
    k_refs, v_refs = rest[:pb], rest[pb:2 * pb]
    kn_ref, vn_ref, o_ref, m_scr, acc_scr = rest[2 * pb:]
    keys_scr = bias_ref.at[0]
    s = pl.program_id(1)
    past = n_pages * PAGE_SIZE

    @pl.when(s == 0)
    def _():
        _flash_init(m_scr, acc_scr)

    def attend(key_of, val_of, off, width):
        bias = pltpu.bitcast(keys_scr[:, pl.ds(off, width)], F32)
        bias = jnp.concatenate([bias] * (KV_ROWS // TPAD), axis=0)
        for kv in range(A_KV_HEADS):
            rows = slice(kv * KV_ROWS, (kv + 1) * KV_ROWS)
            _flash_update(rows, _nt_dot(qa_ref[0, rows], key_of(kv)) + bias, val_of(kv), m_scr, acc_scr)

    def gather(refs, kv):
        return jnp.concatenate([_page_head(r, kv) for r in refs], axis=0).astype(BF16)

    attend(lambda kv: gather(k_refs, kv), lambda kv: gather(v_refs, kv),
           pl.multiple_of(s * (pb * PAGE_SIZE), pb * PAGE_SIZE), pb * PAGE_SIZE)

    @pl.when(s == n_pages // pb - 1)
    def _():
        attend(lambda kv: kn_ref[0, kv], lambda kv: vn_ref[0, kv], past, PAGE_SIZE)
        o_ref[0] = _flash_result(acc_scr[...], guard=True)


def _dsa_sample(pt_flat, n_pages, t_new, qi_s, wi_s, kidx_pages, ki_new_t, qa_s, k_pages, v_pages, ka_new, va_new,
                pa, pb):
    ns = qi_s.shape[0]
    past = n_pages * PAGE_SIZE
    topk = min(DSA_TOPK, (past + t_new) // 4)
    cw = 1024
    ntot = -(-(past + PAGE_SIZE) // cw) * cw
    na, nb = n_pages // pa, n_pages // pb
    seq3 = lambda s1, s2: pl.BlockSpec((1, s1, s2), lambda b, s, pt: (b, 0, 0))
    seq4 = pl.BlockSpec((1, A_KV_HEADS, PAGE_SIZE, HEAD_DIM), lambda b, s, pt: (b, 0, 0, 0))
    params = pltpu.CompilerParams(dimension_semantics=("arbitrary", "arbitrary"), vmem_limit_bytes=VMEM_LIMIT)

    keys = pl.pallas_call(
        functools.partial(_dsa_sample_index_kernel, n_pages=n_pages, pa=pa, t_new=t_new),
        grid_spec=pltpu.PrefetchScalarGridSpec(
            num_scalar_prefetch=1,
            grid=(ns, na),
            in_specs=[seq3(IDX_HEADS * TPAD, IDX_DIM), seq3(IDX_HEADS * TPAD, 1)]
                     + [_paged_spec((1, IDX_DIM, PAGE_SIZE), n_pages, pa, 0, na, j) for j in range(pa)]
                     + [seq3(IDX_DIM, PAGE_SIZE)],
            out_specs=seq3(TPAD, ntot),
            scratch_shapes=[pltpu.VMEM((IDX_HEADS * TPAD, LANES), F32)]),
        out_shape=jax.ShapeDtypeStruct((ns, TPAD, ntot), I32),
        name="dsa_sample_index", compiler_params=params,
    )(pt_flat, qi_s, wi_s, *([kidx_pages] * pa), ki_new_t)

    rows = ns * t_new
    rows_pad = -(-rows // SUBLANES) * SUBLANES
    keys2d = _pad_rows(keys[:, :t_new].reshape(rows, ntot), 0, rows_pad)
    bias2d = pl.pallas_call(
        functools.partial(_topk_bias_kernel, cw=cw, topk=topk, idx_bits=max(1, (ntot - 1).bit_length())),
        out_shape=jax.ShapeDtypeStruct((rows_pad, ntot), I32),
        name="dsa_sample_select", compiler_params=pltpu.CompilerParams(vmem_limit_bytes=VMEM_LIMIT),
    )(keys2d)
    bias = _pad_rows(bias2d[:rows].reshape(ns, t_new, ntot), 1, TPAD)

    page_kv = lambda j: _paged_spec((1, PAGE_ROWS, HEAD_DIM), n_pages, pb, 0, nb, j)
    return pl.pallas_call(
        functools.partial(_dsa_sample_attend_kernel, n_pages=n_pages, pb=pb),
        grid_spec=pltpu.PrefetchScalarGridSpec(
            num_scalar_prefetch=1,
            grid=(ns, nb),
            in_specs=[seq3(TPAD, ntot), seq3(SROWS, HEAD_DIM)]
                     + [page_kv(j) for j in range(pb)] + [page_kv(j) for j in range(pb)]
                     + [seq4, seq4],
            out_specs=seq3(SROWS, HEAD_DIM),
            scratch_shapes=[pltpu.VMEM((SROWS, LANES), F32),
                            pltpu.VMEM((SROWS, 2 * HEAD_DIM), F32)]),
        out_shape=jax.ShapeDtypeStruct((ns, SROWS, HEAD_DIM), F32),
        name="dsa_sample_attend", compiler_params=params,
    )(pt_flat, bias, qa_s, *([k_pages] * pb), *([v_pages] * pb), ka_new, va_new)


def _moba_sample_kernel(pt_ref, q_ref, *rest, n_pages, pb, t_new, ksel):
    k_refs, v_refs = rest[:pb], rest[pb:2 * pb]
    kn_ref, vn_ref, o_ref, bsum_scr, s_scr, seln_scr, m_scr, acc_scr = rest[2 * pb:]
    s = pl.program_id(1)
    nst = n_pages // pb
    past = n_pages * PAGE_SIZE
    nbp = past // MOBA_BLOCK
    bps = pb * PAGE_SIZE // MOBA_BLOCK
    width = pb * PAGE_SIZE
    tok = lax.broadcasted_iota(I32, (KV_ROWS, 1), 0) % TPAD
    pad_bias = jnp.where(tok < t_new, 0.0, NEG)
    kv_rows = lambda kv: slice(kv * KV_ROWS, (kv + 1) * KV_ROWS)

    @pl.when(s == 0)
    def _():
        bsum_scr[...] = jnp.zeros(bsum_scr.shape, F32)

    @pl.when(s < nst)
    def _():
        off = pl.multiple_of(s * width, width)
        for kv in range(B_KV_HEADS):
            kf = jnp.concatenate([_page_head(r, kv) for r in k_refs], axis=0)
            bsum_scr[kv, pl.ds(pl.multiple_of(s * bps, bps), bps), :] = jnp.sum(
                kf.reshape(bps, MOBA_BLOCK, HEAD_DIM), axis=1)
            s_scr[kv_rows(kv), pl.ds(off, width)] = _nt_dot(q_ref[0, kv_rows(kv)], kf.astype(BF16))

    @pl.when(s == nst - 1)
    def _():
        for kv in range(B_KV_HEADS):
            q = q_ref[0, kv_rows(kv)]
            s_scr[kv_rows(kv), pl.ds(past, PAGE_SIZE)] = _nt_dot(q, kn_ref[0, kv])
            means = (bsum_scr[kv] * (1.0 / MOBA_BLOCK)).astype(BF16)
            seln_scr[kv_rows(kv)] = _block_bias_rows(_gate_topk(_nt_dot(q, means), nbp, ksel))
        _flash_init(m_scr, acc_scr)

    @pl.when(s >= nst)
    def _():
        sb = s - nst
        off = pl.multiple_of(sb * width, width)
        expand = _block_expansion(width, sb * bps, keys_on_rows=False)
        for kv in range(B_KV_HEADS):
            bias = jnp.dot(seln_scr[kv_rows(kv)], expand, preferred_element_type=F32) + pad_bias
            vv = jnp.concatenate([_page_head(r, kv) for r in v_refs], axis=0).astype(BF16)
            _flash_update(kv_rows(kv), s_scr[kv_rows(kv), pl.ds(off, width)] + bias, vv, m_scr, acc_scr)

    @pl.when(s == 2 * nst - 1)
    def _():
        kpos = past + lax.broadcasted_iota(I32, (KV_ROWS, PAGE_SIZE), 1)
        bias = jnp.where(kpos <= past + tok, 0.0, NEG) + pad_bias
        for kv in range(B_KV_HEADS):
            _flash_update(kv_rows(kv), s_scr[kv_rows(kv), pl.ds(past, PAGE_SIZE)] + bias, vn_ref[0, kv],
                          m_scr, acc_scr)
        o_ref[0] = _flash_result(acc_scr[...], guard=True)


def _moba_sample(pt_flat, n_pages, t_new, qb_s, k_pages, v_pages, kb_new, vb_new, pb):
    ns = qb_s.shape[0]
    past = n_pages * PAGE_SIZE
    nbp = past // MOBA_BLOCK
    assert nbp <= LANES and (pb * PAGE_SIZE // MOBA_BLOCK) % SUBLANES == 0
    ksel = min(MOBA_TOPK, nbp)
    nst = n_pages // pb
    seq3 = lambda s1, s2: pl.BlockSpec((1, s1, s2), lambda b, s, pt: (b, 0, 0))
    seq4 = pl.BlockSpec((1, B_KV_HEADS, PAGE_SIZE, HEAD_DIM), lambda b, s, pt: (b, 0, 0, 0))
    kern = functools.partial(_moba_sample_kernel, n_pages=n_pages, pb=pb, t_new=t_new, ksel=ksel)
    grid_spec = pltpu.PrefetchScalarGridSpec(
        num_scalar_prefetch=1,
        grid=(ns, 2 * nst),
        in_specs=[seq3(SROWS, HEAD_DIM)]
                 + [_paged_spec((1, PAGE_ROWS, HEAD_DIM), n_pages, pb, 0, nst, j) for j in range(pb)]
                 + [_paged_spec((1, PAGE_ROWS, HEAD_DIM), n_pages, pb, nst, nst, j) for j in range(pb)]
                 + [seq4, seq4],
        out_specs=seq3(SROWS, HEAD_DIM),
        scratch_shapes=[pltpu.VMEM((B_KV_HEADS, LANES, HEAD_DIM), F32),
                        pltpu.VMEM((SROWS, past + PAGE_SIZE), F32),
                        pltpu.VMEM((SROWS, LANES), BF16),
                        pltpu.VMEM((SROWS, LANES), F32),
                        pltpu.VMEM((SROWS, 2 * HEAD_DIM), F32)])
    return pl.pallas_call(
        kern, grid_spec=grid_spec, name="moba_sample",
        out_shape=jax.ShapeDtypeStruct((ns, SROWS, HEAD_DIM), F32),
        compiler_params=pltpu.CompilerParams(
            dimension_semantics=("arbitrary", "arbitrary"), vmem_limit_bytes=VMEM_LIMIT),
    )(pt_flat, qb_s, *([k_pages] * pb), *([v_pages] * pb), kb_new, vb_new)


def _memory_kv_kernel(mem_ref, g_ref, w_ref, k_ref, v_ref):
    h = _rms(mem_ref[...], g_ref[...]).astype(BF16)
    kv = jnp.dot(h, w_ref[...], preferred_element_type=F32)
    k_ref[...] = kv[:, :MEM_WIDTH]
    v_ref[...] = kv[:, MEM_WIDTH:]


def _memory_kv(mem, g, w_bf):
    m = mem.shape[0]
    return pl.pallas_call(
        _memory_kv_kernel,
        out_shape=[jax.ShapeDtypeStruct((m, MEM_WIDTH), F32), jax.ShapeDtypeStruct((m, MEM_WIDTH), F32)],
        name="memory_kv",
        compiler_params=pltpu.CompilerParams(vmem_limit_bytes=VMEM_LIMIT),
    )(mem, g, w_bf)


def _outproj_cross_kernel(x_ref, oa_ref, ob_ref, woa_ref, wob_ref, g_ref, wq_ref, mk_ref, mv_ref, wo_ref,
                          y_ref, *, rows_per_seq):
    x1 = x_ref[...] + jnp.dot(oa_ref[...], woa_ref[...], preferred_element_type=F32) \
        + jnp.dot(ob_ref[...], wob_ref[...], preferred_element_type=F32)
    hc = _rms(x1, g_ref[...]).astype(BF16)
    q = jnp.dot(hc, wq_ref[...], preferred_element_type=F32).astype(BF16)
    tm = q.shape[0]
    nk = mk_ref.shape[0]
    scale = MEM_HEAD_DIM ** -0.5
    if rows_per_seq is not None:
        rseq = (pl.program_id(0) * tm + lax.broadcasted_iota(I32, (tm, 1), 0)) // rows_per_seq
        kseq = lax.broadcasted_iota(I32, (tm, nk), 1) // MEM_TOKENS
        mask = kseq == rseq
    outs = []
    for h in range(MEM_HEADS):
        sl = slice(h * MEM_HEAD_DIM, (h + 1) * MEM_HEAD_DIM)
        s = _nt_dot(q[:, sl], mk_ref[:, sl]) * scale
        if rows_per_seq is not None:
            s = jnp.where(mask, s, NEG)
        m = jnp.max(s, axis=1, keepdims=True)
        e = jnp.exp(s - m)
        p = e / jnp.sum(e, axis=1, keepdims=True)
        outs.append(jnp.dot(p.astype(BF16), mv_ref[:, sl], preferred_element_type=F32))
    o = jnp.concatenate(outs, axis=1).astype(BF16)
    y_ref[...] = x1 + jnp.dot(o, wo_ref[...], preferred_element_type=F32)


def _outproj_cross(x, oa, ob, w_out_bf, g_cross, w_mq_bf, mk_bf, mv_bf, w_mo_bf, tm, rows_per_seq):
    rows = x.shape[0]
    nk = mk_bf.shape[0]
    row_spec = lambda w: pl.BlockSpec((tm, w), lambda i: (i, 0))
    full = lambda a, b: pl.BlockSpec((a, b), lambda i: (0, 0))
    kern = functools.partial(_outproj_cross_kernel, rows_per_seq=rows_per_seq)
    return pl.pallas_call(
        kern,
        grid=(rows // tm,),
        in_specs=[row_spec(D_MODEL), row_spec(A_Q_COLS), row_spec(B_Q_COLS),
                  pl.BlockSpec((A_Q_COLS, D_MODEL), lambda i: (0, 0)),
                  pl.BlockSpec((B_Q_COLS, D_MODEL), lambda i: (1, 0)),
                  full(1, D_MODEL), full(D_MODEL, MEM_WIDTH), full(nk, MEM_WIDTH), full(nk, MEM_WIDTH),
                  full(MEM_WIDTH, D_MODEL)],
        out_specs=row_spec(D_MODEL),
        out_shape=jax.ShapeDtypeStruct((rows, D_MODEL), F32),
        name="outproj_cross",
        compiler_params=pltpu.CompilerParams(
            dimension_semantics=("parallel",), vmem_limit_bytes=VMEM_LIMIT),
    )(x, oa, ob, w_out_bf, w_out_bf, g_cross, w_mq_bf, mk_bf, mv_bf, w_mo_bf)


def _mlp_final_kernel(x_ref, g_ref, wu_ref, wd_ref, gf_ref, y_ref, h_scr, acc_scr):
    f = pl.program_id(1)

    @pl.when(f == 0)
    def _():
        h_scr[...] = _rms(x_ref[...], g_ref[...]).astype(BF16)
        acc_scr[...] = jnp.zeros(acc_scr.shape, F32)

    u = jnp.maximum(jnp.dot(h_scr[...], wu_ref[...], preferred_element_type=F32), 0.0)
    acc_scr[...] += jnp.dot((u * u).astype(BF16), wd_ref[...], preferred_element_type=F32)

    @pl.when(f == pl.num_programs(1) - 1)
    def _():
        y_ref[...] = _rms(x_ref[...] + acc_scr[...], gf_ref[...])


def _mlp_final(x, g_ffn, w_up_bf, w_down_bf, g_final, tm, tf):
    rows = x.shape[0]
    d_ff = w_up_bf.shape[1]
    return pl.pallas_call(
        _mlp_final_kernel,
        grid=(rows // tm, d_ff // tf),
        in_specs=[pl.BlockSpec((tm, D_MODEL), lambda i, f: (i, 0)),
                  pl.BlockSpec((1, D_MODEL), lambda i, f: (0, 0)),
                  pl.BlockSpec((D_MODEL, tf), lambda i, f: (0, f)),
                  pl.BlockSpec((tf, D_MODEL), lambda i, f: (f, 0)),
                  pl.BlockSpec((1, D_MODEL), lambda i, f: (0, 0))],
        out_specs=pl.BlockSpec((tm, D_MODEL), lambda i, f: (i, 0)),
        out_shape=jax.ShapeDtypeStruct((rows, D_MODEL), F32),
        scratch_shapes=[pltpu.VMEM((tm, D_MODEL), BF16), pltpu.VMEM((tm, D_MODEL), F32)],
        name="mlp_final",
        compiler_params=pltpu.CompilerParams(
            dimension_semantics=("parallel", "arbitrary"), vmem_limit_bytes=VMEM_LIMIT),
    )(x, g_ffn, w_up_bf, w_down_bf, g_final)


def _row_tile(rows, want):
    t = min(rows, want)
    while rows % t:
        t //= 2
    return t


def _pad_rows(a, axis, size):
    pad = [(0, 0)] * a.ndim
    pad[axis] = (0, size - a.shape[axis])
    return jnp.pad(a, pad)


def _heads_first(a, ns, t_new, heads, dim, t_pad):
    a = a.reshape(ns, t_new, heads, dim).transpose(0, 2, 1, 3)
    return _pad_rows(a, 2, t_pad)


def kernel(x_prompt, x_sample, cache_k_a, cache_v_a, cache_kidx, cache_k_b, cache_v_b, cache_mem_k,
           cache_mem_v, page_table, mem_prompt, g_mix, w_in, g_kidx, b_kidx, w_out, g_cross, w_mq, g_mem,
           w_mkv, w_mo, g_ffn, w_up, w_down, g_final):
    batch, s_len, _ = x_prompt.shape
    ns, t_new, _ = x_sample.shape
    depth = w_in.shape[0]
    n_pages = page_table.shape[1]
    past = n_pages * PAGE_SIZE
    n_phys = cache_k_a.shape[1]
    assert batch == 1 and depth == 1
    assert s_len % 1024 == 0 and s_len // MOBA_BLOCK <= LANES
    assert past % MOBA_BLOCK == 0 and t_new <= TPAD
    l = 0
    row = lambda v: v.reshape(1, -1)

    w_in_p = _permute_w_in(w_in[l])
    gk = row(jnp.concatenate([g_kidx[l], jnp.zeros((LANES - IDX_DIM,), F32)]))
    bk = row(jnp.concatenate([b_kidx[l], jnp.zeros((LANES - IDX_DIM,), F32)]))
    w_out_bf = w_out[l].astype(BF16)
    w_mq_bf = w_mq[l].astype(BF16)
    w_mo_bf = w_mo[l].astype(BF16)
    w_up_bf = w_up[l].astype(BF16)
    w_down_bf = w_down[l].astype(BF16)

    xp = x_prompt.reshape(s_len, D_MODEL)
    tabs_p = _rope_tables(jnp.arange(s_len))
    slab, ka, va, kb, vb, misc, misc_bf = _proj(xp, row(g_mix[l]), w_in_p, tabs_p, gk, bk, _row_tile(s_len, 512))
    kit = misc_bf[:, :IDX_DIM].T
    oa = _dsa_prompt(slab, misc, kit, tq=256, tk=1024)
    means_pad = _pad_rows(_block_means(kb).astype(BF16), 0, LANES)
    ob = _moba_prompt(slab, means_pad, tq=MOBA_BLOCK, tk=1024)
    mk, mv = _memory_kv(mem_prompt.reshape(MEM_TOKENS, D_MODEL), row(g_mem[l]), w_mkv[l].astype(BF16))
    xp2 = _outproj_cross(xp, oa, ob, w_out_bf, row(g_cross[l]), w_mq_bf, mk.astype(BF16), mv.astype(BF16),
                         w_mo_bf, _row_tile(s_len, 256), None)
    y_prompt = _mlp_final(xp2, row(g_ffn[l]), w_up_bf, w_down_bf, row(g_final), _row_tile(s_len, 512), 1024)

    rows_s = ns * t_new
    xs = x_sample.reshape(rows_s, D_MODEL)
    pos_s = jnp.tile(past + jnp.arange(t_new), ns)
    tabs_s = _rope_tables(pos_s)
    rows_pad = -(-rows_s // SUBLANES) * SUBLANES
    xs_pad = _pad_rows(xs, 0, rows_pad)
    tabs_s = tuple(_pad_rows(t, 0, rows_pad) for t in tabs_s)
    slab_s, ka_s, va_s, kb_s, vb_s, misc_s, _ = _proj(xs_pad, row(g_mix[l]), w_in_p, tabs_s, gk, bk,
                                                      _row_tile(rows_pad, 128))
    slab_s, misc_s = slab_s[:rows_s], misc_s[:rows_s]
    ka_s, va_s, kb_s, vb_s = [a[:rows_s * A_KV_HEADS] for a in (ka_s, va_s, kb_s, vb_s)]

    pt_flat = page_table.reshape(-1).astype(I32)
    index_pages, attend_pages = min(32, n_pages), min(16, n_pages)
    assert n_pages % index_pages == 0 and n_pages % attend_pages == 0
    page_view = lambda c: c[l].reshape(n_phys, PAGE_ROWS, HEAD_DIM)
    kidx_view = jnp.swapaxes(cache_kidx[l], 1, 2)
    cols = lambda c0, w: slab_s[:, c0:c0 + w]
    qi_s = _heads_first(cols(COL_QI, I_Q_COLS), ns, t_new, IDX_HEADS, IDX_DIM, TPAD)
    qi_s = qi_s.reshape(ns, IDX_HEADS * TPAD, IDX_DIM)
    wi_s = _heads_first(misc_s[:, MISC_WI:MISC_WI + IDX_HEADS], ns, t_new, IDX_HEADS, 1, TPAD)
    wi_s = wi_s.reshape(ns, IDX_HEADS * TPAD, 1)
    ki_new_t = _pad_rows(cols(COL_MISC, IDX_DIM).reshape(ns, t_new, IDX_DIM).transpose(0, 2, 1), 2, PAGE_SIZE)
    q_rows = lambda c0: _heads_first(cols(c0, A_Q_COLS), ns, t_new, A_HEADS, HEAD_DIM, TPAD).reshape(
        ns, SROWS, HEAD_DIM)
    new_kv = lambda c0: _heads_first(cols(c0, A_KV_COLS), ns, t_new, A_KV_HEADS, HEAD_DIM, PAGE_SIZE)
    oa_s = _dsa_sample(pt_flat, n_pages, t_new, qi_s, wi_s, kidx_view, ki_new_t, q_rows(COL_QA),
                       page_view(cache_k_a), page_view(cache_v_a), new_kv(COL_KA), new_kv(COL_VA),
                       index_pages, attend_pages)
    ob_s = _moba_sample(pt_flat, n_pages, t_new, q_rows(COL_QB), page_view(cache_k_b), page_view(cache_v_b),
                        new_kv(COL_KB), new_kv(COL_VB), attend_pages)
    tokens_first = lambda o: o.reshape(ns, A_HEADS, TPAD, HEAD_DIM)[:, :, :t_new].transpose(0, 2, 1, 3).reshape(
        rows_s, -1).astype(BF16)
    oa_s2 = _pad_rows(tokens_first(oa_s), 0, rows_pad)
    ob_s2 = _pad_rows(tokens_first(ob_s), 0, rows_pad)
    mk_s = cache_mem_k[l].reshape(ns * MEM_TOKENS, MEM_WIDTH).astype(BF16)
    mv_s = cache_mem_v[l].reshape(ns * MEM_TOKENS, MEM_WIDTH).astype(BF16)
    xs2 = _outproj_cross(xs_pad, oa_s2, ob_s2, w_out_bf, row(g_cross[l]), w_mq_bf, mk_s, mv_s, w_mo_bf,
                         rows_pad, t_new)
    y_sample = _mlp_final(xs2, row(g_ffn[l]), w_up_bf, w_down_bf, row(g_final), rows_pad, 512)[:rows_s]

    kv5 = lambda a, n, t: a.reshape(1, n, t, A_KV_HEADS, HEAD_DIM)
    return (y_prompt.reshape(batch, s_len, D_MODEL), y_sample.reshape(ns, t_new, D_MODEL),
            kv5(ka, batch, s_len), kv5(va, batch, s_len),
            misc[:, :IDX_DIM].reshape(1, batch, s_len, IDX_DIM),
            kv5(kb, batch, s_len), kv5(vb, batch, s_len),
            mk.reshape(1, batch, MEM_TOKENS, MEM_HEADS, MEM_HEAD_DIM),
            mv.reshape(1, batch, MEM_TOKENS, MEM_HEADS, MEM_HEAD_DIM),
            kv5(ka_s, ns, t_new), kv5(va_s, ns, t_new),
            misc_s[:, :IDX_DIM].reshape(1, ns, t_new, IDX_DIM),
            kv5(kb_s, ns, t_new), kv5(vb_s, ns, t_new))
```

```python
import functools

import jax
import jax.numpy as jnp
from jax import lax
from jax.experimental import pallas as pl
from jax.experimental.pallas import tpu as pltpu

F32 = jnp.float32
BF16 = jnp.bfloat16
I32 = jnp.int32

D_MODEL = 2048
HEAD_DIM = 128
A_HEADS = 8
A_KV_HEADS = 4
B_HEADS = 8
B_KV_HEADS = 4
IDX_HEADS = 16
IDX_DIM = 64
DSA_TOPK = 256
MOBA_BLOCK = 256
MOBA_TOPK = 3
MEM_TOKENS = 256
MEM_HEADS = 4
MEM_HEAD_DIM = 128
PAGE_SIZE = 128
ROPE_THETA = 500000.0
NORM_EPS = 1e-6

A_Q_COLS = A_HEADS * HEAD_DIM
A_KV_COLS = A_KV_HEADS * HEAD_DIM
I_Q_COLS = IDX_HEADS * IDX_DIM
B_Q_COLS = B_HEADS * HEAD_DIM
B_KV_COLS = B_KV_HEADS * HEAD_DIM
MEM_WIDTH = MEM_HEADS * MEM_HEAD_DIM

LANES = 128
SUBLANES = 8
VMEM_LIMIT = 56 * 1024 * 1024
INT_MIN = -(2 ** 31)
NEG = -1e30
LOG2E = 1.4426950408889634
SOFTMAX_EXP2_SCALE = HEAD_DIM ** -0.5 * LOG2E

PROJ_TN = 512
COL_QA, COL_KA, COL_VA, COL_QI, COL_QB, COL_KB, COL_VB, COL_MISC = 0, 1024, 1536, 2048, 3072, 4096, 4608, 5120
SLAB_COLS = COL_MISC + PROJ_TN
MISC_WI = IDX_DIM


def _nt_dot(a, b):
    return lax.dot_general(a, b, (((1,), (1,)), ((), ())), preferred_element_type=F32)


def _rms(x, g):
    return x * lax.rsqrt(jnp.mean(x * x, axis=-1, keepdims=True) + NORM_EPS) * g


def _rope(x, cos_t, sin_t, half, period):
    width = x.shape[1]
    reps = width // LANES
    c = jnp.concatenate([cos_t] * reps, axis=1) if reps > 1 else cos_t
    s = jnp.concatenate([sin_t] * reps, axis=1) if reps > 1 else sin_t
    lane = lax.broadcasted_iota(I32, x.shape, 1)
    first = (lane & (period - 1)) < half
    up = pltpu.roll(x, width - half, axis=1)
    dn = pltpu.roll(x, half, axis=1)
    return x * c + jnp.where(first, up, dn) * s


def _proj_kernel(x_ref, g_ref, w_ref, c128_ref, s128_ref, c64_ref, s64_ref, gk_ref, bk_ref,
                 slab_ref, ka_ref, va_ref, kb_ref, vb_ref, misc_ref, miscb_ref, h_scr, pend_scr):
    j = pl.program_id(1)
    jj = j - 1

    def step(epilogue):
        acc = jnp.dot(h_scr[...], w_ref[...], preferred_element_type=F32)
        if epilogue is not None:
            epilogue(pend_scr[...])
        pend_scr[...] = acc

    @pl.when(j == 0)
    def _():
        h_scr[...] = _rms(x_ref[...], g_ref[...]).astype(BF16)
        step(None)

    def rope128(v):
        return _rope(v, c128_ref[...], s128_ref[...], HEAD_DIM // 8, HEAD_DIM)

    def store_heads(ref, v):
        for kv in range(A_KV_HEADS):
            ref[pl.ds(kv, v.shape[0], stride=A_KV_HEADS), :] = v[:, kv * HEAD_DIM:(kv + 1) * HEAD_DIM]

    def finish_q128(acc):
        slab_ref[...] = (rope128(acc) * SOFTMAX_EXP2_SCALE).astype(BF16)

    def finish_k(ref):
        def fn(acc):
            r = rope128(acc)
            store_heads(ref, r)
            slab_ref[...] = r.astype(BF16)
        return fn

    def finish_v(ref):
        def fn(acc):
            store_heads(ref, acc)
            slab_ref[...] = acc.astype(BF16)
        return fn

    def finish_qi(acc):
        slab_ref[...] = _rope(acc, c64_ref[...], s64_ref[...], IDX_DIM // 8, IDX_DIM).astype(BF16)

    is_q128 = (jj == COL_QA // PROJ_TN) | (jj == COL_QA // PROJ_TN + 1) | \
              (jj == COL_QB // PROJ_TN) | (jj == COL_QB // PROJ_TN + 1)
    pl.when(is_q128)(lambda: step(finish_q128))
    pl.when(jj == COL_KA // PROJ_TN)(lambda: step(finish_k(ka_ref)))
    pl.when(jj == COL_KB // PROJ_TN)(lambda: step(finish_k(kb_ref)))
    pl.when(jj == COL_VA // PROJ_TN)(lambda: step(finish_v(va_ref)))
    pl.when(jj == COL_VB // PROJ_TN)(lambda: step(finish_v(vb_ref)))
    pl.when((jj == COL_QI // PROJ_TN) | (jj == COL_QI // PROJ_TN + 1))(lambda: step(finish_qi))

    @pl.when(jj == COL_MISC // PROJ_TN)
    def _():
        acc = pend_scr[...]
        y = acc[:, :LANES]
        lane = lax.broadcasted_iota(I32, y.shape, 1)
        is_ki = lane < IDX_DIM
        mu = jnp.sum(jnp.where(is_ki, y, 0.0), axis=-1, keepdims=True) * (1.0 / IDX_DIM)
        xc = y - mu
        var = jnp.sum(jnp.where(is_ki, xc * xc, 0.0), axis=-1, keepdims=True) * (1.0 / IDX_DIM)
        ln = xc * lax.rsqrt(var + NORM_EPS) * gk_ref[...] + bk_ref[...]
        ki = _rope(ln, c64_ref[...], s64_ref[...], IDX_DIM // 8, IDX_DIM)
        wi = y * (IDX_HEADS ** -0.5 * IDX_DIM ** -0.5)
        out = jnp.where(is_ki, ki, wi)
        misc_ref[...] = out
        miscb_ref[...] = out.astype(BF16)
        slab_ref[...] = jnp.concatenate(
            [out.astype(BF16), jnp.zeros((out.shape[0], PROJ_TN - LANES), BF16)], axis=1)


def _proj(x, g, w_perm, tabs, gk, bk, tm):
    rows = x.shape[0]
    c128, s128, c64, s64 = tabs
    n_col = SLAB_COLS // PROJ_TN
    row_spec = lambda w: pl.BlockSpec((tm, w), lambda i, j: (i, 0))
    vec_spec = lambda w: pl.BlockSpec((1, w), lambda i, j: (0, 0))
    head_rows_spec = pl.BlockSpec((tm * A_KV_HEADS, HEAD_DIM), lambda i, j: (i, 0))
    head_rows = jax.ShapeDtypeStruct((rows * A_KV_HEADS, HEAD_DIM), F32)
    return pl.pallas_call(
        _proj_kernel,
        grid=(rows // tm, n_col + 1),
        in_specs=[row_spec(D_MODEL), vec_spec(D_MODEL),
                  pl.BlockSpec((D_MODEL, PROJ_TN), lambda i, j: (0, jnp.minimum(j, n_col - 1))),
                  row_spec(LANES), row_spec(LANES), row_spec(LANES), row_spec(LANES),
                  vec_spec(LANES), vec_spec(LANES)],
        out_specs=[pl.BlockSpec((tm, PROJ_TN), lambda i, j: (i, jnp.maximum(j - 1, 0))),
                   head_rows_spec, head_rows_spec, head_rows_spec, head_rows_spec,
                   row_spec(LANES), row_spec(LANES)],
        out_shape=[jax.ShapeDtypeStruct((rows, SLAB_COLS), BF16),
                   head_rows, head_rows, head_rows, head_rows,
                   jax.ShapeDtypeStruct((rows, LANES), F32),
                   jax.ShapeDtypeStruct((rows, LANES), BF16)],
        scratch_shapes=[pltpu.VMEM((tm, D_MODEL), BF16), pltpu.VMEM((tm, PROJ_TN), F32)],
        name="proj",
        compiler_params=pltpu.CompilerParams(
            dimension_semantics=("parallel", "arbitrary"), vmem_limit_bytes=VMEM_LIMIT),
    )(x, g, w_perm, c128, s128, c64, s64, gk, bk)


def _rope_tables(pos):
    posf = pos.astype(F32)[:, None]

    def tables(dh):
        rot = dh // 4
        half = rot // 2
        inv = jnp.power(jnp.float32(ROPE_THETA), -jnp.arange(half, dtype=F32) * (2.0 / rot))
        ang = posf * inv[None, :]
        cos, sin = jnp.cos(ang), jnp.sin(ang)
        ones = jnp.ones((pos.shape[0], dh - rot), F32)
        c = jnp.concatenate([cos, cos, ones], axis=1)
        s = jnp.concatenate([-sin, sin, 0.0 * ones], axis=1)
        reps = LANES // dh
        return jnp.tile(c, (1, reps)), jnp.tile(s, (1, reps))

    c128, s128 = tables(HEAD_DIM)
    c64, s64 = tables(IDX_DIM)
    return c128, s128, c64, s64


def _permute_w_in(w_in):
    sizes = (A_Q_COLS, A_KV_COLS, A_KV_COLS, I_Q_COLS, IDX_HEADS, IDX_DIM, B_Q_COLS, B_KV_COLS, B_KV_COLS)
    offs = [0]
    for s in sizes:
        offs.append(offs[-1] + s)
    qa, ka, va, qi, wi, ki, qb, kb, vb = [w_in[:, offs[n]:offs[n + 1]] for n in range(len(sizes))]
    pad = jnp.zeros((w_in.shape[0], PROJ_TN - IDX_DIM - IDX_HEADS), w_in.dtype)
    return jnp.concatenate([qa, ka, va, qi, qb, kb, vb, ki, wi, pad], axis=1).astype(BF16)


TOPK_GROUPS = 256


def _sort_key(score):
    score = jnp.where(score == 0.0, 0.0, score)
    bits = pltpu.bitcast(score, I32)
    return bits ^ ((bits >> 31) & 0x7FFFFFFF)


def _fold_lanes(op, x, width):
    parts = [x[:, u * width:(u + 1) * width] for u in range(x.shape[1] // width)]
    while len(parts) > 1:
        nxt = [op(parts[u], parts[u + 1]) for u in range(0, len(parts) - 1, 2)]
        if len(parts) % 2:
            nxt.append(parts[-1])
        parts = nxt
    return parts[0]


def _lane_allreduce(op, x):
    shift = LANES // 2
    while shift >= 1:
        x = op(x, pltpu.roll(x, shift, axis=1))
        shift //= 2
    return x


def _count(keys_ref, nch, cw, pred):
    rows = keys_ref.shape[0]

    def body(c, acc):
        off = pl.multiple_of(c * cw, cw)
        blk = keys_ref[:, pl.ds(off, cw)]
        idx = off + lax.broadcasted_iota(I32, blk.shape, 1)
        return acc + _fold_lanes(jnp.add, jnp.where(pred(blk, idx), 1.0, 0.0), LANES)

    acc = lax.fori_loop(0, nch, body, jnp.zeros((rows, LANES), F32))
    return jnp.sum(acc, axis=1, keepdims=True)


def _row_bounds(keys_ref, nch, cw):
    rows = keys_ref.shape[0]

    def body(c, gmax):
        off = pl.multiple_of(c * cw, cw)
        return jnp.maximum(gmax, _fold_lanes(jnp.maximum, keys_ref[:, pl.ds(off, cw)], TOPK_GROUPS))

    gmax = lax.fori_loop(0, nch, body, jnp.full((rows, TOPK_GROUPS), INT_MIN, I32))
    lo = _lane_allreduce(jnp.minimum, _fold_lanes(jnp.minimum, gmax, LANES))[:, :1]
    top = _lane_allreduce(jnp.maximum, _fold_lanes(jnp.maximum, gmax, LANES))[:, :1]
    return lo, top


def _select_threshold(keys_ref, nch, cw, topk, idx_bits):
    rows = keys_ref.shape[0]
    kf = float(topk)
    lo, top = _row_bounds(keys_ref, nch, cw)
    hi = top + 1

    def n_open(lo, hi):
        return jnp.max(jnp.where(hi - 1 > lo, 1.0, 0.0))

    def cond(st):
        return (st[2] > 0.5) & (st[3] < 40)

    def body(st):
        lo, hi, _, it = st
        mid = (lo & hi) + ((lo ^ hi) >> 1)
        cnt = _count(keys_ref, nch, cw, lambda k, i: k >= mid)
        ge = cnt >= kf
        lo = jnp.where(ge, mid, lo)
        hi = jnp.where(cnt == kf, mid + 1, jnp.where(ge, hi, mid))
        return lo, hi, n_open(lo, hi), it + 1

    tau, _, _, _ = lax.while_loop(cond, body, (lo, hi, n_open(lo, hi), jnp.int32(0)))
    n_ge = _count(keys_ref, nch, cw, lambda k, i: k >= tau)
    has_k = tau > INT_MIN
    tied = has_k & (n_ge > kf)
    any_tied = jnp.max(jnp.where(tied, 1.0, 0.0)) > 0.5

    def resolve(_):
        need = kf - _count(keys_ref, nch, cw, lambda k, i: k > tau)
        lim = jnp.zeros((rows, 1), I32)
        for b in range(idx_bits - 1, -1, -1):
            cand = lim + (1 << b)
            cnt = _count(keys_ref, nch, cw, lambda k, i: (k == tau) & (i < cand))
            lim = jnp.where(cnt < need, cand, lim)
        return lim

    lim = lax.cond(any_tied, resolve, lambda _: jnp.zeros((rows, 1), I32), 0)
    jlim = jnp.where(tied, lim, jnp.where(has_k, jnp.int32(2 ** 31 - 1), jnp.int32(-1)))
    return tau, jlim


def _keys_to_bias(keys_ref, bias_ref, nch, cw, tau, jlim):
    def body(c, carry):
        off = pl.multiple_of(c * cw, cw)
        key = keys_ref[:, pl.ds(off, cw)]
        idx = off + lax.broadcasted_iota(I32, key.shape, 1)
        sel = (key > tau) | ((key == tau) & (idx <= jlim))
        bias_ref[:, pl.ds(off, cw)] = pltpu.bitcast(jnp.where(sel, 0.0, NEG), I32)
        return carry

    lax.fori_loop(0, nch, body, 0)


def _flash_update(sl, d, v, m_scr, acc_scr):
    reps = d.shape[1] // LANES
    m_old = m_scr[sl]
    m_new = jnp.maximum(m_old, jnp.max(d, axis=1, keepdims=True))
    alpha = jnp.exp2(m_old - m_new)
    p = jnp.exp2(d - jnp.tile(m_new, (1, reps)))
    v_ones = jnp.concatenate([v, jnp.ones(v.shape, BF16)], axis=1)
    acc_scr[sl] = jnp.tile(alpha, (1, 2)) * acc_scr[sl] + jnp.dot(p.astype(BF16), v_ones,
                                                                   preferred_element_type=F32)
    m_scr[sl] = m_new


def _flash_init(m_scr, acc_scr):
    m_scr[...] = jnp.full(m_scr.shape, NEG, F32)
    acc_scr[...] = jnp.zeros(acc_scr.shape, F32)


def _flash_result(acc, guard=False):
    l = acc[:, HEAD_DIM:]
    return acc[:, :HEAD_DIM] / (jnp.maximum(l, 1e-30) if guard else l)


def _causal_steps(n_q, chunks_of):
    qb, cb = [], []
    for i in range(n_q):
        for c in range(chunks_of(i)):
            qb.append(i)
            cb.append(c)
    return jnp.asarray(qb, I32), jnp.asarray(cb, I32)


IDX_TN = 1024


def _dsa_prompt_kernel(qb_ref, cb_ref, qi_ref, misc_ref, kit_ref, qa_ref, k_ref, v_ref, o_ref,
                       keys_scr, qih_scr, wib_scr, d_scr, m_scr, acc_scr,
                       *, tq, tk, topk, idx_bits):
    step = pl.program_id(0)
    i = qb_ref[step]
    c = cb_ref[step]
    t0 = i * tq
    nch = (t0 + tq + tk - 1) // tk
    rt = 16
    n_chunks = keys_scr.shape[1] // tk
    scan = max(w for w in (1, 2) if n_chunks % w == 0)

    @pl.when(c == 0)
    def _():
        for hh in range(IDX_HEADS):
            qih_scr[hh] = qi_ref[:, hh * IDX_DIM:(hh + 1) * IDX_DIM]
            wib_scr[hh] = jnp.broadcast_to(misc_ref[:, MISC_WI + hh:MISC_WI + hh + 1], (tq, LANES))
        qpos = t0 + lax.broadcasted_iota(I32, (tq, 1), 0)

        def sub_body(kc, carry):
            off = pl.multiple_of(kc * IDX_TN, IDX_TN)
            d_scr[...] = jnp.dot(qih_scr[...].reshape(IDX_HEADS * tq, IDX_DIM),
                                 kit_ref[:, pl.ds(off, IDX_TN)], preferred_element_type=F32)
            kpos = off + lax.broadcasted_iota(I32, (rt, IDX_TN), 1)
            for r in range(tq // rt):
                score = jnp.zeros((rt, IDX_TN), F32)
                for hh in range(IDX_HEADS):
                    w = wib_scr[hh, r * rt:(r + 1) * rt, :]
                    dd = d_scr[hh * tq + r * rt:hh * tq + (r + 1) * rt, :]
                    score = score + jnp.tile(w, (1, IDX_TN // LANES)) * jnp.maximum(dd, 0.0)
                keys_scr[r * rt:(r + 1) * rt, pl.ds(off, IDX_TN)] = jnp.where(
                    kpos <= qpos[r * rt:(r + 1) * rt], _sort_key(score), INT_MIN)
            return carry

        lax.fori_loop(0, nch * (tk // IDX_TN), sub_body, 0)

        n_scan = (nch + scan - 1) // scan

        def pad_body(kc, carry):
            keys_scr[:, pl.ds(pl.multiple_of(kc * tk, tk), tk)] = jnp.full((tq, tk), INT_MIN, I32)
            return carry

        lax.fori_loop(nch, n_scan * scan, pad_body, 0)
        tau, jl = _select_threshold(keys_scr, n_scan, scan * tk, topk, idx_bits)
        _keys_to_bias(keys_scr, keys_scr, n_scan, scan * tk, tau, jl)
        _flash_init(m_scr, acc_scr)

    bias = pltpu.bitcast(keys_scr[:, pl.ds(pl.multiple_of(c * tk, tk), tk)], F32)
    group = A_HEADS // A_KV_HEADS

    def scores(h):
        kv = h // group
        return _nt_dot(qa_ref[:, h * HEAD_DIM:(h + 1) * HEAD_DIM],
                       k_ref[:, kv * HEAD_DIM:(kv + 1) * HEAD_DIM]) + bias

    d_next = scores(0)
    for h in range(A_HEADS):
        d = d_next
        if h + 1 < A_HEADS:
            d_next = scores(h + 1)
        kv = h // group
        _flash_update(h, d, v_ref[:, kv * HEAD_DIM:(kv + 1) * HEAD_DIM], m_scr, acc_scr)

    @pl.when(c == nch - 1)
    def _():
        for h in range(A_HEADS):
            o_ref[:, h * HEAD_DIM:(h + 1) * HEAD_DIM] = _flash_result(acc_scr[h]).astype(o_ref.dtype)


def _dsa_prompt(slab, misc, kit, tq, tk):
    rows = slab.shape[0]
    topk = min(DSA_TOPK, rows // 4)
    qb, cb = _causal_steps(rows // tq, lambda i: (i * tq + tq + tk - 1) // tk)
    kern = functools.partial(_dsa_prompt_kernel, tq=tq, tk=tk, topk=topk,
                             idx_bits=max(1, (rows - 1).bit_length()))
    grid_spec = pltpu.PrefetchScalarGridSpec(
        num_scalar_prefetch=2,
        grid=(int(qb.shape[0]),),
        in_specs=[pl.BlockSpec((tq, I_Q_COLS), lambda s, qb, cb: (qb[s], COL_QI // I_Q_COLS)),
                  pl.BlockSpec((tq, LANES), lambda s, qb, cb: (qb[s], 0)),
                  pl.BlockSpec((IDX_DIM, rows), lambda s, qb, cb: (0, 0)),
                  pl.BlockSpec((tq, A_Q_COLS), lambda s, qb, cb: (qb[s], COL_QA // A_Q_COLS)),
                  pl.BlockSpec((tk, A_KV_COLS), lambda s, qb, cb: (cb[s], COL_KA // A_KV_COLS)),
                  pl.BlockSpec((tk, A_KV_COLS), lambda s, qb, cb: (cb[s], COL_VA // A_KV_COLS))],
        out_specs=pl.BlockSpec((tq, A_Q_COLS), lambda s, qb, cb: (qb[s], 0)),
        scratch_shapes=[pltpu.VMEM((tq, rows), I32),
                        pltpu.VMEM((IDX_HEADS, tq, IDX_DIM), BF16),
                        pltpu.VMEM((IDX_HEADS, tq, LANES), F32),
                        pltpu.VMEM((IDX_HEADS * tq, IDX_TN), F32),
                        pltpu.VMEM((A_HEADS, tq, LANES), F32),
                        pltpu.VMEM((A_HEADS, tq, 2 * HEAD_DIM), F32)])
    return pl.pallas_call(
        kern, grid_spec=grid_spec, name="dsa_prompt",
        out_shape=jax.ShapeDtypeStruct((rows, A_Q_COLS), BF16),
        compiler_params=pltpu.CompilerParams(
            dimension_semantics=("arbitrary",), vmem_limit_bytes=VMEM_LIMIT),
    )(qb, cb, slab, misc, kit, slab, slab, slab)


def _gate_topk(gate, n_valid, ksel):
    lane = lax.broadcasted_iota(I32, gate.shape, 1)
    lanef = lane.astype(F32)
    gate = jnp.where(lane < n_valid, gate, NEG)
    sel = jnp.zeros(gate.shape, F32)
    for _ in range(ksel):
        mx = jnp.max(gate, axis=1, keepdims=True)
        first = jnp.min(jnp.where(gate == mx, lanef, 1e9), axis=1, keepdims=True)
        hit = (lanef == first) & (mx > 0.5 * NEG)
        sel = jnp.where(hit, 1.0, sel)
        gate = jnp.where(hit, NEG, gate)
    return sel


def _block_bias_rows(sel):
    return ((sel - 1.0) * (-NEG)).astype(BF16)


def _block_expansion(n_keys, blk0, keys_on_rows):
    shape = (n_keys, LANES) if keys_on_rows else (LANES, n_keys)
    kdim, bdim = (0, 1) if keys_on_rows else (1, 0)
    kblk = lax.broadcasted_iota(I32, shape, kdim) // MOBA_BLOCK + blk0
    return jnp.where(kblk == lax.broadcasted_iota(I32, shape, bdim), 1.0, 0.0).astype(BF16)


def _block_means_kernel(k_ref, o_ref):
    x = k_ref[...]
    s8 = jnp.sum(x.reshape(x.shape[0] // SUBLANES, SUBLANES, HEAD_DIM), axis=0)
    o_ref[0] = (s8[:B_KV_HEADS] + s8[B_KV_HEADS:]) * (1.0 / MOBA_BLOCK)


def _block_means(kb_rows):
    blk_rows = MOBA_BLOCK * B_KV_HEADS
    nb = kb_rows.shape[0] // blk_rows
    out = pl.pallas_call(
        _block_means_kernel,
        grid=(nb,),
        in_specs=[pl.BlockSpec((blk_rows, HEAD_DIM), lambda n: (n, 0))],
        out_specs=pl.BlockSpec((1, B_KV_HEADS, HEAD_DIM), lambda n: (n, 0, 0)),
        out_shape=jax.ShapeDtypeStruct((nb, B_KV_HEADS, HEAD_DIM), F32),
        name="block_means",
    )(kb_rows)
    return out.reshape(nb, B_KV_COLS)


def _moba_prompt_kernel(qb_ref, cb_ref, q_ref, means_ref, k_ref, v_ref, o_ref,
                        qaug_scr, m_scr, acc_scr, *, tq, tk, ksel):
    step = pl.program_id(0)
    i = qb_ref[step]
    c = cb_ref[step]
    t0 = i * tq
    qblk = t0 // MOBA_BLOCK
    bpc = tk // MOBA_BLOCK
    nch = qblk // bpc + 1
    group = B_HEADS // B_KV_HEADS

    @pl.when(c == 0)
    def _():
        lane = lax.broadcasted_iota(I32, (tq, LANES), 1)
        for h in range(B_HEADS):
            kv = h // group
            q = q_ref[:, h * HEAD_DIM:(h + 1) * HEAD_DIM]
            gate = _nt_dot(q, means_ref[:, kv * HEAD_DIM:(kv + 1) * HEAD_DIM])
            sel = jnp.where(lane == qblk, 1.0, _gate_topk(gate, qblk, ksel))
            qaug_scr[h] = jnp.concatenate([q, _block_bias_rows(sel)], axis=1)
        _flash_init(m_scr, acc_scr)

    e_t = _block_expansion(tk, c * bpc, keys_on_rows=True)
    qlim = t0 + lax.broadcasted_iota(I32, (tq, 1), 0) + jnp.where(c == nch - 1, 0, 2 ** 30)
    kpos = c * tk + lax.broadcasted_iota(I32, (tq, tk), 1)
    causal = jnp.where(kpos <= qlim, 0.0, NEG)
    for kv in range(B_KV_HEADS):
        rhs = jnp.concatenate([k_ref[:, kv * HEAD_DIM:(kv + 1) * HEAD_DIM], e_t], axis=1)
        vv = v_ref[:, kv * HEAD_DIM:(kv + 1) * HEAD_DIM]
        for g in range(group):
            h = kv * group + g
            _flash_update(h, _nt_dot(qaug_scr[h], rhs) + causal, vv, m_scr, acc_scr)

    @pl.when(c == nch - 1)
    def _():
        for h in range(B_HEADS):
            o_ref[:, h * HEAD_DIM:(h + 1) * HEAD_DIM] = _flash_result(acc_scr[h]).astype(o_ref.dtype)


def _moba_prompt(slab, means_pad, tq, tk):
    rows = slab.shape[0]
    ksel = min(MOBA_TOPK, (rows - 1) // MOBA_BLOCK)
    assert tq == MOBA_BLOCK and tk % MOBA_BLOCK == 0 and means_pad.shape[0] == LANES
    qb, cb = _causal_steps(rows // tq, lambda i: (i * tq) // tk + 1)
    kern = functools.partial(_moba_prompt_kernel, tq=tq, tk=tk, ksel=ksel)
    grid_spec = pltpu.PrefetchScalarGridSpec(
        num_scalar_prefetch=2,
        grid=(int(qb.shape[0]),),
        in_specs=[pl.BlockSpec((tq, B_Q_COLS), lambda s, qb, cb: (qb[s], COL_QB // B_Q_COLS)),
                  pl.BlockSpec((LANES, B_KV_COLS), lambda s, qb, cb: (0, 0)),
                  pl.BlockSpec((tk, B_KV_COLS), lambda s, qb, cb: (cb[s], COL_KB // B_KV_COLS)),
                  pl.BlockSpec((tk, B_KV_COLS), lambda s, qb, cb: (cb[s], COL_VB // B_KV_COLS))],
        out_specs=pl.BlockSpec((tq, B_Q_COLS), lambda s, qb, cb: (qb[s], 0)),
        scratch_shapes=[pltpu.VMEM((B_HEADS, tq, 2 * HEAD_DIM), BF16),
                        pltpu.VMEM((B_HEADS, tq, LANES), F32),
                        pltpu.VMEM((B_HEADS, tq, 2 * HEAD_DIM), F32)])
    return pl.pallas_call(
        kern, grid_spec=grid_spec, name="moba_prompt",
        out_shape=jax.ShapeDtypeStruct((rows, B_Q_COLS), BF16),
        compiler_params=pltpu.CompilerParams(
            dimension_semantics=("arbitrary",), vmem_limit_bytes=VMEM_LIMIT),
    )(qb, cb, slab, means_pad, slab, slab)


TPAD = SUBLANES
SROWS = 8 * TPAD
KV_ROWS = SROWS // 4
PAGE_ROWS = PAGE_SIZE * 4


def _page_head(ref, kv):
    return ref[0, pl.ds(kv, PAGE_SIZE, stride=4), :]


def _paged_spec(shape, n_pages, per_step, first_step, n_steps, j):
    def index(b, s, pt):
        local = jnp.clip(s - first_step, 0, n_steps - 1)
        return (pt[b * n_pages + local * per_step + j], 0, 0)
    return pl.BlockSpec(shape, index)


def _dsa_sample_index_kernel(pt_ref, qi_ref, wi_ref, *rest, n_pages, pa, t_new):
    kidx_refs = rest[:pa]
    kin_ref, keys_ref, wib_scr = rest[pa:]
    keys_scr = keys_ref.at[0]
    s = pl.program_id(1)
    na = n_pages // pa
    past = n_pages * PAGE_SIZE
    ntot = keys_scr.shape[1]
    trow = lax.broadcasted_iota(I32, (TPAD, 1), 0)

    def index_keys(kd, off):
        width = kd.shape[1]
        x = jnp.tile(wib_scr[...], (1, width // LANES)) * jnp.maximum(
            jnp.dot(qi_ref[0], kd, preferred_element_type=F32), 0.0)
        score = x[:TPAD]
        for hh in range(1, IDX_HEADS):
            score = score + x[hh * TPAD:(hh + 1) * TPAD]
        kpos = off + lax.broadcasted_iota(I32, (TPAD, width), 1)
        visible = (kpos <= past + trow) & (trow < t_new)
        keys_scr[:, pl.ds(off, width)] = jnp.where(visible, _sort_key(score), INT_MIN)

    @pl.when(s == 0)
    def _():
        wib_scr[...] = jnp.broadcast_to(wi_ref[0], wib_scr.shape)
        keys_scr[:, past:] = jnp.full((TPAD, ntot - past), INT_MIN, I32)

    @pl.when(s < na)
    def _():
        kd = jnp.concatenate([r[0].astype(BF16) for r in kidx_refs], axis=1)
        index_keys(kd, pl.multiple_of(s * (pa * PAGE_SIZE), pa * PAGE_SIZE))

    @pl.when(s == na - 1)
    def _():
        index_keys(kin_ref[0], past)


def _topk_bias_kernel(keys_ref, bias_ref, *, cw, topk, idx_bits):
    nch = keys_ref.shape[1] // cw
    tau, jl = _select_threshold(keys_ref, nch, cw, topk, idx_bits)
    _keys_to_bias(keys_ref, bias_ref, nch, cw, tau, jl)


def _dsa_sample_attend_kernel(pt_ref, bias_ref, qa_ref, *rest, n_pages, pb):
    k_refs, v_refs = rest[:pb], rest[pb:2 * pb]
    kn_ref, vn_ref, o_ref, m_scr, acc_scr = rest[2 * pb:]
    keys_scr = bias_ref.at[0]
    s = pl.program_id(1)
    past = n_pages * PAGE_SIZE

    @pl.when(s == 0)
    def _():
        _flash_init(m_scr, acc_scr)

    def attend(key_of, val_of, off, width):
        bias = pltpu.bitcast(keys_scr[:, pl.ds(off, width)], F32)
        bias = jnp.concatenate([bias] * (KV_ROWS // TPAD), axis=0)
        for kv in range(A_KV_HEADS):
            rows = slice(kv * KV_ROWS, (kv + 1) * KV_ROWS)
            _flash_update(rows, _nt_dot(qa_ref[0, rows], key_of(kv)) + bias, val_of(kv), m_scr, acc_scr)

    def gather(refs, kv):
        return jnp.concatenate([_page_head(r, kv) for r in refs], axis=0).astype(BF16)

    attend(lambda kv: gather(k_refs, kv), lambda kv: gather(v_refs, kv),
           pl.multiple_of(s * (pb * PAGE_SIZE), pb * PAGE_SIZE), pb * PAGE_SIZE)

    @pl.when(s == n_pages // pb - 1)
    def _():
        attend(lambda kv: kn_ref[0, kv], lambda kv: vn_ref[0, kv], past, PAGE_SIZE)
        o_ref[0] = _flash_result(acc_scr[...], guard=True)


def _dsa_sample(pt_flat, n_pages, t_new, qi_s, wi_s, kidx_pages, ki_new_t, qa_s, k_pages, v_pages, ka_new, va_new,
                pa, pb):
    ns = qi_s.shape[0]
    past = n_pages * PAGE_SIZE
    topk = min(DSA_TOPK, (past + t_new) // 4)
    cw = 1024
    ntot = -(-(past + PAGE_SIZE) // cw) * cw
    na, nb = n_pages // pa, n_pages // pb
    seq3 = lambda s1, s2: pl.BlockSpec((1, s1, s2), lambda b, s, pt: (b, 0, 0))
    seq4 = pl.BlockSpec((1, A_KV_HEADS, PAGE_SIZE, HEAD_DIM), lambda b, s, pt: (b, 0, 0, 0))
    params = pltpu.CompilerParams(dimension_semantics=("arbitrary", "arbitrary"), vmem_limit_bytes=VMEM_LIMIT)

    keys = pl.pallas_call(
        functools.partial(_dsa_sample_index_kernel, n_pages=n_pages, pa=pa, t_new=t_new),
        grid_spec=pltpu.PrefetchScalarGridSpec(
            num_scalar_prefetch=1,
            grid=(ns, na),
            in_specs=[seq3(IDX_HEADS * TPAD, IDX_DIM), seq3(IDX_HEADS * TPAD, 1)]
                     + [_paged_spec((1, IDX_DIM, PAGE_SIZE), n_pages, pa, 0, na, j) for j in range(pa)]
                     + [seq3(IDX_DIM, PAGE_SIZE)],
            out_specs=seq3(TPAD, ntot),
            scratch_shapes=[pltpu.VMEM((IDX_HEADS * TPAD, LANES), F32)]),
        out_shape=jax.ShapeDtypeStruct((ns, TPAD, ntot), I32),
        name="dsa_sample_index", compiler_params=params,
    )(pt_flat, qi_s, wi_s, *([kidx_pages] * pa), ki_new_t)

    rows = ns * t_new
    rows_pad = -(-rows // SUBLANES) * SUBLANES
    keys2d = _pad_rows(keys[:, :t_new].reshape(rows, ntot), 0, rows_pad)
    bias2d = pl.pallas_call(
        functools.partial(_topk_bias_kernel, cw=cw, topk=topk, idx_bits=max(1, (ntot - 1).bit_length())),
        out_shape=jax.ShapeDtypeStruct((rows_pad, ntot), I32),
        name="dsa_sample_select", compiler_params=pltpu.CompilerParams(vmem_limit_bytes=VMEM_LIMIT),
    )(keys2d)
    bias = _pad_rows(bias2d[:rows].reshape(ns, t_new, ntot), 1, TPAD)

    page_kv = lambda j: _paged_spec((1, PAGE_ROWS, HEAD_DIM), n_pages, pb, 0, nb, j)
    return pl.pallas_call(
        functools.partial(_dsa_sample_attend_kernel, n_pages=n_pages, pb=pb),
        grid_spec=pltpu.PrefetchScalarGridSpec(
            num_scalar_prefetch=1,
            grid=(ns, nb),
            in_specs=[seq3(TPAD, ntot), seq3(SROWS, HEAD_DIM)]
                     + [page_kv(j) for j in range(pb)] + [page_kv(j) for j in range(pb)]
                     + [seq4, seq4],
            out_specs=seq3(SROWS, HEAD_DIM),
            scratch_shapes=[pltpu.VMEM((SROWS, LANES), F32),
                            pltpu.VMEM((SROWS, 2 * HEAD_DIM), F32)]),
        out_shape=jax.ShapeDtypeStruct((ns, SROWS, HEAD_DIM), F32),
        name="dsa_sample_attend", compiler_params=params,
    )(pt_flat, bias, qa_s, *([k_pages] * pb), *([v_pages] * pb), ka_new, va_new)


def _moba_sample_kernel(pt_ref, q_ref, *rest, n_pages, pb, t_new, ksel):
    k_refs, v_refs = rest[:pb], rest[pb:2 * pb]
    kn_ref, vn_ref, o_ref, bsum_scr, s_scr, seln_scr, m_scr, acc_scr = rest[2 * pb:]
    s = pl.program_id(1)
    nst = n_pages // pb
    past = n_pages * PAGE_SIZE
    nbp = past // MOBA_BLOCK
    bps = pb * PAGE_SIZE // MOBA_BLOCK
    width = pb * PAGE_SIZE
    tok = lax.broadcasted_iota(I32, (KV_ROWS, 1), 0) % TPAD
    pad_bias = jnp.where(tok < t_new, 0.0, NEG)
    kv_rows = lambda kv: slice(kv * KV_ROWS, (kv + 1) * KV_ROWS)

    @pl.when(s == 0)
    def _():
        bsum_scr[...] = jnp.zeros(bsum_scr.shape, F32)

    @pl.when(s < nst)
    def _():
        off = pl.multiple_of(s * width, width)
        for kv in range(B_KV_HEADS):
            kf = jnp.concatenate([_page_head(r, kv) for r in k_refs], axis=0)
            bsum_scr[kv, pl.ds(pl.multiple_of(s * bps, bps), bps), :] = jnp.sum(
                kf.reshape(bps, MOBA_BLOCK, HEAD_DIM), axis=1)
            s_scr[kv_rows(kv), pl.ds(off, width)] = _nt_dot(q_ref[0, kv_rows(kv)], kf.astype(BF16))

    @pl.when(s == nst - 1)
    def _():
        for kv in range(B_KV_HEADS):
            q = q_ref[0, kv_rows(kv)]
            s_scr[kv_rows(kv), pl.ds(past, PAGE_SIZE)] = _nt_dot(q, kn_ref[0, kv])
            means = (bsum_scr[kv] * (1.0 / MOBA_BLOCK)).astype(BF16)
            seln_scr[kv_rows(kv)] = _block_bias_rows(_gate_topk(_nt_dot(q, means), nbp, ksel))
        _flash_init(m_scr, acc_scr)

    @pl.when(s >= nst)
    def _():
        sb = s - nst
        off = pl.multiple_of(sb * width, width)
        expand = _block_expansion(width, sb * bps, keys_on_rows=False)
        for kv in range(B_KV_HEADS):
            bias = jnp.dot(seln_scr[kv_rows(kv)], expand, preferred_element_type=F32) + pad_bias
            vv = jnp.concatenate([_page_head(r, kv) for r in v_refs], axis=0).astype(BF16)
            _flash_update(kv_rows(kv), s_scr[kv_rows(kv), pl.ds(off, width)] + bias, vv, m_scr, acc_scr)

    @pl.when(s == 2 * nst - 1)
    def _():
        kpos = past + lax.broadcasted_iota(I32, (KV_ROWS, PAGE_SIZE), 1)
        bias = jnp.where(kpos <= past + tok, 0.0, NEG) + pad_bias
        for kv in range(B_KV_HEADS):
            _flash_update(kv_rows(kv), s_scr[kv_rows(kv), pl.ds(past, PAGE_SIZE)] + bias, vn_ref[0, kv],
                          m_scr, acc_scr)
        o_ref[0] = _flash_result(acc_scr[...], guard=True)


def _moba_sample(pt_flat, n_pages, t_new, qb_s, k_pages, v_pages, kb_new, vb_new, pb):
    ns = qb_s.shape[0]
    past = n_pages * PAGE_SIZE
    nbp = past // MOBA_BLOCK
    assert nbp <= LANES and (pb * PAGE_SIZE // MOBA_BLOCK) % SUBLANES == 0
    ksel = min(MOBA_TOPK, nbp)
    nst = n_pages // pb
    seq3 = lambda s1, s2: pl.BlockSpec((1, s1, s2), lambda b, s, pt: (b, 0, 0))
    seq4 = pl.BlockSpec((1, B_KV_HEADS, PAGE_SIZE, HEAD_DIM), lambda b, s, pt: (b, 0, 0, 0))
    kern = functools.partial(_moba_sample_kernel, n_pages=n_pages, pb=pb, t_new=t_new, ksel=ksel)
    grid_spec = pltpu.PrefetchScalarGridSpec(
        num_scalar_prefetch=1,
        grid=(ns, 2 * nst),
        in_specs=[seq3(SROWS, HEAD_DIM)]
                 + [_paged_spec((1, PAGE_ROWS, HEAD_DIM), n_pages, pb, 0, nst, j) for j in range(pb)]
                 + [_paged_spec((1, PAGE_ROWS, HEAD_DIM), n_pages, pb, nst, nst, j) for j in range(pb)]
                 + [seq4, seq4],
        out_specs=seq3(SROWS, HEAD_DIM),
        scratch_shapes=[pltpu.VMEM((B_KV_HEADS, LANES, HEAD_DIM), F32),
                        pltpu.VMEM((SROWS, past + PAGE_SIZE), F32),
                        pltpu.VMEM((SROWS, LANES), BF16),
                        pltpu.VMEM((SROWS, LANES), F32),
                        pltpu.VMEM((SROWS, 2 * HEAD_DIM), F32)])
    return pl.pallas_call(
        kern, grid_spec=grid_spec, name="moba_sample",
        out_shape=jax.ShapeDtypeStruct((ns, SROWS, HEAD_DIM), F32),
        compiler_params=pltpu.CompilerParams(
            dimension_semantics=("arbitrary", "arbitrary"), vmem_limit_bytes=VMEM_LIMIT),
    )(pt_flat, qb_s, *([k_pages] * pb), *([v_pages] * pb), kb_new, vb_new)


def _memory_kv_kernel(mem_ref, g_ref, w_ref, k_ref, v_ref):
    h = _rms(mem_ref[...], g_ref[...]).astype(BF16)
    kv = jnp.dot(h, w_ref[...], preferred_element_type=F32)
    k_ref[...] = kv[:, :MEM_WIDTH]
    v_ref[...] = kv[:, MEM_WIDTH:]


def _memory_kv(mem, g, w_bf):
    m = mem.shape[0]
    return pl.pallas_call(
        _memory_kv_kernel,
        out_shape=[jax.ShapeDtypeStruct((m, MEM_WIDTH), F32), jax.ShapeDtypeStruct((m, MEM_WIDTH), F32)],
        name="memory_kv",
        compiler_params=pltpu.CompilerParams(vmem_limit_bytes=VMEM_LIMIT),
    )(mem, g, w_bf)


def _outproj_cross_kernel(x_ref, oa_ref, ob_ref, woa_ref, wob_ref, g_ref, wq_ref, mk_ref, mv_ref, wo_ref,
                          y_ref, *, rows_per_seq):
    x1 = x_ref[...] + jnp.dot(oa_ref[...], woa_ref[...], preferred_element_type=F32) \
        + jnp.dot(ob_ref[...], wob_ref[...], preferred_element_type=F32)
    hc = _rms(x1, g_ref[...]).astype(BF16)
    q = jnp.dot(hc, wq_ref[...], preferred_element_type=F32).astype(BF16)
    tm = q.shape[0]
    nk = mk_ref.shape[0]
    scale = MEM_HEAD_DIM ** -0.5
    if rows_per_seq is not None:
        rseq = (pl.program_id(0) * tm + lax.broadcasted_iota(I32, (tm, 1), 0)) // rows_per_seq
        kseq = lax.broadcasted_iota(I32, (tm, nk), 1) // MEM_TOKENS
        mask = kseq == rseq
    outs = []
    for h in range(MEM_HEADS):
        sl = slice(h * MEM_HEAD_DIM, (h + 1) * MEM_HEAD_DIM)
        s = _nt_dot(q[:, sl], mk_ref[:, sl]) * scale
        if rows_per_seq is not None:
            s = jnp.where(mask, s, NEG)
        m = jnp.max(s, axis=1, keepdims=True)
        e = jnp.exp(s - m)
        p = e / jnp.sum(e, axis=1, keepdims=True)
        outs.append(jnp.dot(p.astype(BF16), mv_ref[:, sl], preferred_element_type=F32))
    o = jnp.concatenate(outs, axis=1).astype(BF16)
    y_ref[...] = x1 + jnp.dot(o, wo_ref[...], preferred_element_type=F32)


def _outproj_cross(x, oa, ob, w_out_bf, g_cross, w_mq_bf, mk_bf, mv_bf, w_mo_bf, tm, rows_per_seq):
    rows = x.shape[0]
    nk = mk_bf.shape[0]
    row_spec = lambda w: pl.BlockSpec((tm, w), lambda i: (i, 0))
    full = lambda a, b: pl.BlockSpec((a, b), lambda i: (0, 0))
    kern = functools.partial(_outproj_cross_kernel, rows_per_seq=rows_per_seq)
    return pl.pallas_call(
        kern,
        grid=(rows // tm,),
        in_specs=[row_spec(D_MODEL), row_spec(A_Q_COLS), row_spec(B_Q_COLS),
                  pl.BlockSpec((A_Q_COLS, D_MODEL), lambda i: (0, 0)),
                  pl.BlockSpec((B_Q_COLS, D_MODEL), lambda i: (1, 0)),
                  full(1, D_MODEL), full(D_MODEL, MEM_WIDTH), full(nk, MEM_WIDTH), full(nk, MEM_WIDTH),
                  full(MEM_WIDTH, D_MODEL)],
        out_specs=row_spec(D_MODEL),
        out_shape=jax.ShapeDtypeStruct((rows, D_MODEL), F32),
        name="outproj_cross",
        compiler_params=pltpu.CompilerParams(
            dimension_semantics=("parallel",), vmem_limit_bytes=VMEM_LIMIT),
    )(x, oa, ob, w_out_bf, w_out_bf, g_cross, w_mq_bf, mk_bf, mv_bf, w_mo_bf)


def _mlp_final_kernel(x_ref, g_ref, wu_ref, wd_ref, gf_ref, y_ref, h_scr, acc_scr):
    f = pl.program_id(1)

    @pl.when(f == 0)
    def _():
        h_scr[...] = _rms(x_ref[...], g_ref[...]).astype(BF16)
        acc_scr[...] = jnp.zeros(acc_scr.shape, F32)

    u = jnp.maximum(jnp.dot(h_scr[...], wu_ref[...], preferred_element_type=F32), 0.0)
    acc_scr[...] += jnp.dot((u * u).astype(BF16), wd_ref[...], preferred_element_type=F32)

    @pl.when(f == pl.num_programs(1) - 1)
    def _():
        y_ref[...] = _rms(x_ref[...] + acc_scr[...], gf_ref[...])


def _mlp_final(x, g_ffn, w_up_bf, w_down_bf, g_final, tm, tf):
    rows = x.shape[0]
    d_ff = w_up_bf.shape[1]
    return pl.pallas_call(
        _mlp_final_kernel,
        grid=(rows // tm, d_ff // tf),
        in_specs=[pl.BlockSpec((tm, D_MODEL), lambda i, f: (i, 0)),
                  pl.BlockSpec((1, D_MODEL), lambda i, f: (0, 0)),
                  pl.BlockSpec((D_MODEL, tf), lambda i, f: (0, f)),
                  pl.BlockSpec((tf, D_MODEL), lambda i, f: (f, 0)),
                  pl.BlockSpec((1, D_MODEL), lambda i, f: (0, 0))],
        out_specs=pl.BlockSpec((tm, D_MODEL), lambda i, f: (i, 0)),
        out_shape=jax.ShapeDtypeStruct((rows, D_MODEL), F32),
        scratch_shapes=[pltpu.VMEM((tm, D_MODEL), BF16), pltpu.VMEM((tm, D_MODEL), F32)],
        name="mlp_final",
        compiler_params=pltpu.CompilerParams(
            dimension_semantics=("parallel", "arbitrary"), vmem_limit_bytes=VMEM_LIMIT),
    )(x, g_ffn, w_up_bf, w_down_bf, g_final)


def _row_tile(rows, want):
    t = min(rows, want)
    while rows % t:
        t //= 2
    return t


def _pad_rows(a, axis, size):
    pad = [(0, 0)] * a.ndim
    pad[axis] = (0, size - a.shape[axis])
    return jnp.pad(a, pad)


def _heads_first(a, ns, t_new, heads, dim, t_pad):
    a = a.reshape(ns, t_new, heads, dim).transpose(0, 2, 1, 3)
    return _pad_rows(a, 2, t_pad)


def kernel(x_prompt, x_sample, cache_k_a, cache_v_a, cache_kidx, cache_k_b, cache_v_b, cache_mem_k,
           cache_mem_v, page_table, mem_prompt, g_mix, w_in, g_kidx, b_kidx, w_out, g_cross, w_mq, g_mem,
           w_mkv, w_mo, g_ffn, w_up, w_down, g_final):
    batch, s_len, _ = x_prompt.shape
    ns, t_new, _ = x_sample.shape
    depth = w_in.shape[0]
    n_pages = page_table.shape[1]
    past = n_pages * PAGE_SIZE
    n_phys = cache_k_a.shape[1]
    assert batch == 1 and depth == 1
    assert s_len % 1024 == 0 and s_len // MOBA_BLOCK <= LANES
    assert past % MOBA_BLOCK == 0 and t_new <= TPAD
    l = 0
    row = lambda v: v.reshape(1, -1)

    w_in_p = _permute_w_in(w_in[l])
    gk = row(jnp.concatenate([g_kidx[l], jnp.zeros((LANES - IDX_DIM,), F32)]))
    bk = row(jnp.concatenate([b_kidx[l], jnp.zeros((LANES - IDX_DIM,), F32)]))
    w_out_bf = w_out[l].astype(BF16)
    w_mq_bf = w_mq[l].astype(BF16)
    w_mo_bf = w_mo[l].astype(BF16)
    w_up_bf = w_up[l].astype(BF16)
    w_down_bf = w_down[l].astype(BF16)

    xp = x_prompt.reshape(s_len, D_MODEL)
    tabs_p = _rope_tables(jnp.arange(s_len))
    slab, ka, va, kb, vb, misc, misc_bf = _proj(xp, row(g_mix[l]), w_in_p, tabs_p, gk, bk, _row_tile(s_len, 1024))
    kit = misc_bf[:, :IDX_DIM].T
    oa = _dsa_prompt(slab, misc, kit, tq=256, tk=1024)
    means_pad = _pad_rows(_block_means(kb).astype(BF16), 0, LANES)
    ob = _moba_prompt(slab, means_pad, tq=MOBA_BLOCK, tk=1024)
    mk, mv = _memory_kv(mem_prompt.reshape(MEM_TOKENS, D_MODEL), row(g_mem[l]), w_mkv[l].astype(BF16))
    xp2 = _outproj_cross(xp, oa, ob, w_out_bf, row(g_cross[l]), w_mq_bf, mk.astype(BF16), mv.astype(BF16),
                         w_mo_bf, _row_tile(s_len, 256), None)
    y_prompt = _mlp_final(xp2, row(g_ffn[l]), w_up_bf, w_down_bf, row(g_final), _row_tile(s_len, 512), 1024)

    rows_s = ns * t_new
    xs = x_sample.reshape(rows_s, D_MODEL)
    pos_s = jnp.tile(past + jnp.arange(t_new), ns)
    tabs_s = _rope_tables(pos_s)
    rows_pad = -(-rows_s // SUBLANES) * SUBLANES
    xs_pad = _pad_rows(xs, 0, rows_pad)
    tabs_s = tuple(_pad_rows(t, 0, rows_pad) for t in tabs_s)
    slab_s, ka_s, va_s, kb_s, vb_s, misc_s, _ = _proj(xs_pad, row(g_mix[l]), w_in_p, tabs_s, gk, bk,
                                                      _row_tile(rows_pad, 128))
    slab_s, misc_s = slab_s[:rows_s], misc_s[:rows_s]
    ka_s, va_s, kb_s, vb_s = [a[:rows_s * A_KV_HEADS] for a in (ka_s, va_s, kb_s, vb_s)]

    pt_flat = page_table.reshape(-1).astype(I32)
    index_pages, attend_pages = min(32, n_pages), min(16, n_pages)
    assert n_pages % index_pages == 0 and n_pages % attend_pages == 0
    page_view = lambda c: c[l].reshape(n_phys, PAGE_ROWS, HEAD_DIM)
    kidx_view = jnp.swapaxes(cache_kidx[l], 1, 2)
    cols = lambda c0, w: slab_s[:, c0:c0 + w]
    qi_s = _heads_first(cols(COL_QI, I_Q_COLS), ns, t_new, IDX_HEADS, IDX_DIM, TPAD)
    qi_s = qi_s.reshape(ns, IDX_HEADS * TPAD, IDX_DIM)
    wi_s = _heads_first(misc_s[:, MISC_WI:MISC_WI + IDX_HEADS], ns, t_new, IDX_HEADS, 1, TPAD)
    wi_s = wi_s.reshape(ns, IDX_HEADS * TPAD, 1)
    ki_new_t = _pad_rows(cols(COL_MISC, IDX_DIM).reshape(ns, t_new, IDX_DIM).transpose(0, 2, 1), 2, PAGE_SIZE)
    q_rows = lambda c0: _heads_first(cols(c0, A_Q_COLS), ns, t_new, A_HEADS, HEAD_DIM, TPAD).reshape(
        ns, SROWS, HEAD_DIM)
    new_kv = lambda c0: _heads_first(cols(c0, A_KV_COLS), ns, t_new, A_KV_HEADS, HEAD_DIM, PAGE_SIZE)
    oa_s = _dsa_sample(pt_flat, n_pages, t_new, qi_s, wi_s, kidx_view, ki_new_t, q_rows(COL_QA),
                       page_view(cache_k_a), page_view(cache_v_a), new_kv(COL_KA), new_kv(COL_VA),
                       index_pages, attend_pages)
    ob_s = _moba_sample(pt_flat, n_pages, t_new, q_rows(COL_QB), page_view(cache_k_b), page_view(cache_v_b),
                        new_kv(COL_KB), new_kv(COL_VB), attend_pages)
    tokens_first = lambda o: o.reshape(ns, A_HEADS, TPAD, HEAD_DIM)[:, :, :t_new].transpose(0, 2, 1, 3).reshape(
        rows_s, -1).astype(BF16)
    oa_s2 = _pad_rows(tokens_first(oa_s), 0, rows_pad)
    ob_s2 = _pad_rows(tokens_first(ob_s), 0, rows_pad)
    mk_s = cache_mem_k[l].reshape(ns * MEM_TOKENS, MEM_WIDTH).astype(BF16)
    mv_s = cache_mem_v[l].reshape(ns * MEM_TOKENS, MEM_WIDTH).astype(BF16)
    xs2 = _outproj_cross(xs_pad, oa_s2, ob_s2, w_out_bf, row(g_cross[l]), w_mq_bf, mk_s, mv_s, w_mo_bf,
                         rows_pad, t_new)
    y_sample = _mlp_final(xs2, row(g_ffn[l]), w_up_bf, w_down_bf, row(g_final), rows_pad, 512)[:rows_s]

    kv5 = lambda a, n, t: a.reshape(1, n, t, A_KV_HEADS, HEAD_DIM)
    return (y_prompt.reshape(batch, s_len, D_MODEL), y_sample.reshape(ns, t_new, D_MODEL),
            kv5(ka, batch, s_len), kv5(va, batch, s_len),
            misc[:, :IDX_DIM].reshape(1, batch, s_len, IDX_DIM),
            kv5(kb, batch, s_len), kv5(vb, batch, s_len),
            mk.reshape(1, batch, MEM_TOKENS, MEM_HEADS, MEM_HEAD_DIM),
            mv.reshape(1, batch, MEM_TOKENS, MEM_HEADS, MEM_HEAD_DIM),
            kv5(ka_s, ns, t_new), kv5(va_s, ns, t_new),
            misc_s[:, :IDX_DIM].reshape(1, ns, t_new, IDX_DIM),
            kv5(kb_s, ns, t_new), kv5(vb_s, ns, t_new))
```
